```python
import jax, jax.numpy as jnp
from jax import lax
import numpy as np

D_MODEL = 1024
BATCH = 4
SEQ = 4096
DEPTH = 2

GLA_HEADS = 4
GLA_DK = 32
GLA_DV = 64
GLA_LOWRANK = 16
GLA_GATE_TAU = 16.0
MOBA_HEADS = 8
MOBA_HD = 64
MOBA_BLOCK = 256
MOBA_TOPK = 3
MOBA_QCHUNK = 128
ROT_DIM = MOBA_HD // 4
ROPE_THETA = 500000.0
HGRN_HEADS = 4
HGRN_DK = 64
HGRN_DV = 64
LIN_CHUNK = 16
GLA_KW = GLA_HEADS * GLA_DK
GLA_VW = GLA_HEADS * GLA_DV
MOBA_W = MOBA_HEADS * MOBA_HD
HGRN_KW = HGRN_HEADS * HGRN_DK
HGRN_VW = HGRN_HEADS * HGRN_DV
MIX_WIDTH = GLA_VW + MOBA_W + HGRN_VW
IN_SPLITS = (GLA_KW, GLA_KW, GLA_VW, GLA_VW, GLA_LOWRANK,
             MOBA_W, MOBA_W, MOBA_W,
             HGRN_KW, HGRN_KW, HGRN_VW, HGRN_VW)
IN_COLS = sum(IN_SPLITS)
MOE_GROUPS = 4
MOE_EXPERTS_PER_GROUP = 4
MOE_N_EXPERTS = MOE_GROUPS * MOE_EXPERTS_PER_GROUP
MOE_TOPK = 2
MOE_FF = 512
NORM_EPS = 1e-6

kernel_name = 'hymba_style_gla_moba_hgrn2_hmoe'


def rms_norm(x, g):
    xf = x.astype(jnp.float32)
    y = xf * lax.rsqrt(jnp.mean(xf * xf, axis=-1, keepdims=True) + NORM_EPS)
    return (y * g.astype(jnp.float32)).astype(x.dtype)


def partial_rotary(x, pos):
    half = ROT_DIM // 2
    inv_freq = ROPE_THETA ** (-(jnp.arange(0, ROT_DIM, 2, dtype=jnp.float32) / ROT_DIM))
    ang = pos.astype(jnp.float32)[:, None] * inv_freq[None, :]
    cos = jnp.cos(ang)[None, :, None, :]
    sin = jnp.sin(ang)[None, :, None, :]
    x1 = x[..., :half].astype(jnp.float32)
    x2 = x[..., half:ROT_DIM].astype(jnp.float32)
    rot = jnp.concatenate([x1 * cos - x2 * sin, x2 * cos + x1 * sin], axis=-1)
    return jnp.concatenate([rot.astype(x.dtype), x[..., ROT_DIM:]], axis=-1)


def gated_linear_chunked(q, k, v, log_a):
    B, S, H, dk = q.shape
    dv = v.shape[-1]
    C = LIN_CHUNK
    N = S // C
    f32 = jnp.float32

    def chunk(t):
        return t.astype(f32).reshape(B, N, C, H, t.shape[-1]).transpose(0, 3, 1, 2, 4)

    q, k, v, log_a = chunk(q), chunk(k), chunk(v), chunk(log_a)
    b = jnp.cumsum(log_a, axis=3)
    b_last = b[:, :, :, -1:, :]
    causal = jnp.tril(jnp.ones((C, C), dtype=bool))[:, :, None]
    diff = b[:, :, :, :, None, :] - b[:, :, :, None, :, :]
    decay = jnp.exp(jnp.where(causal, diff, -jnp.inf))
    scores = jnp.einsum('bhnik,bhnjk,bhnijk->bhnij', q, k, decay)
    o = jnp.einsum('bhnij,bhnjv->bhniv', scores, v)
    chunk_kv = jnp.einsum('bhnck,bhncv->bhnkv', k * jnp.exp(b_last - b), v)
    chunk_decay = jnp.exp(b_last[:, :, :, 0, :])

    def step(state, inp):
        a, kv = inp
        return a[..., None] * state + kv, state

    _, states = lax.scan(step, jnp.zeros((B, H, dk, dv), f32),
                         (jnp.moveaxis(chunk_decay, 2, 0), jnp.moveaxis(chunk_kv, 2, 0)))
    states = jnp.moveaxis(states, 0, 2)
    o = o + jnp.einsum('bhnck,bhnkv->bhncv', q * jnp.exp(b), states)
    return o.transpose(0, 2, 3, 1, 4).reshape(B, S, H, dv)


def moba_attention(q, k, v):
    B, S, H, D = q.shape
    BLK, Q = MOBA_BLOCK, MOBA_QCHUNK
    f32 = jnp.float32
    nb = -(-S // BLK)
    pad = nb * BLK - S
    q = q.transpose(0, 2, 1, 3)
    k = jnp.pad(k.transpose(0, 2, 1, 3), ((0, 0), (0, 0), (0, pad), (0, 0)))
    v = jnp.pad(v.transpose(0, 2, 1, 3), ((0, 0), (0, 0), (0, pad), (0, 0)))
    k_blocks = k.reshape(B, H, nb, BLK, D)
    v_blocks = v.reshape(B, H, nb, BLK, D)
    k_mean = jnp.mean(k_blocks.astype(f32), axis=3)
    pos = jnp.arange(S)
    q_blk = pos // BLK
    gate = jnp.einsum('bhsd,bhnd->bhsn', q.astype(f32), k_mean)
    fully_past = jnp.arange(nb)[None, :] < q_blk[:, None]
    gate = jnp.where(fully_past, gate, -jnp.inf)
    topk = min(MOBA_TOPK, nb)
    top_val, top_idx = lax.top_k(gate, topk)
    sel_ok = top_val > -jnp.inf
    nqc = S // Q

    def to_chunks(t):
        rest = t.shape[3:]
        t = t.reshape((B, H, nqc, Q) + rest).swapaxes(1, 2)
        return t.reshape((B * nqc, H, Q) + rest)

    b_ids = jnp.repeat(jnp.arange(B, dtype=jnp.int32), nqc)
    c_ids = jnp.tile(jnp.arange(nqc, dtype=jnp.int32), B)
    scale = D ** -0.5
    h_ar = jnp.arange(H)[:, None, None]

    def one_chunk(args):
        qc, idx, ok, bi, ci = args
        kb = k_blocks[bi]
        vb = v_blocks[bi]
        ks = kb[h_ar, idx]
        vs = vb[h_ar, idx]
        s_sel = jnp.einsum('hqd,hqjkd->hqjk', qc, ks).astype(f32) * scale
        s_sel = jnp.where(ok[..., None], s_sel, -jnp.inf).reshape(H, Q, topk * BLK)
        n_own = (ci * Q) // BLK
        k_own = lax.dynamic_index_in_dim(kb, n_own, axis=1, keepdims=False)
        v_own = lax.dynamic_index_in_dim(vb, n_own, axis=1, keepdims=False)
        s_own = jnp.einsum('hqd,hkd->hqk', qc, k_own).astype(f32) * scale
        qpos = ci * Q + jnp.arange(Q)
        kpos = n_own * BLK + jnp.arange(BLK)
        s_own = jnp.where((kpos[None, :] <= qpos[:, None])[None], s_own, -jnp.inf)
        p = jax.nn.softmax(jnp.concatenate([s_sel, s_own], axis=-1), axis=-1)
        p_sel = p[..., :topk * BLK].reshape(H, Q, topk, BLK).astype(vs.dtype)
        p_own = p[..., topk * BLK:].astype(v_own.dtype)
        return (jnp.einsum('hqjk,hqjkd->hqd', p_sel, vs)
                + jnp.einsum('hqk,hkd->hqd', p_own, v_own))

    out = lax.map(one_chunk, (to_chunks(q), to_chunks(top_idx), to_chunks(sel_ok), b_ids, c_ids))
    out = out.reshape(B, nqc, H, Q, D).transpose(0, 1, 3, 2, 4)
    return out.reshape(B, S, H * D)


def hierarchical_moe(h, w_rg, w_re, w_g, w_u, w_d):
    B, S, D = h.shape
    t = h.reshape(-1, D)
    f32 = jnp.float32
    lg = (t @ w_rg).astype(f32)
    p_group = jax.nn.softmax(lg, axis=-1)
    g_sel = jnp.argmax(lg, axis=-1)
    gate_group = jnp.take_along_axis(p_group, g_sel[:, None], axis=1)[:, 0]
    le = (t @ w_re).astype(f32).reshape(-1, MOE_GROUPS, MOE_EXPERTS_PER_GROUP)
    le = jnp.take_along_axis(le, g_sel[:, None, None], axis=1)[:, 0]
    top_v, top_i = lax.top_k(le, MOE_TOPK)
    w_sel = jax.nn.softmax(top_v, axis=-1) * gate_group[:, None]
    e_ids = g_sel[:, None] * MOE_EXPERTS_PER_GROUP + top_i
    comb = jnp.sum(jax.nn.one_hot(e_ids, MOE_N_EXPERTS, dtype=f32) * w_sel[..., None], axis=1)
    comb = comb.astype(t.dtype)
    y = jnp.zeros_like(t)
    for e in range(MOE_N_EXPERTS):
        he = jax.nn.silu(t @ w_g[e]) * (t @ w_u[e])
        y = y + comb[:, e:e + 1] * (he @ w_d[e])
    return y.reshape(B, S, D)


def setup_inputs(seed: int = 0) -> dict:
    key = jax.random.key(seed)
    ks = jax.random.split(key, 17)
    f32 = jnp.float32
    nrm = lambda k, shape, s: jax.random.normal(k, shape, f32) * s
    gain = lambda k, shape: 1.0 + 0.02 * jax.random.normal(k, shape, f32)
    return {
        'x': nrm(ks[0], (BATCH, SEQ, D_MODEL), 1.0),
        'attn_norm_g': gain(ks[1], (DEPTH, D_MODEL)),
        'w_in': nrm(ks[2], (DEPTH, D_MODEL, IN_COLS), D_MODEL ** -0.5),
        'gla_gk_w2': nrm(ks[3], (DEPTH, GLA_LOWRANK, GLA_KW), GLA_LOWRANK ** -0.5),
        'gla_gk_b': nrm(ks[4], (DEPTH, GLA_KW), 0.1),
        'gla_norm_g': gain(ks[5], (DEPTH, GLA_DV)),
        'moba_qnorm_g': gain(ks[6], (DEPTH, MOBA_HD)),
        'moba_knorm_g': gain(ks[7], (DEPTH, MOBA_HD)),
        'hgrn_lb_param': nrm(ks[8], (DEPTH, HGRN_KW), 1.0),
        'hgrn_norm_g': gain(ks[9], (DEPTH, HGRN_DV)),
        'w_out': nrm(ks[10], (DEPTH, MIX_WIDTH, D_MODEL), MIX_WIDTH ** -0.5),
        'ffn_norm_g': gain(ks[11], (DEPTH, D_MODEL)),
        'w_router_group': nrm(ks[12], (DEPTH, D_MODEL, MOE_GROUPS), D_MODEL ** -0.5),
        'w_router_expert': nrm(ks[13], (DEPTH, D_MODEL, MOE_N_EXPERTS), D_MODEL ** -0.5),
        'w_exp_gate': nrm(ks[14], (DEPTH, MOE_N_EXPERTS, D_MODEL, MOE_FF), D_MODEL ** -0.5),
        'w_exp_up': nrm(ks[15], (DEPTH, MOE_N_EXPERTS, D_MODEL, MOE_FF), D_MODEL ** -0.5),
        'w_exp_down': nrm(ks[16], (DEPTH, MOE_N_EXPERTS, MOE_FF, D_MODEL), MOE_FF ** -0.5),
    }


def reference(x, attn_norm_g, w_in, gla_gk_w2, gla_gk_b, gla_norm_g, moba_qnorm_g,
              moba_knorm_g, hgrn_lb_param, hgrn_norm_g, w_out, ffn_norm_g,
              w_router_group, w_router_expert, w_exp_gate, w_exp_up, w_exp_down):
    B, S, _ = x.shape
    f32 = jnp.float32
    pos = jnp.arange(S)
    offs = np.cumsum(IN_SPLITS)[:-1].tolist()
    lb_p = jax.nn.softmax(hgrn_lb_param.astype(f32), axis=0)
    lower_bounds = jnp.cumsum(lb_p, axis=0) - lb_p[0]

    def heads(t, n):
        return t.reshape(B, S, n, -1)

    for l in range(DEPTH):
        h = rms_norm(x, attn_norm_g[l])
        proj = h @ w_in[l]
        (a_q, a_k, a_v, a_g, a_lr, b_q, b_k, b_v,
         c_q, c_f, c_i, c_g) = jnp.split(proj, offs, axis=-1)

        log_alpha = jax.nn.log_sigmoid((a_lr @ gla_gk_w2[l] + gla_gk_b[l]).astype(f32)) / GLA_GATE_TAU
        o_a = gated_linear_chunked(heads(a_q, GLA_HEADS) * (GLA_DK ** -0.5), heads(a_k, GLA_HEADS),
                                   heads(a_v, GLA_HEADS), heads(log_alpha, GLA_HEADS))
        o_a = rms_norm(o_a.astype(x.dtype), gla_norm_g[l]) * jax.nn.silu(heads(a_g, GLA_HEADS))

        mq = partial_rotary(rms_norm(heads(b_q, MOBA_HEADS), moba_qnorm_g[l]), pos)
        mk = partial_rotary(rms_norm(heads(b_k, MOBA_HEADS), moba_knorm_g[l]), pos)
        o_b = moba_attention(mq, mk, heads(b_v, MOBA_HEADS))

        lb = lower_bounds[l].reshape(HGRN_HEADS, HGRN_DK)
        f_raw = heads(c_f, HGRN_HEADS).astype(f32)
        log_f = jnp.log(lb + (1.0 - lb) * jax.nn.sigmoid(f_raw))
        k_c = (1.0 - lb) * jax.nn.sigmoid(-f_raw)
        o_c = gated_linear_chunked(jax.nn.silu(heads(c_q, HGRN_HEADS)), k_c,
                                   heads(c_i, HGRN_HEADS), log_f)
        o_c = rms_norm(o_c.astype(x.dtype), hgrn_norm_g[l]) * jax.nn.silu(heads(c_g, HGRN_HEADS))

        mix = jnp.concatenate([o_a.reshape(B, S, GLA_VW), o_b, o_c.reshape(B, S, HGRN_VW)], axis=-1)
        x = x + mix @ w_out[l]

        h2 = rms_norm(x, ffn_norm_g[l])
        x = x + hierarchical_moe(h2, w_router_group[l], w_router_expert[l],
                                 w_exp_gate[l], w_exp_up[l], w_exp_down[l])
    return x
```

```python
import functools

import numpy as np
import jax
import jax.numpy as jnp
from jax import lax
from jax.experimental import pallas as pl
from jax.experimental.pallas import tpu as pltpu

F32 = jnp.float32
BF16 = jnp.bfloat16

NORM_EPS = 1e-6
GLA_GATE_TAU = 16.0
ROPE_THETA = 500000.0
MOBA_BLOCK = 256
MOBA_TOPK = 3
MOE_GROUPS = 4
MOE_EXPERTS_PER_GROUP = 4
MOE_TOPK = 2

LANES = 128
VMEM_LIMIT = 56 * 1024 * 1024
LIN_CHUNK = 64
LIN_TILE = 512
ROW_TILE = 512
MOE_ROW_TILE = 1024
NEG = -1e30


def _cparams(*sem):
    return pltpu.CompilerParams(dimension_semantics=sem, vmem_limit_bytes=VMEM_LIMIT)


def _sigmoid(x):
    return 1.0 / (1.0 + jnp.exp(-x))


def _split3(x):
    a = x.astype(BF16)
    r = x - a.astype(F32)
    b = r.astype(BF16)
    c = (r - b.astype(F32)).astype(BF16)
    return a, b, c


def _dot(a, b):
    return jnp.dot(a, b, preferred_element_type=F32)


def _dot_nt(a, b):
    return lax.dot_general(a, b, (((1,), (1,)), ((), ())), preferred_element_type=F32)


def _dot_tn(a, b):
    return lax.dot_general(a, b, (((0,), (0,)), ((), ())), preferred_element_type=F32)


def _dot_exact_rhs(x, m):
    a, b, c = _split3(x)
    return _dot(a, m) + _dot(b, m) + _dot(c, m)


def _in_proj_kernel(groups, x_ref, g_ref, w_ref, *out_refs):
    x = x_ref[...]
    ms = jnp.mean(x * x, axis=-1, keepdims=True)
    h = (x * lax.rsqrt(ms + NORM_EPS) * g_ref[...]).astype(BF16)
    for (a, b), o_ref in zip(groups, out_refs):
        o_ref[...] = _dot(h, w_ref[:, a:b]).astype(o_ref.dtype)


def _in_proj(x, g, w, groups, dtypes):
    n, d = x.shape
    tm = ROW_TILE
    out_shape = [jax.ShapeDtypeStruct((n, b - a), dt) for (a, b), dt in zip(groups, dtypes)]
    out_specs = [pl.BlockSpec((tm, b - a), lambda i: (i, 0)) for (a, b) in groups]
    return pl.pallas_call(
        functools.partial(_in_proj_kernel, groups),
        grid=(n // tm,),
        in_specs=[pl.BlockSpec((tm, d), lambda i: (i, 0)),
                  pl.BlockSpec((1, d), lambda i: (0, 0)),
                  pl.BlockSpec(w.shape, lambda i: (0, 0))],
        out_specs=out_specs,
        out_shape=out_shape,
        compiler_params=_cparams("parallel"),
        name="in_proj",
    )(x, g.reshape(1, d), w)


def _lin_constants(heads, dk, dv):
    L = LIN_CHUNK
    nlev = int(np.log2(L))
    K, V = heads * dk, heads * dv
    i = np.arange(L)[:, None]
    t = np.arange(L)[None, :]
    w_rows = [(t <= i), (t > i)]
    masks = []
    for lev in range(nlev):
        b = L >> lev
        half = b // 2
        r = (i // b) * b + half - 1
        w_rows.append((t > np.minimum(i, r)) & (t <= np.maximum(i, r)))
        j = t
        masks.append((i // b == j // b) & (i % b >= half) & (j % b < half))
    masks.append(i == t)
    w_all = np.concatenate(w_rows, axis=0).astype(np.float32)
    m_all = np.stack([np.tile(m, (1, heads)) for m in masks]).astype(np.float32)
    rh = np.repeat(np.arange(heads), L)[:, None]
    bdk = (rh == np.repeat(np.arange(heads), dk)[None, :]).astype(np.float32)
    bdv = (rh == np.repeat(np.arange(heads), dv)[None, :]).astype(np.float32)
    bds = (np.repeat(np.arange(heads), dv)[:, None]
           == np.repeat(np.arange(heads), dk)[None, :]).astype(np.float32)
    ones_v = (np.repeat(np.arange(heads), dv)[:, None]
              == np.repeat(np.arange(heads), dv)[None, :]).astype(np.float32)
    return (jnp.asarray(w_all, BF16), jnp.asarray(m_all, F32), jnp.asarray(bdk, BF16),
            jnp.asarray(bdv, BF16), jnp.asarray(bds, F32), jnp.asarray(ones_v, BF16))


def _lin_chunk(q, k, v, la, g, gain, st_ref, w_ref, m_ref, bdk_ref, bdv_ref, bds_ref, ones_ref,
               heads, dv):
    L = LIN_CHUNK
    nlev = m_ref.shape[0] - 1
    K = q.shape[1]
    a1, a2, a3 = _split3(la)
    z3 = _dot(w_ref[...], jnp.concatenate([a1, a2, a3], axis=1))
    e = jnp.exp(z3[:, :K] + z3[:, K:2 * K] + z3[:, 2 * K:])
    v_bf = v.astype(BF16)
    bdk = bdk_ref[...]
    scores = None
    for lev in range(nlev + 1):
        if lev < nlev:
            el = e[(2 + lev) * L:(3 + lev) * L]
            ql = (q * el).astype(BF16)
            kl = (k * el).astype(BF16)
        else:
            ql = q.astype(BF16)
            kl = k.astype(BF16)
        kbd = jnp.concatenate([kl] * heads, axis=0) * bdk
        s = _dot_nt(ql, kbd) * m_ref[lev]
        scores = s if scores is None else scores + s
    vbd = jnp.concatenate([v_bf] * heads, axis=0) * bdv_ref[...]
    st = st_ref[...]
    o = _dot(scores.astype(BF16), vbd)
    o = o + _dot_nt((q * e[0:L]).astype(BF16), st.astype(BF16))
    kb = (k * e[L:2 * L]).astype(BF16)
    st_ref[...] = st * e[L - 1:L] + _dot_tn(v_bf, kb) * bds_ref[...]
    oo = o * o
    hi = oo.astype(BF16)
    lo = (oo - hi.astype(F32)).astype(BF16)
    ms = (_dot(hi, ones_ref[...]) + _dot(lo, ones_ref[...])) * (1.0 / dv)
    gf = g.astype(F32)
    return o * lax.rsqrt(ms + NORM_EPS) * gain * (gf * _sigmoid(gf))


def _gla_kernel(heads, dk, dv, x_ref, lr_ref, w2_ref, b_ref, gain_ref,
                w_ref, m_ref, bdk_ref, bdv_ref, bds_ref, ones_ref, o_ref, st_ref):
    K, V = heads * dk, heads * dv
    L = LIN_CHUNK

    @pl.when(pl.program_id(1) == 0)
    def _():
        st_ref[...] = jnp.zeros_like(st_ref)

    gain = gain_ref[...]
    w2 = w2_ref[...]
    bias = b_ref[...]

    def body(c, carry):
        rows = pl.ds(pl.multiple_of(c * L, L), L)
        q = x_ref[rows, 0:K].astype(F32) * (dk ** -0.5)
        k = x_ref[rows, K:2 * K].astype(F32)
        v = x_ref[rows, 2 * K:2 * K + V]
        g = x_ref[rows, 2 * K + V:2 * K + 2 * V]
        gk = jnp.dot(lr_ref[rows, :], w2, precision=lax.Precision.HIGHEST,
                     preferred_element_type=F32) + bias
        la = (jnp.minimum(gk, 0.0) - jnp.log1p(jnp.exp(-jnp.abs(gk)))) * (1.0 / GLA_GATE_TAU)
        o_ref[rows, :] = _lin_chunk(q, k, v, la, g, gain, st_ref, w_ref, m_ref, bdk_ref, bdv_ref,
                                    bds_ref, ones_ref, heads, dv).astype(o_ref.dtype)
        return carry

    lax.fori_loop(0, x_ref.shape[0] // L, body, 0)


def _hgrn_kernel(heads, dk, dv, layer, x_ref, f_ref, lbp_ref, gain_ref,
                 w_ref, m_ref, bdk_ref, bdv_ref, bds_ref, ones_ref, o_ref, st_ref):
    K, V = heads * dk, heads * dv
    L = LIN_CHUNK

    @pl.when(pl.program_id(1) == 0)
    def _():
        st_ref[...] = jnp.zeros_like(st_ref)

    lbp = lbp_ref[...]
    depth = lbp.shape[0]
    mx = lbp[0:1]
    for r in range(1, depth):
        mx = jnp.maximum(mx, lbp[r:r + 1])
    ex = [jnp.exp(lbp[r:r + 1] - mx) for r in range(depth)]
    den = ex[0]
    for r in range(1, depth):
        den = den + ex[r]
    lb = jnp.zeros_like(den)
    for r in range(1, layer + 1):
        lb = lb + ex[r] / den
    gain = gain_ref[...]

    def body(c, carry):
        rows = pl.ds(pl.multiple_of(c * L, L), L)
        cq = x_ref[rows, 0:K].astype(F32)
        q = cq * _sigmoid(cq)
        v = x_ref[rows, K:K + V]
        g = x_ref[rows, K + V:K + 2 * V]
        f = f_ref[rows, :]
        la = jnp.log(lb + (1.0 - lb) * _sigmoid(f))
        k = (1.0 - lb) * _sigmoid(-f)
        o_ref[rows, :] = _lin_chunk(q, k, v, la, g, gain, st_ref, w_ref, m_ref, bdk_ref, bdv_ref,
                                    bds_ref, ones_ref, heads, dv).astype(o_ref.dtype)
        return carry

    lax.fori_loop(0, x_ref.shape[0] // L, body, 0)


def _lin_call(kernel_fn, name, batch, seq, heads, dk, dv, row_inputs, small_inputs):
    T = LIN_TILE
    nt = seq // T
    consts = _lin_constants(heads, dk, dv)
    row_specs = [pl.BlockSpec((T, a.shape[1]), lambda b, t: (b * nt + t, 0)) for a in row_inputs]
    full = lambda a: pl.BlockSpec(a.shape, lambda b, t: (0,) * a.ndim)
    return pl.pallas_call(
        kernel_fn,
        grid=(batch, nt),
        in_specs=row_specs + [full(a) for a in small_inputs] + [full(a) for a in consts],
        out_specs=pl.BlockSpec((T, heads * dv), lambda b, t: (b * nt + t, 0)),
        out_shape=jax.ShapeDtypeStruct((batch * seq, heads * dv), BF16),
        scratch_shapes=[pltpu.VMEM((heads * dv, heads * dk), F32)],
        compiler_params=_cparams("parallel", "arbitrary"),
        name=name,
    )(*row_inputs, *small_inputs, *consts)


def _moba_constants(heads, hd, seq):
    W = heads * hd
    nb = LANES // heads
    rot = hd // 4
    half = rot // 2
    inv_freq = ROPE_THETA ** (-(np.arange(0, rot, 2, dtype=np.float64) / rot))
    ang = np.arange(seq, dtype=np.float64)[:, None] * inv_freq[None, :]
    cos, sin = np.cos(ang), np.sin(ang)
    c = np.ones((seq, hd), np.float32)
    sp = np.zeros((seq, hd), np.float32)
    sm = np.zeros((seq, hd), np.float32)
    c[:, :half] = cos
    c[:, half:rot] = cos
    sm[:, :half] = -sin
    sp[:, half:rot] = sin
    reps = LANES // hd
    tabs = [jnp.asarray(np.tile(a, (1, reps)), F32) for a in (c, sp, sm)]
    hl = np.repeat(np.arange(heads), hd)
    ones_h = (hl[:, None] == hl[None, :]).astype(np.float32)
    hm = (np.arange(heads)[:, None] == hl[None, :]).astype(np.float32)
    pq = np.zeros((W, heads * LANES), np.float32)
    pq[np.arange(W), hl * LANES + np.arange(W) % hd] = 1.0
    pb = np.zeros((LANES, heads * LANES), np.float32)
    gl = np.arange(LANES)
    pb[gl, (gl % heads) * LANES + hd + gl // heads] = 1.0
    return tabs, (jnp.asarray(ones_h, BF16), jnp.asarray(hm, F32),
                  jnp.asarray(np.concatenate([pq, pb], axis=0), BF16))


def _moba_prep_kernel(heads, hd, x_ref, qg_ref, kg_ref, c_ref, sp_ref, sm_ref,
                      ones_ref, hm_ref, pexp_ref, qa_ref, ka_ref, kmt_ref):
    W = heads * hd
    t = pl.program_id(1)
    half = hd // 8
    reps = W // LANES

    @pl.when(t == 0)
    def _():
        kmt_ref[...] = jnp.zeros_like(kmt_ref)

    wide = lambda ref: jnp.concatenate([ref[...]] * reps, axis=1)
    c, sp, sm = wide(c_ref), wide(sp_ref), wide(sm_ref)

    def norm_rot(x, gain):
        xx = x * x
        hi = xx.astype(BF16)
        lo = (xx - hi.astype(F32)).astype(BF16)
        ms = (_dot(hi, ones_ref[...]) + _dot(lo, ones_ref[...])) * (1.0 / hd)
        y = x * lax.rsqrt(ms + NORM_EPS) * gain
        return y * c + pltpu.roll(y, half, 1) * sp + pltpu.roll(y, W - half, 1) * sm

    q = norm_rot(x_ref[:, 0:W].astype(F32), qg_ref[...])
    k = norm_rot(x_ref[:, W:2 * W].astype(F32), kg_ref[...])

    gate = lax.dot_general(q, kmt_ref[...], (((1,), (1,)), ((), ())),
                           precision=lax.Precision.HIGHEST, preferred_element_type=F32)
    lane = lax.broadcasted_iota(jnp.int32, gate.shape, 1)
    blk = lax.shift_right_logical(lane, int(np.log2(heads)))
    valid = blk < t
    gate = jnp.where(valid, gate, -jnp.inf)
    rank = jnp.zeros(gate.shape, F32)
    nslots = LANES // heads
    for s in range(1, nslots):
        other = pltpu.roll(gate, s * heads, 1)
        ahead = (other > gate) | ((other == gate) & (blk >= s))
        rank = rank + ahead.astype(F32)
    sel = valid & (rank < float(MOBA_TOPK))
    bias = jnp.where(sel, 0.0, NEG)

    lhs_q = jnp.concatenate([(q * (hd ** -0.5)).astype(BF16), bias.astype(BF16)], axis=1)
    qa_ref[...] = _dot(lhs_q, pexp_ref[...]).astype(qa_ref.dtype)
    ka = _dot(k.astype(BF16), pexp_ref[0:W, :])
    out_lane = lax.broadcasted_iota(jnp.int32, ka.shape, 1) & (LANES - 1)
    ka_ref[...] = jnp.where(out_lane == hd + t, 1.0, ka).astype(ka_ref.dtype)

    kmean = jnp.mean(k, axis=0, keepdims=True)
    row0 = pl.multiple_of(t * heads, heads)
    kmt_ref[pl.ds(row0, heads), :] = kmean * hm_ref[...]


def _moba_prep(x, qg, kg, batch, seq, heads, hd):
    W = heads * hd
    nt = seq // MOBA_BLOCK
    tabs, consts = _moba_constants(heads, hd, seq)
    full = lambda a: pl.BlockSpec(a.shape, lambda b, t: (0,) * a.ndim)
    tile_g = lambda g: jnp.tile(g.astype(F32), heads).reshape(1, W)
    small = (tile_g(qg), tile_g(kg))
    out_sd = jax.ShapeDtypeStruct((batch * seq, heads * LANES), BF16)
    out_spec = pl.BlockSpec((MOBA_BLOCK, heads * LANES), lambda b, t: (b * nt + t, 0))
    return pl.pallas_call(
        functools.partial(_moba_prep_kernel, heads, hd),
        grid=(batch, nt),
        in_specs=[pl.BlockSpec((MOBA_BLOCK, 2 * W), lambda b, t: (b * nt + t, 0))]
        + [full(a) for a in small]
        + [pl.BlockSpec((MOBA_BLOCK, LANES), lambda b, t: (t, 0)) for _ in tabs]
        + [full(a) for a in consts],
        out_specs=[out_spec, out_spec],
        out_shape=[out_sd, out_sd],
        scratch_shapes=[pltpu.VMEM((LANES, W), F32)],
        compiler_params=_cparams("parallel", "arbitrary"),
        name="moba_prep",
    )(x, *small, *tabs, *consts)


def _moba_attn_kernel(hd, qa_ref, ka_ref, v_ref, o_ref):
    t = pl.program_id(2)
    tq = qa_ref.shape[0]
    blk = MOBA_BLOCK
    row = lax.broadcasted_iota(jnp.int32, (tq, blk), 0)
    col = lax.broadcasted_iota(jnp.int32, (tq, blk), 1)
    lane = lax.broadcasted_iota(jnp.int32, (tq, LANES), 1)
    outs = []
    for hh in range(2):
        cols = slice(hh * LANES, (hh + 1) * LANES)
        q = qa_ref[:, cols]
        q_own = jnp.where(lane < hd, q, jnp.zeros_like(q))
        own = pl.ds(pl.multiple_of(t * blk, blk), blk)
        s = _dot_nt(q_own, ka_ref[own, cols])
        s = jnp.where(col <= row, s, NEG)
        m = jnp.max(s, axis=-1, keepdims=True)
        p = jnp.exp(s - m)
        l = jnp.sum(p, axis=-1, keepdims=True)
        acc = _dot(p.astype(BF16), v_ref[own, :])

        def body(n, carry):
            m, l, acc = carry
            rows = pl.ds(pl.multiple_of(n * blk, blk), blk)
            s = _dot_nt(q, ka_ref[rows, cols])
            m_new = jnp.maximum(m, jnp.max(s, axis=-1, keepdims=True))
            alpha = jnp.exp(m - m_new)
            p = jnp.exp(s - m_new)
            l = alpha * l + jnp.sum(p, axis=-1, keepdims=True)
            acc = alpha * acc + _dot(p.astype(BF16), v_ref[rows, :])
            return m_new, l, acc

        m, l, acc = lax.fori_loop(0, t, body, (m, l, acc))
        outs.append(acc / l)
    o_ref[...] = jnp.where(lane < hd, outs[0], outs[1]).astype(o_ref.dtype)


def _moba_attn(qa, ka, xv, v_col0, batch, seq, heads, hd):
    nt = seq // MOBA_BLOCK
    pairs = heads // 2
    vb0 = v_col0 // LANES
    return pl.pallas_call(
        functools.partial(_moba_attn_kernel, hd),
        grid=(batch, pairs, nt),
        in_specs=[pl.BlockSpec((MOBA_BLOCK, 2 * LANES), lambda b, p, t: (b * nt + t, p)),
                  pl.BlockSpec((seq, 2 * LANES), lambda b, p, t: (b, p)),
                  pl.BlockSpec((seq, LANES), lambda b, p, t: (b, vb0 + p))],
        out_specs=pl.BlockSpec((MOBA_BLOCK, LANES), lambda b, p, t: (b * nt + t, p)),
        out_shape=jax.ShapeDtypeStruct((batch * seq, heads * hd), BF16),
        compiler_params=_cparams("parallel", "parallel", "arbitrary"),
        name="moba_attn",
    )(qa, ka, xv)


def _out_proj_kernel(widths, x_ref, oa_ref, ob_ref, oc_ref, w_ref, g_ref, wr_ref,
                     x1_ref, h2_ref, comb_ref):
    acc = x_ref[...]
    r0 = 0
    for o_ref, wd in zip((oa_ref, ob_ref, oc_ref), widths):
        acc = acc + _dot(o_ref[...], w_ref[r0:r0 + wd, :])
        r0 += wd
    x1_ref[...] = acc
    ms = jnp.mean(acc * acc, axis=-1, keepdims=True)
    h2 = acc * lax.rsqrt(ms + NORM_EPS) * g_ref[...]
    h2_ref[...] = h2.astype(h2_ref.dtype)

    G, E = MOE_GROUPS, MOE_EXPERTS_PER_GROUP
    logits = jnp.dot(h2, wr_ref[...], precision=lax.Precision.HIGHEST, preferred_element_type=F32)
    lane = lax.broadcasted_iota(jnp.int32, logits.shape, 1).astype(F32)
    big = float(LANES)
    is_g = lane < G
    lg = jnp.where(is_g, logits, -jnp.inf)
    mg = jnp.max(lg, axis=-1, keepdims=True)
    gate_group = 1.0 / jnp.sum(jnp.exp(lg - mg), axis=-1, keepdims=True)
    g_sel = jnp.min(jnp.where(lg == mg, lane, big), axis=-1, keepdims=True)
    lo = G + E * g_sel
    in_grp = (lane >= lo) & (lane < lo + E)
    v = jnp.where(in_grp, logits, -jnp.inf)
    v1 = jnp.max(v, axis=-1, keepdims=True)
    i1 = jnp.min(jnp.where(v == v1, lane, big), axis=-1, keepdims=True)
    v = jnp.where(lane == i1, -jnp.inf, v)
    v2 = jnp.max(v, axis=-1, keepdims=True)
    i2 = jnp.min(jnp.where(v == v2, lane, big), axis=-1, keepdims=True)
    e2 = jnp.exp(v2 - v1)
    w1 = gate_group / (1.0 + e2)
    w2 = gate_group * e2 / (1.0 + e2)
    comb_ref[...] = jnp.where(lane == i1, w1, 0.0) + jnp.where(lane == i2, w2, 0.0)


def _out_proj(x, oa, ob, oc, w, g, wr):
    n, d = x.shape
    tm = ROW_TILE
    widths = (oa.shape[1], ob.shape[1], oc.shape[1])
    row = lambda width: pl.BlockSpec((tm, width), lambda i: (i, 0))
    full = lambda a: pl.BlockSpec(a.shape, lambda i: (0, 0))
    g2 = g.reshape(1, d)
    return pl.pallas_call(
        functools.partial(_out_proj_kernel, widths),
        grid=(n // tm,),
        in_specs=[row(d), row(widths[0]), row(widths[1]), row(widths[2]), full(w), full(g2), full(wr)],
        out_specs=[row(d), row(d), row(LANES)],
        out_shape=[jax.ShapeDtypeStruct((n, d), F32), jax.ShapeDtypeStruct((n, d), BF16),
                   jax.ShapeDtypeStruct((n, LANES), F32)],
        compiler_params=_cparams("parallel"),
        name="out_proj",
    )(x, oa, ob, oc, w, g2, wr)


def _moe_kernel(x1_ref, h_ref, comb_ref, wg_ref, wu_ref, wd_ref, o_ref, acc_ref):
    e = pl.program_id(1)

    @pl.when(e == 0)
    def _():
        acc_ref[...] = x1_ref[...]

    h = h_ref[...]
    comb = comb_ref[...]
    lane = lax.broadcasted_iota(jnp.int32, comb.shape, 1)
    c = jnp.sum(jnp.where(lane == MOE_GROUPS + e, comb, 0.0), axis=-1, keepdims=True)
    a = _dot(h, wg_ref[0])
    u = _dot(h, wu_ref[0])
    he = (a * _sigmoid(a)) * u * c
    acc_ref[...] += _dot(he.astype(BF16), wd_ref[0])

    @pl.when(e == pl.num_programs(1) - 1)
    def _():
        o_ref[...] = acc_ref[...]


def _moe(x1, h2, comb, wg, wu, wd):
    n, d = x1.shape
    ne, _, ff = wg.shape
    tm = MOE_ROW_TILE
    return pl.pallas_call(
        _moe_kernel,
        grid=(n // tm, ne),
        in_specs=[pl.BlockSpec((tm, d), lambda i, e: (i, 0)),
                  pl.BlockSpec((tm, d), lambda i, e: (i, 0)),
                  pl.BlockSpec((tm, LANES), lambda i, e: (i, 0)),
                  pl.BlockSpec((1, d, ff), lambda i, e: (e, 0, 0)),
                  pl.BlockSpec((1, d, ff), lambda i, e: (e, 0, 0)),
                  pl.BlockSpec((1, ff, d), lambda i, e: (e, 0, 0))],
        out_specs=pl.BlockSpec((tm, d), lambda i, e: (i, 0)),
        out_shape=jax.ShapeDtypeStruct((n, d), F32),
        scratch_shapes=[pltpu.VMEM((tm, d), F32)],
        compiler_params=_cparams("parallel", "arbitrary"),
        name="moe",
    )(x1, h2, comb, wg, wu, wd)


def _pad_cols(w, width):
    return jnp.pad(w, ((0, 0), (0, width - w.shape[1])))


def kernel(x, attn_norm_g, w_in, gla_gk_w2, gla_gk_b, gla_norm_g, moba_qnorm_g, moba_knorm_g,
           hgrn_lb_param, hgrn_norm_g, w_out, ffn_norm_g, w_router_group, w_router_expert,
           w_exp_gate, w_exp_up, w_exp_down):
    batch, seq, d = x.shape
    depth = w_in.shape[0]
    lowrank, gla_kw = gla_gk_w2.shape[1:]
    gla_dv = gla_norm_g.shape[1]
    moba_hd = moba_qnorm_g.shape[1]
    hgrn_kw = hgrn_lb_param.shape[1]
    hgrn_dv = hgrn_norm_g.shape[1]
    mix_w = w_out.shape[1]
    hgrn_heads = 4
    gla_heads = 4
    gla_vw = gla_heads * gla_dv
    hgrn_vw = hgrn_heads * hgrn_dv
    moba_w = mix_w - gla_vw - hgrn_vw
    moba_heads = moba_w // moba_hd
    gla_dk = gla_kw // gla_heads
    hgrn_dk = hgrn_kw // hgrn_heads

    splits = (gla_kw, gla_kw, gla_vw, gla_vw, lowrank, moba_w, moba_w, moba_w,
              hgrn_kw, hgrn_kw, hgrn_vw, hgrn_vw)
    offs = np.concatenate([[0], np.cumsum(splits)]).tolist()
    col = lambda w, i: w[:, offs[i]:offs[i + 1]]

    xf = x.reshape(batch * seq, d)
    for l in range(depth):
        w = w_in[l]
        wcat = jnp.concatenate(
            [col(w, 0), col(w, 1), col(w, 2), col(w, 3), _pad_cols(col(w, 4), LANES),
             col(w, 5), col(w, 6), col(w, 7), col(w, 8), col(w, 10), col(w, 11), col(w, 9)],
            axis=1).astype(BF16)
        g_gla = 2 * gla_kw + 2 * gla_vw
        g_hg = hgrn_kw + 2 * hgrn_vw
        bounds = np.cumsum([0, g_gla, LANES, 3 * moba_w, g_hg, hgrn_kw]).tolist()
        groups = tuple((bounds[i], bounds[i + 1]) for i in range(5))
        y_gla, y_lr, y_moba, y_hg, y_f = _in_proj(
            xf, attn_norm_g[l], wcat, groups, (BF16, F32, BF16, BF16, F32))

        w2 = jnp.pad(gla_gk_w2[l], ((0, LANES - lowrank), (0, 0)))
        o_a = _lin_call(
            functools.partial(_gla_kernel, gla_heads, gla_dk, gla_dv), "gla",
            batch, seq, gla_heads, gla_dk, gla_dv, (y_gla, y_lr),
            (w2, gla_gk_b[l].reshape(1, gla_kw), jnp.tile(gla_norm_g[l], gla_heads).reshape(1, gla_vw)))
        o_c = _lin_call(
            functools.partial(_hgrn_kernel, hgrn_heads, hgrn_dk, hgrn_dv, l), "hgrn",
            batch, seq, hgrn_heads, hgrn_dk, hgrn_dv, (y_hg, y_f),
            (hgrn_lb_param, jnp.tile(hgrn_norm_g[l], hgrn_heads).reshape(1, hgrn_vw)))

        qa, ka = _moba_prep(y_moba, moba_qnorm_g[l], moba_knorm_g[l], batch, seq, moba_heads, moba_hd)
        o_b = _moba_attn(qa, ka, y_moba, 2 * moba_w, batch, seq, moba_heads, moba_hd)

        wr = _pad_cols(jnp.concatenate([w_router_group[l], w_router_expert[l]], axis=1), LANES)
        x1, h2, comb = _out_proj(xf, o_a, o_b, o_c, w_out[l].astype(BF16), ffn_norm_g[l], wr)
        xf = _moe(x1, h2, comb, w_exp_gate[l].astype(BF16), w_exp_up[l].astype(BF16),
                  w_exp_down[l].astype(BF16))
    return xf.reshape(batch, seq, d)
```

```python
import functools

import numpy as np
import jax
import jax.numpy as jnp
from jax import lax
from jax.experimental import pallas as pl
from jax.experimental.pallas import tpu as pltpu

F32 = jnp.float32
BF16 = jnp.bfloat16

NORM_EPS = 1e-6
GLA_GATE_TAU = 16.0
ROPE_THETA = 500000.0
MOBA_BLOCK = 256
MOBA_TOPK = 3
MOE_GROUPS = 4
MOE_EXPERTS_PER_GROUP = 4
MOE_TOPK = 2

LANES = 128
VMEM_LIMIT = 56 * 1024 * 1024
LIN_CHUNK = 64
LIN_TILE = 512
ROW_TILE = 512
MOE_ROW_TILE = 1024
NEG = -1e30


def _cparams(*sem):
    return pltpu.CompilerParams(dimension_semantics=sem, vmem_limit_bytes=VMEM_LIMIT)


def _sigmoid(x):
    return 1.0 / (1.0 + jnp.exp(-x))


def _split3(x):
    a = x.astype(BF16)
    r = x - a.astype(F32)
    b = r.astype(BF16)
    c = (r - b.astype(F32)).astype(BF16)
    return a, b, c


def _dot(a, b):
    return jnp.dot(a, b, preferred_element_type=F32)


def _dot_nt(a, b):
    return lax.dot_general(a, b, (((1,), (1,)), ((), ())), preferred_element_type=F32)


def _dot_tn(a, b):
    return lax.dot_general(a, b, (((0,), (0,)), ((), ())), preferred_element_type=F32)


def _dot_exact_rhs(x, m):
    a, b, c = _split3(x)
    return _dot(a, m) + _dot(b, m) + _dot(c, m)


def _in_proj_kernel(groups, x_ref, g_ref, w_ref, *out_refs):
    x = x_ref[...]
    ms = jnp.mean(x * x, axis=-1, keepdims=True)
    h = (x * lax.rsqrt(ms + NORM_EPS) * g_ref[...]).astype(BF16)
    for (a, b), o_ref in zip(groups, out_refs):
        o_ref[...] = _dot(h, w_ref[:, a:b]).astype(o_ref.dtype)


def _in_proj(x, g, w, groups, dtypes):
    n, d = x.shape
    tm = ROW_TILE
    out_shape = [jax.ShapeDtypeStruct((n, b - a), dt) for (a, b), dt in zip(groups, dtypes)]
    out_specs = [pl.BlockSpec((tm, b - a), lambda i: (i, 0)) for (a, b) in groups]
    return pl.pallas_call(
        functools.partial(_in_proj_kernel, groups),
        grid=(n // tm,),
        in_specs=[pl.BlockSpec((tm, d), lambda i: (i, 0)),
                  pl.BlockSpec((1, d), lambda i: (0, 0)),
                  pl.BlockSpec(w.shape, lambda i: (0, 0))],
        out_specs=out_specs,
        out_shape=out_shape,
        compiler_params=_cparams("parallel"),
        name="in_proj",
    )(x, g.reshape(1, d), w)


def _lin_constants(heads, dk, dv):
    L = LIN_CHUNK
    nlev = int(np.log2(L))
    K, V = heads * dk, heads * dv
    i = np.arange(L)[:, None]
    t = np.arange(L)[None, :]
    w_rows = [(t <= i), (t > i)]
    masks = []
    for lev in range(nlev):
        b = L >> lev
        half = b // 2
        r = (i // b) * b + half - 1
        w_rows.append((t > np.minimum(i, r)) & (t <= np.maximum(i, r)))
        j = t
        masks.append((i // b == j // b) & (i % b >= half) & (j % b < half))
    masks.append(i == t)
    w_all = np.concatenate(w_rows, axis=0).astype(np.float32)
    m_all = np.stack([np.tile(m, (1, heads)) for m in masks]).astype(np.float32)
    rh = np.repeat(np.arange(heads), L)[:, None]
    bdk = (rh == np.repeat(np.arange(heads), dk)[None, :]).astype(np.float32)
    bdv = (rh == np.repeat(np.arange(heads), dv)[None, :]).astype(np.float32)
    bds = (np.repeat(np.arange(heads), dv)[:, None]
           == np.repeat(np.arange(heads), dk)[None, :]).astype(np.float32)
    ones_v = (np.repeat(np.arange(heads), dv)[:, None]
              == np.repeat(np.arange(heads), dv)[None, :]).astype(np.float32)
    return (jnp.asarray(w_all, BF16), jnp.asarray(m_all, F32), jnp.asarray(bdk, BF16),
            jnp.asarray(bdv, BF16), jnp.asarray(bds, F32), jnp.asarray(ones_v, BF16))


def _lin_chunk(q, k, v, la, g, gain, st_ref, w_ref, m_ref, bdk_ref, bdv_ref, bds_ref, ones_ref,
               heads, dv):
    L = LIN_CHUNK
    nlev = m_ref.shape[0] - 1
    K = q.shape[1]
    a1, a2, a3 = _split3(la)
    z3 = _dot(w_ref[...], jnp.concatenate([a1, a2, a3], axis=1))
    e = jnp.exp(z3[:, :K] + z3[:, K:2 * K] + z3[:, 2 * K:])
    v_bf = v.astype(BF16)
    bdk = bdk_ref[...]
    scores = None
    for lev in range(nlev + 1):
        if lev < nlev:
            el = e[(2 + lev) * L:(3 + lev) * L]
            ql = (q * el).astype(BF16)
            kl = (k * el).astype(BF16)
        else:
            ql = q.astype(BF16)
            kl = k.astype(BF16)
        kbd = jnp.concatenate([kl] * heads, axis=0) * bdk
        s = _dot_nt(ql, kbd) * m_ref[lev]
        scores = s if scores is None else scores + s
    vbd = jnp.concatenate([v_bf] * heads, axis=0) * bdv_ref[...]
    st = st_ref[...]
    o = _dot(scores.astype(BF16), vbd)
    o = o + _dot_nt((q * e[0:L]).astype(BF16), st.astype(BF16))
    kb = (k * e[L:2 * L]).astype(BF16)
    st_ref[...] = st * e[L - 1:L] + _dot_tn(v_bf, kb) * bds_ref[...]
    oo = o * o
    hi = oo.astype(BF16)
    lo = (oo - hi.astype(F32)).astype(BF16)
    ms = (_dot(hi, ones_ref[...]) + _dot(lo, ones_ref[...])) * (1.0 / dv)
    gf = g.astype(F32)
    return o * lax.rsqrt(ms + NORM_EPS) * gain * (gf * _sigmoid(gf))


def _gla_kernel(heads, dk, dv, x_ref, lr_ref, w2_ref, b_ref, gain_ref,
                w_ref, m_ref, bdk_ref, bdv_ref, bds_ref, ones_ref, o_ref, st_ref):
    K, V = heads * dk, heads * dv
    L = LIN_CHUNK

    @pl.when(pl.program_id(1) == 0)
    def _():
        st_ref[...] = jnp.zeros_like(st_ref)

    gain = gain_ref[...]
    w2 = w2_ref[...]
    bias = b_ref[...]

    def body(c, carry):
        rows = pl.ds(pl.multiple_of(c * L, L), L)
        q = x_ref[rows, 0:K].astype(F32) * (dk ** -0.5)
        k = x_ref[rows, K:2 * K].astype(F32)
        v = x_ref[rows, 2 * K:2 * K + V]
        g = x_ref[rows, 2 * K + V:2 * K + 2 * V]
        gk = jnp.dot(lr_ref[rows, :], w2, precision=lax.Precision.HIGHEST,
                     preferred_element_type=F32) + bias
        la = (jnp.minimum(gk, 0.0) - jnp.log1p(jnp.exp(-jnp.abs(gk)))) * (1.0 / GLA_GATE_TAU)
        o_ref[rows, :] = _lin_chunk(q, k, v, la, g, gain, st_ref, w_ref, m_ref, bdk_ref, bdv_ref,
                                    bds_ref, ones_ref, heads, dv).astype(o_ref.dtype)
        return carry

    lax.fori_loop(0, x_ref.shape[0] // L, body, 0)


def _hgrn_kernel(heads, dk, dv, layer, x_ref, f_ref, lbp_ref, gain_ref,
                 w_ref, m_ref, bdk_ref, bdv_ref, bds_ref, ones_ref, o_ref, st_ref):
    K, V = heads * dk, heads * dv
    L = LIN_CHUNK

    @pl.when(pl.program_id(1) == 0)
    def _():
        st_ref[...] = jnp.zeros_like(st_ref)

    lbp = lbp_ref[...]
    depth = lbp.shape[0]
    mx = lbp[0:1]
    for r in range(1, depth):
        mx = jnp.maximum(mx, lbp[r:r + 1])
    ex = [jnp.exp(lbp[r:r + 1] - mx) for r in range(depth)]
    den = ex[0]
    for r in range(1, depth):
        den = den + ex[r]
    lb = jnp.zeros_like(den)
    for r in range(1, layer + 1):
        lb = lb + ex[r] / den
    gain = gain_ref[...]

    def body(c, carry):
        rows = pl.ds(pl.multiple_of(c * L, L), L)
        cq = x_ref[rows, 0:K].astype(F32)
        q = cq * _sigmoid(cq)
        v = x_ref[rows, K:K + V]
        g = x_ref[rows, K + V:K + 2 * V]
        f = f_ref[rows, :]
        la = jnp.log(lb + (1.0 - lb) * _sigmoid(f))
        k = (1.0 - lb) * _sigmoid(-f)
        o_ref[rows, :] = _lin_chunk(q, k, v, la, g, gain, st_ref, w_ref, m_ref, bdk_ref, bdv_ref,
                                    bds_ref, ones_ref, heads, dv).astype(o_ref.dtype)
        return carry

    lax.fori_loop(0, x_ref.shape[0] // L, body, 0)


def _lin_call(kernel_fn, name, batch, seq, heads, dk, dv, row_inputs, small_inputs):
    T = LIN_TILE
    nt = seq // T
    consts = _lin_constants(heads, dk, dv)
    row_specs = [pl.BlockSpec((T, a.shape[1]), lambda b, t: (b * nt + t, 0)) for a in row_inputs]
    full = lambda a: pl.BlockSpec(a.shape, lambda b, t: (0,) * a.ndim)
    return pl.pallas_call(
        kernel_fn,
        grid=(batch, nt),
        in_specs=row_specs + [full(a) for a in small_inputs] + [full(a) for a in consts],
        out_specs=pl.BlockSpec((T, heads * dv), lambda b, t: (b * nt + t, 0)),
        out_shape=jax.ShapeDtypeStruct((batch * seq, heads * dv), BF16),
        scratch_shapes=[pltpu.VMEM((heads * dv, heads * dk), F32)],
        compiler_params=_cparams("parallel", "arbitrary"),
        name=name,
    )(*row_inputs, *small_inputs, *consts)


def _moba_constants(heads, hd, seq):
    W = heads * hd
    nb = LANES // heads
    rot = hd // 4
    half = rot // 2
    inv_freq = ROPE_THETA ** (-(np.arange(0, rot, 2, dtype=np.float64) / rot))
    ang = np.arange(seq, dtype=np.float64)[:, None] * inv_freq[None, :]
    cos, sin = np.cos(ang), np.sin(ang)
    c = np.ones((seq, hd), np.float32)
    sp = np.zeros((seq, hd), np.float32)
    sm = np.zeros((seq, hd), np.float32)
    c[:, :half] = cos
    c[:, half:rot] = cos
    sm[:, :half] = -sin
    sp[:, half:rot] = sin
    reps = LANES // hd
    tabs = [jnp.asarray(np.tile(a, (1, reps)), F32) for a in (c, sp, sm)]
    hl = np.repeat(np.arange(heads), hd)
    ones_h = (hl[:, None] == hl[None, :]).astype(np.float32)
    hm = (np.arange(heads)[:, None] == hl[None, :]).astype(np.float32)
    pq = np.zeros((W, heads * LANES), np.float32)
    pq[np.arange(W), hl * LANES + np.arange(W) % hd] = 1.0
    pb = np.zeros((LANES, heads * LANES), np.float32)
    gl = np.arange(LANES)
    pb[gl, (gl % heads) * LANES + hd + gl // heads] = 1.0
    return tabs, (jnp.asarray(ones_h, BF16), jnp.asarray(hm, F32),
                  jnp.asarray(np.concatenate([pq, pb], axis=0), BF16))


def _moba_prep_kernel(heads, hd, x_ref, qg_ref, kg_ref, c_ref, sp_ref, sm_ref,
                      ones_ref, hm_ref, pexp_ref, qa_ref, ka_ref, vt_ref, kmt_ref):
    W = heads * hd
    t = pl.program_id(1)
    half = hd // 8
    reps = W // LANES

    @pl.when(t == 0)
    def _():
        kmt_ref[...] = jnp.zeros_like(kmt_ref)

    wide = lambda ref: jnp.concatenate([ref[...]] * reps, axis=1)
    c, sp, sm = wide(c_ref), wide(sp_ref), wide(sm_ref)

    def norm_rot(x, gain):
        xx = x * x
        hi = xx.astype(BF16)
        lo = (xx - hi.astype(F32)).astype(BF16)
        ms = (_dot(hi, ones_ref[...]) + _dot(lo, ones_ref[...])) * (1.0 / hd)
        y = x * lax.rsqrt(ms + NORM_EPS) * gain
        return y * c + pltpu.roll(y, half, 1) * sp + pltpu.roll(y, W - half, 1) * sm

    q = norm_rot(x_ref[:, 0:W].astype(F32), qg_ref[...])
    k = norm_rot(x_ref[:, W:2 * W].astype(F32), kg_ref[...])

    gate = lax.dot_general(q, kmt_ref[...], (((1,), (1,)), ((), ())),
                           precision=lax.Precision.HIGHEST, preferred_element_type=F32)
    lane = lax.broadcasted_iota(jnp.int32, gate.shape, 1)
    blk = lax.shift_right_logical(lane, int(np.log2(heads)))
    valid = blk < t
    gate = jnp.where(valid, gate, -jnp.inf)
    rank = jnp.zeros(gate.shape, F32)
    nslots = LANES // heads
    for s in range(1, nslots):
        other = pltpu.roll(gate, s * heads, 1)
        ahead = (other > gate) | ((other == gate) & (blk >= s))
        rank = rank + ahead.astype(F32)
    sel = valid & (rank < float(MOBA_TOPK))
    bias = jnp.where(sel | (blk == t), 0.0, NEG)

    lhs_q = jnp.concatenate([(q * (hd ** -0.5)).astype(BF16), bias.astype(BF16)], axis=1)
    qa_ref[...] = _dot(lhs_q, pexp_ref[...]).astype(qa_ref.dtype)
    ka = _dot(k.astype(BF16), pexp_ref[0:W, :])
    out_lane = lax.broadcasted_iota(jnp.int32, ka.shape, 1) & (LANES - 1)
    ka_ref[...] = jnp.where(out_lane == hd + t, 1.0, ka).astype(ka_ref.dtype)

    kmean = jnp.mean(k, axis=0, keepdims=True)
    row0 = pl.multiple_of(t * heads, heads)
    kmt_ref[pl.ds(row0, heads), :] = kmean * hm_ref[...]

    vt = x_ref[:, 2 * W:3 * W].astype(F32).T
    ones = jnp.ones((LANES - hd, vt.shape[1]), vt_ref.dtype)
    for h in range(heads):
        vt_ref[0, h, 0, 0:hd, :] = vt[h * hd:(h + 1) * hd].astype(vt_ref.dtype)
        vt_ref[0, h, 0, hd:LANES, :] = ones


def _moba_prep(x, qg, kg, batch, seq, heads, hd):
    W = heads * hd
    nt = seq // MOBA_BLOCK
    tabs, consts = _moba_constants(heads, hd, seq)
    full = lambda a: pl.BlockSpec(a.shape, lambda b, t: (0,) * a.ndim)
    tile_g = lambda g: jnp.tile(g.astype(F32), heads).reshape(1, W)
    small = (tile_g(qg), tile_g(kg))
    out_sd = jax.ShapeDtypeStruct((batch * seq, heads * LANES), BF16)
    out_spec = pl.BlockSpec((MOBA_BLOCK, heads * LANES), lambda b, t: (b * nt + t, 0))
    vt_sd = jax.ShapeDtypeStruct((batch, heads, nt, LANES, MOBA_BLOCK), BF16)
    vt_spec = pl.BlockSpec((1, heads, 1, LANES, MOBA_BLOCK), lambda b, t: (b, 0, t, 0, 0))
    return pl.pallas_call(
        functools.partial(_moba_prep_kernel, heads, hd),
        grid=(batch, nt),
        in_specs=[pl.BlockSpec((MOBA_BLOCK, 3 * W), lambda b, t: (b * nt + t, 0))]
        + [full(a) for a in small]
        + [pl.BlockSpec((MOBA_BLOCK, LANES), lambda b, t: (t, 0)) for _ in tabs]
        + [full(a) for a in consts],
        out_specs=[out_spec, out_spec, vt_spec],
        out_shape=[out_sd, out_sd, vt_sd],
        scratch_shapes=[pltpu.VMEM((LANES, W), F32)],
        compiler_params=_cparams("parallel", "arbitrary"),
        name="moba_prep",
    )(x, *small, *tabs, *consts)


def _moba_attn_kernel(hd, qa_ref, ka_ref, vt_ref, o_ref, acc_ref, sa_ref, sb_ref):
    t = pl.program_id(1)
    tq = qa_ref.shape[0]
    blk = MOBA_BLOCK
    nh = vt_ref.shape[1]

    def scores_to(dst_ref, n, hh):
        cols = slice(hh * LANES, (hh + 1) * LANES)
        rows = pl.ds(pl.multiple_of(n * blk, blk), blk)
        dst_ref[hh] = _dot_nt(ka_ref[rows, cols], qa_ref[:, cols])

    def update(s, m, n, hh):
        m_new = jnp.maximum(m, jnp.max(s, axis=0, keepdims=True))
        alpha = jnp.exp(m - m_new)
        p = jnp.exp(s - m_new).astype(BF16)
        acc_ref[hh] = alpha * acc_ref[hh] + _dot(vt_ref[0, hh, n], p)
        return m_new

    def step(cur_ref, nxt_ref, n, ms):
        new = []
        for hh in range(nh):
            scores_to(nxt_ref, n + 1, hh)
            new.append(update(cur_ref[hh], ms[hh], n, hh))
        return tuple(new)

    acc_ref[...] = jnp.zeros_like(acc_ref)
    for hh in range(nh):
        scores_to(sa_ref, 0, hh)

    def pair(i, ms):
        ms = step(sa_ref, sb_ref, 2 * i, ms)
        return step(sb_ref, sa_ref, 2 * i + 1, ms)

    m0 = jnp.full((1, tq), 2 * NEG, F32)
    ms = lax.fori_loop(0, t // 2, pair, (m0,) * nh)

    def odd_block(ms):
        new = []
        for hh in range(nh):
            new.append(update(sa_ref[hh], ms[hh], t - 1, hh))
            scores_to(sa_ref, t, hh)
        return tuple(new)

    ms = lax.cond(t % 2 == 1, odd_block, lambda ms: ms, ms)

    key = lax.broadcasted_iota(jnp.int32, (blk, tq), 0)
    qry = lax.broadcasted_iota(jnp.int32, (blk, tq), 1)
    outs = []
    for hh in range(nh):
        s = jnp.where(key <= qry, sa_ref[hh], NEG)
        update(s, ms[hh], t, hh)
        a = acc_ref[hh]
        outs.append(a[0:hd] / a[hd:hd + 1])
    o_ref[...] = jnp.concatenate(outs, axis=0).T.astype(o_ref.dtype)


def _moba_attn(qa, ka, vt, batch, seq, heads, hd):
    nt = seq // MOBA_BLOCK
    return pl.pallas_call(
        functools.partial(_moba_attn_kernel, hd),
        grid=(batch, nt),
        in_specs=[pl.BlockSpec((MOBA_BLOCK, heads * LANES), lambda b, t: (b * nt + t, 0)),
                  pl.BlockSpec((seq, heads * LANES), lambda b, t: (b, 0)),
                  pl.BlockSpec((1, heads, nt, LANES, MOBA_BLOCK), lambda b, t: (b, 0, 0, 0, 0))],
        out_specs=pl.BlockSpec((MOBA_BLOCK, heads * hd), lambda b, t: (b * nt + t, 0)),
        out_shape=jax.ShapeDtypeStruct((batch * seq, heads * hd), BF16),
        scratch_shapes=[pltpu.VMEM((heads, LANES, MOBA_BLOCK), F32),
                        pltpu.VMEM((heads, MOBA_BLOCK, MOBA_BLOCK), F32),
                        pltpu.VMEM((heads, MOBA_BLOCK, MOBA_BLOCK), F32)],
        compiler_params=_cparams("parallel", "arbitrary"),
        name="moba_attn",
    )(qa, ka, vt)


def _out_proj_kernel(widths, x_ref, oa_ref, ob_ref, oc_ref, w_ref, g_ref, wr_ref,
                     x1_ref, h2_ref, comb_ref):
    acc = x_ref[...]
    r0 = 0
    for o_ref, wd in zip((oa_ref, ob_ref, oc_ref), widths):
        acc = acc + _dot(o_ref[...], w_ref[r0:r0 + wd, :])
        r0 += wd
    x1_ref[...] = acc
    ms = jnp.mean(acc * acc, axis=-1, keepdims=True)
    h2 = acc * lax.rsqrt(ms + NORM_EPS) * g_ref[...]
    h2_ref[...] = h2.astype(h2_ref.dtype)

    G, E = MOE_GROUPS, MOE_EXPERTS_PER_GROUP
    logits = jnp.dot(h2, wr_ref[...], precision=lax.Precision.HIGHEST, preferred_element_type=F32)
    lane = lax.broadcasted_iota(jnp.int32, logits.shape, 1).astype(F32)
    big = float(LANES)
    is_g = lane < G
    lg = jnp.where(is_g, logits, -jnp.inf)
    mg = jnp.max(lg, axis=-1, keepdims=True)
    gate_group = 1.0 / jnp.sum(jnp.exp(lg - mg), axis=-1, keepdims=True)
    g_sel = jnp.min(jnp.where(lg == mg, lane, big), axis=-1, keepdims=True)
    lo = G + E * g_sel
    in_grp = (lane >= lo) & (lane < lo + E)
    v = jnp.where(in_grp, logits, -jnp.inf)
    v1 = jnp.max(v, axis=-1, keepdims=True)
    i1 = jnp.min(jnp.where(v == v1, lane, big), axis=-1, keepdims=True)
    v = jnp.where(lane == i1, -jnp.inf, v)
    v2 = jnp.max(v, axis=-1, keepdims=True)
    i2 = jnp.min(jnp.where(v == v2, lane, big), axis=-1, keepdims=True)
    e2 = jnp.exp(v2 - v1)
    w1 = gate_group / (1.0 + e2)
    w2 = gate_group * e2 / (1.0 + e2)
    comb_ref[...] = jnp.where(lane == i1, w1, 0.0) + jnp.where(lane == i2, w2, 0.0)


def _out_proj(x, oa, ob, oc, w, g, wr):
    n, d = x.shape
    tm = ROW_TILE
    widths = (oa.shape[1], ob.shape[1], oc.shape[1])
    row = lambda width: pl.BlockSpec((tm, width), lambda i: (i, 0))
    full = lambda a: pl.BlockSpec(a.shape, lambda i: (0, 0))
    g2 = g.reshape(1, d)
    return pl.pallas_call(
        functools.partial(_out_proj_kernel, widths),
        grid=(n // tm,),
        in_specs=[row(d), row(widths[0]), row(widths[1]), row(widths[2]), full(w), full(g2), full(wr)],
        out_specs=[row(d), row(d), row(LANES)],
        out_shape=[jax.ShapeDtypeStruct((n, d), F32), jax.ShapeDtypeStruct((n, d), BF16),
                   jax.ShapeDtypeStruct((n, LANES), F32)],
        compiler_params=_cparams("parallel"),
        name="out_proj",
    )(x, oa, ob, oc, w, g2, wr)


def _moe_kernel(x1_ref, h_ref, comb_ref, wg_ref, wu_ref, wd_ref, o_ref, acc_ref):
    e = pl.program_id(1)

    @pl.when(e == 0)
    def _():
        acc_ref[...] = x1_ref[...]

    h = h_ref[...]
    comb = comb_ref[...]
    lane = lax.broadcasted_iota(jnp.int32, comb.shape, 1)
    c = jnp.sum(jnp.where(lane == MOE_GROUPS + e, comb, 0.0), axis=-1, keepdims=True)
    a = _dot(h, wg_ref[0])
    u = _dot(h, wu_ref[0])
    he = (a * _sigmoid(a)) * u * c
    acc_ref[...] += _dot(he.astype(BF16), wd_ref[0])

    @pl.when(e == pl.num_programs(1) - 1)
    def _():
        o_ref[...] = acc_ref[...]


def _moe(x1, h2, comb, wg, wu, wd):
    n, d = x1.shape
    ne, _, ff = wg.shape
    tm = MOE_ROW_TILE
    return pl.pallas_call(
        _moe_kernel,
        grid=(n // tm, ne),
        in_specs=[pl.BlockSpec((tm, d), lambda i, e: (i, 0)),
                  pl.BlockSpec((tm, d), lambda i, e: (i, 0)),
                  pl.BlockSpec((tm, LANES), lambda i, e: (i, 0)),
                  pl.BlockSpec((1, d, ff), lambda i, e: (e, 0, 0)),
                  pl.BlockSpec((1, d, ff), lambda i, e: (e, 0, 0)),
                  pl.BlockSpec((1, ff, d), lambda i, e: (e, 0, 0))],
        out_specs=pl.BlockSpec((tm, d), lambda i, e: (i, 0)),
        out_shape=jax.ShapeDtypeStruct((n, d), F32),
        scratch_shapes=[pltpu.VMEM((tm, d), F32)],
        compiler_params=_cparams("parallel", "arbitrary"),
        name="moe",
    )(x1, h2, comb, wg, wu, wd)


def _pad_cols(w, width):
    return jnp.pad(w, ((0, 0), (0, width - w.shape[1])))


def kernel(x, attn_norm_g, w_in, gla_gk_w2, gla_gk_b, gla_norm_g, moba_qnorm_g, moba_knorm_g,
           hgrn_lb_param, hgrn_norm_g, w_out, ffn_norm_g, w_router_group, w_router_expert,
           w_exp_gate, w_exp_up, w_exp_down):
    batch, seq, d = x.shape
    depth = w_in.shape[0]
    lowrank, gla_kw = gla_gk_w2.shape[1:]
    gla_dv = gla_norm_g.shape[1]
    moba_hd = moba_qnorm_g.shape[1]
    hgrn_kw = hgrn_lb_param.shape[1]
    hgrn_dv = hgrn_norm_g.shape[1]
    mix_w = w_out.shape[1]
    hgrn_heads = 4
    gla_heads = 4
    gla_vw = gla_heads * gla_dv
    hgrn_vw = hgrn_heads * hgrn_dv
    moba_w = mix_w - gla_vw - hgrn_vw
    moba_heads = moba_w // moba_hd
    gla_dk = gla_kw // gla_heads
    hgrn_dk = hgrn_kw // hgrn_heads

    splits = (gla_kw, gla_kw, gla_vw, gla_vw, lowrank, moba_w, moba_w, moba_w,
              hgrn_kw, hgrn_kw, hgrn_vw, hgrn_vw)
    offs = np.concatenate([[0], np.cumsum(splits)]).tolist()
    col = lambda w, i: w[:, offs[i]:offs[i + 1]]

    xf = x.reshape(batch * seq, d)
    for l in range(depth):
        w = w_in[l]
        wcat = jnp.concatenate(
            [col(w, 0), col(w, 1), col(w, 2), col(w, 3), _pad_cols(col(w, 4), LANES),
             col(w, 5), col(w, 6), col(w, 7), col(w, 8), col(w, 10), col(w, 11), col(w, 9)],
            axis=1).astype(BF16)
        g_gla = 2 * gla_kw + 2 * gla_vw
        g_hg = hgrn_kw + 2 * hgrn_vw
        bounds = np.cumsum([0, g_gla, LANES, 3 * moba_w, g_hg, hgrn_kw]).tolist()
        groups = tuple((bounds[i], bounds[i + 1]) for i in range(5))
        y_gla, y_lr, y_moba, y_hg, y_f = _in_proj(
            xf, attn_norm_g[l], wcat, groups, (BF16, F32, BF16, BF16, F32))

        w2 = jnp.pad(gla_gk_w2[l], ((0, LANES - lowrank), (0, 0)))
        o_a = _lin_call(
            functools.partial(_gla_kernel, gla_heads, gla_dk, gla_dv), "gla",
            batch, seq, gla_heads, gla_dk, gla_dv, (y_gla, y_lr),
            (w2, gla_gk_b[l].reshape(1, gla_kw), jnp.tile(gla_norm_g[l], gla_heads).reshape(1, gla_vw)))
        o_c = _lin_call(
            functools.partial(_hgrn_kernel, hgrn_heads, hgrn_dk, hgrn_dv, l), "hgrn",
            batch, seq, hgrn_heads, hgrn_dk, hgrn_dv, (y_hg, y_f),
            (hgrn_lb_param, jnp.tile(hgrn_norm_g[l], hgrn_heads).reshape(1, hgrn_vw)))

        qa, ka, vt = _moba_prep(y_moba, moba_qnorm_g[l], moba_knorm_g[l], batch, seq, moba_heads, moba_hd)
        o_b = _moba_attn(qa, ka, vt, batch, seq, moba_heads, moba_hd)

        wr = _pad_cols(jnp.concatenate([w_router_group[l], w_router_expert[l]], axis=1), LANES)
        x1, h2, comb = _out_proj(xf, o_a, o_b, o_c, w_out[l].astype(BF16), ffn_norm_g[l], wr)
        xf = _moe(x1, h2, comb, w_exp_gate[l].astype(BF16), w_exp_up[l].astype(BF16),
                  w_exp_down[l].astype(BF16))
    return xf.reshape(batch, seq, d)
```

```python
import functools

import numpy as np
import jax
import jax.numpy as jnp
from jax import lax
from jax.experimental import pallas as pl
from jax.experimental.pallas import tpu as pltpu

F32 = jnp.float32
BF16 = jnp.bfloat16

NORM_EPS = 1e-6
GLA_GATE_TAU = 16.0
ROPE_THETA = 500000.0
MOBA_BLOCK = 256
MOBA_TOPK = 3
MOE_GROUPS = 4
MOE_EXPERTS_PER_GROUP = 4
MOE_TOPK = 2

LANES = 128
VMEM_LIMIT = 56 * 1024 * 1024
LIN_CHUNK = 64
LIN_TILE = 512
LIN_GROUP = 8
ROW_TILE = 512
MOE_ROW_TILE = 1024
NEG = -1e30


def _cparams(*sem):
    return pltpu.CompilerParams(dimension_semantics=sem, vmem_limit_bytes=VMEM_LIMIT)


def _sigmoid(x):
    return 1.0 / (1.0 + jnp.exp(-x))


def _dot(a, b):
    return jnp.dot(a, b, preferred_element_type=F32)


def _dot_nt(a, b):
    return lax.dot_general(a, b, (((1,), (1,)), ((), ())), preferred_element_type=F32)


def _dot_tn(a, b):
    return lax.dot_general(a, b, (((0,), (0,)), ((), ())), preferred_element_type=F32)


def _hi_lo(x):
    hi = x.astype(BF16)
    return hi, (x - hi.astype(F32)).astype(BF16)


def _dot3(a, b, nt=False):
    a_hi, a_lo = _hi_lo(a)
    b_hi, b_lo = _hi_lo(b)
    lhs = jnp.concatenate([a_hi, a_hi, a_lo], axis=1)
    if nt:
        return _dot_nt(lhs, jnp.concatenate([b_hi, b_lo, b_hi], axis=1))
    return _dot(lhs, jnp.concatenate([b_hi, b_lo, b_hi], axis=0))


def _in_proj_kernel(groups, x_ref, g_ref, w_ref, *out_refs):
    x = x_ref[...]
    ms = jnp.mean(x * x, axis=-1, keepdims=True)
    h = (x * lax.rsqrt(ms + NORM_EPS) * g_ref[...]).astype(BF16)
    for (a, b), o_ref in zip(groups, out_refs):
        o_ref[...] = _dot(h, w_ref[:, a:b]).astype(o_ref.dtype)


def _in_proj(x, g, w, groups, dtypes):
    n, d = x.shape
    tm = ROW_TILE
    out_shape = [jax.ShapeDtypeStruct((n, b - a), dt) for (a, b), dt in zip(groups, dtypes)]
    out_specs = [pl.BlockSpec((tm, b - a), lambda i: (i, 0)) for (a, b) in groups]
    return pl.pallas_call(
        functools.partial(_in_proj_kernel, groups),
        grid=(n // tm,),
        in_specs=[pl.BlockSpec((tm, d), lambda i: (i, 0)),
                  pl.BlockSpec((1, d), lambda i: (0, 0)),
                  pl.BlockSpec(w.shape, lambda i: (0, 0))],
        out_specs=out_specs,
        out_shape=out_shape,
        compiler_params=_cparams("parallel"),
        name="in_proj",
    )(x, g.reshape(1, d), w)


def _lin_constants(heads, dk, dv):
    L = LIN_CHUNK
    nlev = int(np.log2(L))
    K, V = heads * dk, heads * dv
    i = np.arange(L)[:, None]
    t = np.arange(L)[None, :]
    w_rows = [(t <= i), (t > i)]
    masks = []
    for lev in range(nlev):
        b = L >> lev
        half = b // 2
        r = (i // b) * b + half - 1
        w_rows.append((t > np.minimum(i, r)) & (t <= np.maximum(i, r)))
        j = t
        masks.append((i // b == j // b) & (i % b >= half) & (j % b < half))
    masks.append(i == t)
    w_all = np.concatenate(w_rows, axis=0).astype(np.float32)
    m_all = np.stack([np.tile(m, (1, heads)) for m in masks]).astype(np.float32)
    rh = np.repeat(np.arange(heads), L)[:, None]
    bdk = (rh == np.repeat(np.arange(heads), dk)[None, :]).astype(np.float32)
    bdv = (rh == np.repeat(np.arange(heads), dv)[None, :]).astype(np.float32)
    bds = (np.repeat(np.arange(heads), dv)[:, None]
           == np.repeat(np.arange(heads), dk)[None, :]).astype(np.float32)
    ones_v = (np.repeat(np.arange(heads), dv)[:, None]
              == np.repeat(np.arange(heads), dv)[None, :]).astype(np.float32)
    return (jnp.asarray(w_all, BF16), jnp.asarray(m_all, F32), jnp.asarray(bdk, BF16),
            jnp.asarray(bdv, BF16), jnp.asarray(bds, F32), jnp.asarray(ones_v, BF16))


def _lin_group(q, k, v, la, g, gain, st_ref, w_ref, m_ref, bdk_ref, bdv_ref, bds_ref, ones_ref,
               heads, dv):
    L = LIN_CHUNK
    G = q.shape[0] // L
    nlev = m_ref.shape[0] - 1
    K = q.shape[1]
    chunks = [slice(c * L, (c + 1) * L) for c in range(G)]
    lacat = jnp.concatenate(_hi_lo(la), axis=1)
    v_bf = v.astype(BF16)
    es = []
    for c in chunks:
        z2 = _dot(w_ref[...], lacat[c])
        es.append(jnp.exp(z2[:, :K] + z2[:, K:]))
    bdk = bdk_ref[...]
    scores = [None] * G
    for lev in range(nlev + 1):
        for i, c in enumerate(chunks):
            if lev < nlev:
                el = es[i][(2 + lev) * L:(3 + lev) * L]
                ql = (q[c] * el).astype(BF16)
                kl = (k[c] * el).astype(BF16)
            else:
                ql = q[c].astype(BF16)
                kl = k[c].astype(BF16)
            kbd = jnp.concatenate([kl] * heads, axis=0) * bdk
            s = _dot_nt(ql, kbd) * m_ref[lev]
            scores[i] = s if scores[i] is None else scores[i] + s
    bdv = bdv_ref[...]
    intra, kvs = [], []
    for i, c in enumerate(chunks):
        vbd = jnp.concatenate([v_bf[c]] * heads, axis=0) * bdv
        intra.append(_dot(scores[i].astype(BF16), vbd))
        kb = (k[c] * es[i][L:2 * L]).astype(BF16)
        kvs.append(_dot_tn(v_bf[c], kb) * bds_ref[...])
    st = st_ref[...]
    outs = []
    for i, c in enumerate(chunks):
        outs.append(intra[i] + _dot_nt((q[c] * es[i][0:L]).astype(BF16), st.astype(BF16)))
        st = st * es[i][L - 1:L] + kvs[i]
    st_ref[...] = st
    o = jnp.concatenate(outs, axis=0)
    ms = _dot((o * o).astype(BF16), ones_ref[...]) * (1.0 / dv)
    gf = g.astype(F32)
    return o * lax.rsqrt(ms + NORM_EPS) * gain * (gf * _sigmoid(gf))


def _gla_kernel(heads, dk, dv, x_ref, lr_ref, w2_ref, b_ref, gain_ref,
                w_ref, m_ref, bdk_ref, bdv_ref, bds_ref, ones_ref, o_ref, st_ref):
    K, V = heads * dk, heads * dv
    L = LIN_CHUNK

    @pl.when(pl.program_id(1) == 0)
    def _():
        st_ref[...] = jnp.zeros_like(st_ref)

    gain = gain_ref[...]
    w2 = w2_ref[...]
    bias = b_ref[...]

    R = LIN_GROUP * L

    def body(c, carry):
        rows = pl.ds(pl.multiple_of(c * R, R), R)
        q = x_ref[rows, 0:K].astype(F32) * (dk ** -0.5)
        k = x_ref[rows, K:2 * K].astype(F32)
        v = x_ref[rows, 2 * K:2 * K + V]
        g = x_ref[rows, 2 * K + V:2 * K + 2 * V]
        gk = _dot3(lr_ref[rows, :], w2) + bias
        la = (jnp.minimum(gk, 0.0) - jnp.log1p(jnp.exp(-jnp.abs(gk)))) * (1.0 / GLA_GATE_TAU)
        o_ref[rows, :] = _lin_group(q, k, v, la, g, gain, st_ref, w_ref, m_ref, bdk_ref, bdv_ref,
                                    bds_ref, ones_ref, heads, dv).astype(o_ref.dtype)
        return carry

    lax.fori_loop(0, x_ref.shape[0] // R, body, 0)


def _hgrn_kernel(heads, dk, dv, layer, x_ref, f_ref, lbp_ref, gain_ref,
                 w_ref, m_ref, bdk_ref, bdv_ref, bds_ref, ones_ref, o_ref, st_ref):
    K, V = heads * dk, heads * dv
    L = LIN_CHUNK

    @pl.when(pl.program_id(1) == 0)
    def _():
        st_ref[...] = jnp.zeros_like(st_ref)

    lbp = lbp_ref[...]
    depth = lbp.shape[0]
    mx = lbp[0:1]
    for r in range(1, depth):
        mx = jnp.maximum(mx, lbp[r:r + 1])
    ex = [jnp.exp(lbp[r:r + 1] - mx) for r in range(depth)]
    den = ex[0]
    for r in range(1, depth):
        den = den + ex[r]
    lb = jnp.zeros_like(den)
    for r in range(1, layer + 1):
        lb = lb + ex[r] / den
    gain = gain_ref[...]

    R = LIN_GROUP * L

    def body(c, carry):
        rows = pl.ds(pl.multiple_of(c * R, R), R)
        cq = x_ref[rows, 0:K].astype(F32)
        q = cq * _sigmoid(cq)
        v = x_ref[rows, K:K + V]
        g = x_ref[rows, K + V:K + 2 * V]
        f = f_ref[rows, :]
        la = jnp.log(lb + (1.0 - lb) * _sigmoid(f))
        k = (1.0 - lb) * _sigmoid(-f)
        o_ref[rows, :] = _lin_group(q, k, v, la, g, gain, st_ref, w_ref, m_ref, bdk_ref, bdv_ref,
                                    bds_ref, ones_ref, heads, dv).astype(o_ref.dtype)
        return carry

    lax.fori_loop(0, x_ref.shape[0] // R, body, 0)


def _lin_call(kernel_fn, name, batch, seq, heads, dk, dv, row_inputs, small_inputs):
    T = LIN_TILE
    nt = seq // T
    consts = _lin_constants(heads, dk, dv)
    row_specs = [pl.BlockSpec((T, a.shape[1]), lambda b, t: (b * nt + t, 0)) for a in row_inputs]
    full = lambda a: pl.BlockSpec(a.shape, lambda b, t: (0,) * a.ndim)
    return pl.pallas_call(
        kernel_fn,
        grid=(batch, nt),
        in_specs=row_specs + [full(a) for a in small_inputs] + [full(a) for a in consts],
        out_specs=pl.BlockSpec((T, heads * dv), lambda b, t: (b * nt + t, 0)),
        out_shape=jax.ShapeDtypeStruct((batch * seq, heads * dv), BF16),
        scratch_shapes=[pltpu.VMEM((heads * dv, heads * dk), F32)],
        compiler_params=_cparams("parallel", "arbitrary"),
        name=name,
    )(*row_inputs, *small_inputs, *consts)


def _moba_constants(heads, hd, seq):
    W = heads * hd
    nb = LANES // heads
    rot = hd // 4
    half = rot // 2
    inv_freq = ROPE_THETA ** (-(np.arange(0, rot, 2, dtype=np.float64) / rot))
    ang = np.arange(seq, dtype=np.float64)[:, None] * inv_freq[None, :]
    cos, sin = np.cos(ang), np.sin(ang)
    c = np.ones((seq, hd), np.float32)
    sp = np.zeros((seq, hd), np.float32)
    sm = np.zeros((seq, hd), np.float32)
    c[:, :half] = cos
    c[:, half:rot] = cos
    sm[:, :half] = -sin
    sp[:, half:rot] = sin
    reps = LANES // hd
    tabs = [jnp.asarray(np.tile(a, (1, reps)), F32) for a in (c, sp, sm)]
    hl = np.repeat(np.arange(heads), hd)
    ones_h = (hl[:, None] == hl[None, :]).astype(np.float32)
    hm = (np.arange(heads)[:, None] == hl[None, :]).astype(np.float32)
    pb = np.zeros((LANES, heads * LANES), np.float32)
    gl = np.arange(LANES)
    pb[gl, (gl % heads) * LANES + hd + gl // heads] = 1.0
    return tabs, (jnp.asarray(ones_h, BF16), jnp.asarray(hm, F32), jnp.asarray(pb, BF16))


def _moba_prep_kernel(heads, hd, x_ref, qg_ref, kg_ref, c_ref, sp_ref, sm_ref,
                      ones_ref, hm_ref, pexp_ref, qa_ref, ka_ref, vt_ref, kmt_ref):
    W = heads * hd
    t = pl.program_id(1)
    half = hd // 8
    reps = W // LANES

    @pl.when(t == 0)
    def _():
        kmt_ref[...] = jnp.zeros_like(kmt_ref)

    wide = lambda ref: jnp.concatenate([ref[...]] * reps, axis=1)
    c, sp, sm = wide(c_ref), wide(sp_ref), wide(sm_ref)

    def norm_rot(x, gain):
        ms = _dot((x * x).astype(BF16), ones_ref[...]) * (1.0 / hd)
        y = x * lax.rsqrt(ms + NORM_EPS) * gain
        return y * c + pltpu.roll(y, half, 1) * sp + pltpu.roll(y, W - half, 1) * sm

    q = norm_rot(x_ref[:, 0:W].astype(F32), qg_ref[...])
    k = norm_rot(x_ref[:, W:2 * W].astype(F32), kg_ref[...])

    gate = _dot3(q, kmt_ref[...], nt=True)
    lane = lax.broadcasted_iota(jnp.int32, gate.shape, 1)
    blk = lax.shift_right_logical(lane, int(np.log2(heads)))
    valid = blk < t
    gate = jnp.where(valid, gate, -jnp.inf)
    rank = jnp.zeros(gate.shape, F32)
    nslots = LANES // heads
    for s in range(1, nslots):
        other = pltpu.roll(gate, s * heads, 1)
        ahead = (other > gate) | ((other == gate) & (blk >= s))
        rank = rank + ahead.astype(F32)
    sel = valid & (rank < float(MOBA_TOPK))
    bias = jnp.where(sel | (blk == t), 0.0, NEG)

    bias_slab = _dot(bias.astype(BF16), pexp_ref[...])
    qs = q * (hd ** -0.5)
    lane1 = lax.broadcasted_iota(jnp.int32, (q.shape[0], LANES), 1)
    onehot = jnp.where(lane1 == hd + t, 1.0, 0.0)
    per_vreg = LANES // hd
    for h in range(heads):
        src = slice((h // per_vreg) * LANES, (h // per_vreg + 1) * LANES)
        dst = slice(h * LANES, (h + 1) * LANES)
        shift = (LANES - (h % per_vreg) * hd) % LANES
        move = (lambda a: a) if shift == 0 else (lambda a: pltpu.roll(a, shift, 1))
        qa_ref[:, dst] = jnp.where(lane1 < hd, move(qs[:, src]), bias_slab[:, dst]).astype(qa_ref.dtype)
        ka_ref[:, dst] = jnp.where(lane1 < hd, move(k[:, src]), onehot).astype(ka_ref.dtype)

    kmean = jnp.mean(k, axis=0, keepdims=True)
    row0 = pl.multiple_of(t * heads, heads)
    kmt_ref[pl.ds(row0, heads), :] = kmean * hm_ref[...]

    vt = x_ref[:, 2 * W:3 * W].astype(F32).T
    ones = jnp.ones((LANES - hd, vt.shape[1]), vt_ref.dtype)
    for h in range(heads):
        vt_ref[0, h, 0, 0:hd, :] = vt[h * hd:(h + 1) * hd].astype(vt_ref.dtype)
        vt_ref[0, h, 0, hd:LANES, :] = ones


def _moba_prep(x, qg, kg, batch, seq, heads, hd):
    W = heads * hd
    nt = seq // MOBA_BLOCK
    tabs, consts = _moba_constants(heads, hd, seq)
    full = lambda a: pl.BlockSpec(a.shape, lambda b, t: (0,) * a.ndim)
    tile_g = lambda g: jnp.tile(g.astype(F32), heads).reshape(1, W)
    small = (tile_g(qg), tile_g(kg))
    out_sd = jax.ShapeDtypeStruct((batch * seq, heads * LANES), BF16)
    out_spec = pl.BlockSpec((MOBA_BLOCK, heads * LANES), lambda b, t: (b * nt + t, 0))
    vt_sd = jax.ShapeDtypeStruct((batch, heads, nt, LANES, MOBA_BLOCK), BF16)
    vt_spec = pl.BlockSpec((1, heads, 1, LANES, MOBA_BLOCK), lambda b, t: (b, 0, t, 0, 0))
    return pl.pallas_call(
        functools.partial(_moba_prep_kernel, heads, hd),
        grid=(batch, nt),
        in_specs=[pl.BlockSpec((MOBA_BLOCK, 3 * W), lambda b, t: (b * nt + t, 0))]
        + [full(a) for a in small]
        + [pl.BlockSpec((MOBA_BLOCK, LANES), lambda b, t: (t, 0)) for _ in tabs]
        + [full(a) for a in consts],
        out_specs=[out_spec, out_spec, vt_spec],
        out_shape=[out_sd, out_sd, vt_sd],
        scratch_shapes=[pltpu.VMEM((LANES, W), F32)],
        compiler_params=_cparams("parallel", "arbitrary"),
        name="moba_prep",
    )(x, *small, *tabs, *consts)


def _moba_attn_kernel(hd, qa_ref, ka_ref, vt_ref, o_ref, acc_ref, sa_ref, sb_ref):
    t = pl.program_id(1)
    tq = qa_ref.shape[0]
    blk = MOBA_BLOCK
    nh = vt_ref.shape[1]

    def scores_to(dst_ref, n, hh):
        cols = slice(hh * LANES, (hh + 1) * LANES)
        rows = pl.ds(pl.multiple_of(n * blk, blk), blk)
        dst_ref[hh] = _dot_nt(ka_ref[rows, cols], qa_ref[:, cols])

    def update(s, m, n, hh):
        m_new = jnp.maximum(m, jnp.max(s, axis=0, keepdims=True))
        alpha = jnp.exp(m - m_new)
        p = jnp.exp(s - m_new).astype(BF16)
        acc_ref[hh] = alpha * acc_ref[hh] + _dot(vt_ref[0, hh, n], p)
        return m_new

    def step(cur_ref, nxt_ref, n, ms):
        new = []
        for hh in range(nh):
            scores_to(nxt_ref, n + 1, hh)
            new.append(update(cur_ref[hh], ms[hh], n, hh))
        return tuple(new)

    acc_ref[...] = jnp.zeros_like(acc_ref)
    for hh in range(nh):
        scores_to(sa_ref, 0, hh)

    def pair(i, ms):
        ms = step(sa_ref, sb_ref, 2 * i, ms)
        return step(sb_ref, sa_ref, 2 * i + 1, ms)

    m0 = jnp.full((1, tq), 2 * NEG, F32)
    ms = lax.fori_loop(0, t // 2, pair, (m0,) * nh)

    def odd_block(ms):
        new = []
        for hh in range(nh):
            new.append(update(sa_ref[hh], ms[hh], t - 1, hh))
            scores_to(sa_ref, t, hh)
        return tuple(new)

    ms = lax.cond(t % 2 == 1, odd_block, lambda ms: ms, ms)

    key = lax.broadcasted_iota(jnp.int32, (blk, tq), 0)
    qry = lax.broadcasted_iota(jnp.int32, (blk, tq), 1)
    outs = []
    for hh in range(nh):
        s = jnp.where(key <= qry, sa_ref[hh], NEG)
        update(s, ms[hh], t, hh)
        a = acc_ref[hh]
        outs.append(a[0:hd] / a[hd:hd + 1])
    o_ref[...] = jnp.concatenate(outs, axis=0).T.astype(o_ref.dtype)


def _moba_attn(qa, ka, vt, batch, seq, heads, hd):
    nt = seq // MOBA_BLOCK
    return pl.pallas_call(
        functools.partial(_moba_attn_kernel, hd),
        grid=(batch, nt),
        in_specs=[pl.BlockSpec((MOBA_BLOCK, heads * LANES), lambda b, t: (b * nt + t, 0)),
                  pl.BlockSpec((seq, heads * LANES), lambda b, t: (b, 0)),
                  pl.BlockSpec((1, heads, nt, LANES, MOBA_BLOCK), lambda b, t: (b, 0, 0, 0, 0))],
        out_specs=pl.BlockSpec((MOBA_BLOCK, heads * hd), lambda b, t: (b * nt + t, 0)),
        out_shape=jax.ShapeDtypeStruct((batch * seq, heads * hd), BF16),
        scratch_shapes=[pltpu.VMEM((heads, LANES, MOBA_BLOCK), F32),
                        pltpu.VMEM((heads, MOBA_BLOCK, MOBA_BLOCK), F32),
                        pltpu.VMEM((heads, MOBA_BLOCK, MOBA_BLOCK), F32)],
        compiler_params=_cparams("parallel", "arbitrary"),
        name="moba_attn",
    )(qa, ka, vt)


def _out_proj_kernel(widths, x_ref, oa_ref, ob_ref, oc_ref, w_ref, g_ref, wr_ref,
                     x1_ref, h2_ref, comb_ref):
    acc = x_ref[...]
    r0 = 0
    for o_ref, wd in zip((oa_ref, ob_ref, oc_ref), widths):
        acc = acc + _dot(o_ref[...], w_ref[r0:r0 + wd, :])
        r0 += wd
    x1_ref[...] = acc
    ms = jnp.mean(acc * acc, axis=-1, keepdims=True)
    h2 = acc * lax.rsqrt(ms + NORM_EPS) * g_ref[...]
    h2_ref[...] = h2.astype(h2_ref.dtype)

    G, E = MOE_GROUPS, MOE_EXPERTS_PER_GROUP
    logits = _dot3(h2, wr_ref[...])
    lane = lax.broadcasted_iota(jnp.int32, logits.shape, 1).astype(F32)
    big = float(LANES)
    is_g = lane < G
    lg = jnp.where(is_g, logits, -jnp.inf)
    mg = jnp.max(lg, axis=-1, keepdims=True)
    gate_group = 1.0 / jnp.sum(jnp.exp(lg - mg), axis=-1, keepdims=True)
    g_sel = jnp.min(jnp.where(lg == mg, lane, big), axis=-1, keepdims=True)
    lo = G + E * g_sel
    in_grp = (lane >= lo) & (lane < lo + E)
    v = jnp.where(in_grp, logits, -jnp.inf)
    v1 = jnp.max(v, axis=-1, keepdims=True)
    i1 = jnp.min(jnp.where(v == v1, lane, big), axis=-1, keepdims=True)
    v = jnp.where(lane == i1, -jnp.inf, v)
    v2 = jnp.max(v, axis=-1, keepdims=True)
    i2 = jnp.min(jnp.where(v == v2, lane, big), axis=-1, keepdims=True)
    e2 = jnp.exp(v2 - v1)
    w1 = gate_group / (1.0 + e2)
    w2 = gate_group * e2 / (1.0 + e2)
    comb_ref[...] = jnp.where(lane == i1, w1, 0.0) + jnp.where(lane == i2, w2, 0.0)


def _out_proj(x, oa, ob, oc, w, g, wr):
    n, d = x.shape
    tm = ROW_TILE
    widths = (oa.shape[1], ob.shape[1], oc.shape[1])
    row = lambda width: pl.BlockSpec((tm, width), lambda i: (i, 0))
    full = lambda a: pl.BlockSpec(a.shape, lambda i: (0, 0))
    g2 = g.reshape(1, d)
    return pl.pallas_call(
        functools.partial(_out_proj_kernel, widths),
        grid=(n // tm,),
        in_specs=[row(d), row(widths[0]), row(widths[1]), row(widths[2]), full(w), full(g2), full(wr)],
        out_specs=[row(d), row(d), row(LANES)],
        out_shape=[jax.ShapeDtypeStruct((n, d), F32), jax.ShapeDtypeStruct((n, d), BF16),
                   jax.ShapeDtypeStruct((n, LANES), F32)],
        compiler_params=_cparams("parallel"),
        name="out_proj",
    )(x, oa, ob, oc, w, g2, wr)


def _moe_kernel(x1_ref, h_ref, comb_ref, wg_ref, wu_ref, wd_ref, o_ref, acc_ref):
    e = pl.program_id(1)

    @pl.when(e == 0)
    def _():
        acc_ref[...] = x1_ref[...]

    h = h_ref[...]
    comb = comb_ref[...]
    lane = lax.broadcasted_iota(jnp.int32, comb.shape, 1)
    c = jnp.sum(jnp.where(lane == MOE_GROUPS + e, comb, 0.0), axis=-1, keepdims=True)
    a = _dot(h, wg_ref[0])
    u = _dot(h, wu_ref[0])
    he = (a * _sigmoid(a)) * u * c
    acc_ref[...] += _dot(he.astype(BF16), wd_ref[0])

    @pl.when(e == pl.num_programs(1) - 1)
    def _():
        o_ref[...] = acc_ref[...]


def _moe(x1, h2, comb, wg, wu, wd):
    n, d = x1.shape
    ne, _, ff = wg.shape
    tm = MOE_ROW_TILE
    return pl.pallas_call(
        _moe_kernel,
        grid=(n // tm, ne),
        in_specs=[pl.BlockSpec((tm, d), lambda i, e: (i, 0)),
                  pl.BlockSpec((tm, d), lambda i, e: (i, 0)),
                  pl.BlockSpec((tm, LANES), lambda i, e: (i, 0)),
                  pl.BlockSpec((1, d, ff), lambda i, e: (e, 0, 0)),
                  pl.BlockSpec((1, d, ff), lambda i, e: (e, 0, 0)),
                  pl.BlockSpec((1, ff, d), lambda i, e: (e, 0, 0))],
        out_specs=pl.BlockSpec((tm, d), lambda i, e: (i, 0)),
        out_shape=jax.ShapeDtypeStruct((n, d), F32),
        scratch_shapes=[pltpu.VMEM((tm, d), F32)],
        compiler_params=_cparams("parallel", "arbitrary"),
        name="moe",
    )(x1, h2, comb, wg, wu, wd)


def _pad_cols(w, width):
    return jnp.pad(w, ((0, 0), (0, width - w.shape[1])))


def kernel(x, attn_norm_g, w_in, gla_gk_w2, gla_gk_b, gla_norm_g, moba_qnorm_g, moba_knorm_g,
           hgrn_lb_param, hgrn_norm_g, w_out, ffn_norm_g, w_router_group, w_router_expert,
           w_exp_gate, w_exp_up, w_exp_down):
    batch, seq, d = x.shape
    depth = w_in.shape[0]
    lowrank, gla_kw = gla_gk_w2.shape[1:]
    gla_dv = gla_norm_g.shape[1]
    moba_hd = moba_qnorm_g.shape[1]
    hgrn_kw = hgrn_lb_param.shape[1]
    hgrn_dv = hgrn_norm_g.shape[1]
    mix_w = w_out.shape[1]
    hgrn_heads = 4
    gla_heads = 4
    gla_vw = gla_heads * gla_dv
    hgrn_vw = hgrn_heads * hgrn_dv
    moba_w = mix_w - gla_vw - hgrn_vw
    moba_heads = moba_w // moba_hd
    gla_dk = gla_kw // gla_heads
    hgrn_dk = hgrn_kw // hgrn_heads

    splits = (gla_kw, gla_kw, gla_vw, gla_vw, lowrank, moba_w, moba_w, moba_w,
              hgrn_kw, hgrn_kw, hgrn_vw, hgrn_vw)
    offs = np.concatenate([[0], np.cumsum(splits)]).tolist()
    col = lambda w, i: w[:, offs[i]:offs[i + 1]]

    xf = x.reshape(batch * seq, d)
    for l in range(depth):
        w = w_in[l]
        wcat = jnp.concatenate(
            [col(w, 0), col(w, 1), col(w, 2), col(w, 3), _pad_cols(col(w, 4), LANES),
             col(w, 5), col(w, 6), col(w, 7), col(w, 8), col(w, 10), col(w, 11), col(w, 9)],
            axis=1).astype(BF16)
        g_gla = 2 * gla_kw + 2 * gla_vw
        g_hg = hgrn_kw + 2 * hgrn_vw
        bounds = np.cumsum([0, g_gla, LANES, 3 * moba_w, g_hg, hgrn_kw]).tolist()
        groups = tuple((bounds[i], bounds[i + 1]) for i in range(5))
        y_gla, y_lr, y_moba, y_hg, y_f = _in_proj(
            xf, attn_norm_g[l], wcat, groups, (BF16, F32, BF16, BF16, F32))

        w2 = jnp.pad(gla_gk_w2[l], ((0, LANES - lowrank), (0, 0)))
        o_a = _lin_call(
            functools.partial(_gla_kernel, gla_heads, gla_dk, gla_dv), "gla",
            batch, seq, gla_heads, gla_dk, gla_dv, (y_gla, y_lr),
            (w2, gla_gk_b[l].reshape(1, gla_kw), jnp.tile(gla_norm_g[l], gla_heads).reshape(1, gla_vw)))
        o_c = _lin_call(
            functools.partial(_hgrn_kernel, hgrn_heads, hgrn_dk, hgrn_dv, l), "hgrn",
            batch, seq, hgrn_heads, hgrn_dk, hgrn_dv, (y_hg, y_f),
            (hgrn_lb_param, jnp.tile(hgrn_norm_g[l], hgrn_heads).reshape(1, hgrn_vw)))

        qa, ka, vt = _moba_prep(y_moba, moba_qnorm_g[l], moba_knorm_g[l], batch, seq, moba_heads, moba_hd)
        o_b = _moba_attn(qa, ka, vt, batch, seq, moba_heads, moba_hd)

        wr = _pad_cols(jnp.concatenate([w_router_group[l], w_router_expert[l]], axis=1), LANES)
        x1, h2, comb = _out_proj(xf, o_a, o_b, o_c, w_out[l].astype(BF16), ffn_norm_g[l], wr)
        xf = _moe(x1, h2, comb, w_exp_gate[l].astype(BF16), w_exp_up[l].astype(BF16),
                  w_exp_down[l].astype(BF16))
    return xf.reshape(batch, seq, d)
```

```python
import functools

import numpy as np
import jax
import jax.numpy as jnp
from jax import lax
from jax.experimental import pallas as pl
from jax.experimental.pallas import tpu as pltpu

F32 = jnp.float32
BF16 = jnp.bfloat16

NORM_EPS = 1e-6
GLA_GATE_TAU = 16.0
ROPE_THETA = 500000.0
MOBA_BLOCK = 256
MOBA_TOPK = 3
MOE_GROUPS = 4
MOE_EXPERTS_PER_GROUP = 4
MOE_TOPK = 2

LANES = 128
VMEM_LIMIT = 56 * 1024 * 1024
LIN_CHUNK = 64
LIN_TILE = 512
LIN_GROUP = 8
ROW_TILE = 512
MOE_TILE = 256
NEG = -1e30


def _cparams(*sem):
    return pltpu.CompilerParams(dimension_semantics=sem, vmem_limit_bytes=VMEM_LIMIT)


def _sigmoid(x):
    return 1.0 / (1.0 + jnp.exp(-x))


def _dot(a, b):
    return jnp.dot(a, b, preferred_element_type=F32)


def _dot_nt(a, b):
    return lax.dot_general(a, b, (((1,), (1,)), ((), ())), preferred_element_type=F32)


def _dot_tn(a, b):
    return lax.dot_general(a, b, (((0,), (0,)), ((), ())), preferred_element_type=F32)


def _hi_lo(x):
    hi = x.astype(BF16)
    return hi, (x - hi.astype(F32)).astype(BF16)


def _dot3(a, b, nt=False):
    a_hi, a_lo = _hi_lo(a)
    b_hi, b_lo = _hi_lo(b)
    lhs = jnp.concatenate([a_hi, a_hi, a_lo], axis=1)
    if nt:
        return _dot_nt(lhs, jnp.concatenate([b_hi, b_lo, b_hi], axis=1))
    return _dot(lhs, jnp.concatenate([b_hi, b_lo, b_hi], axis=0))


def _in_proj_kernel(groups, x_ref, g_ref, w_ref, *out_refs):
    x = x_ref[...]
    ms = jnp.mean(x * x, axis=-1, keepdims=True)
    h = (x * lax.rsqrt(ms + NORM_EPS) * g_ref[...]).astype(BF16)
    for (a, b), o_ref in zip(groups, out_refs):
        o_ref[...] = _dot(h, w_ref[:, a:b]).astype(o_ref.dtype)


def _in_proj(x, g, w, groups, dtypes):
    n, d = x.shape
    tm = ROW_TILE
    out_shape = [jax.ShapeDtypeStruct((n, b - a), dt) for (a, b), dt in zip(groups, dtypes)]
    out_specs = [pl.BlockSpec((tm, b - a), lambda i: (i, 0)) for (a, b) in groups]
    return pl.pallas_call(
        functools.partial(_in_proj_kernel, groups),
        grid=(n // tm,),
        in_specs=[pl.BlockSpec((tm, d), lambda i: (i, 0)),
                  pl.BlockSpec((1, d), lambda i: (0, 0)),
                  pl.BlockSpec(w.shape, lambda i: (0, 0))],
        out_specs=out_specs,
        out_shape=out_shape,
        compiler_params=_cparams("parallel"),
        name="in_proj",
    )(x, g.reshape(1, d), w)


def _lin_constants(heads, dk, dv):
    L = LIN_CHUNK
    nlev = int(np.log2(L))
    K, V = heads * dk, heads * dv
    i = np.arange(L)[:, None]
    t = np.arange(L)[None, :]
    w_rows = [(t <= i), (t > i)]
    masks = []
    for lev in range(nlev):
        b = L >> lev
        half = b // 2
        r = (i // b) * b + half - 1
        w_rows.append((t > np.minimum(i, r)) & (t <= np.maximum(i, r)))
        j = t
        masks.append((i // b == j // b) & (i % b >= half) & (j % b < half))
    masks.append(i == t)
    w_all = np.concatenate(w_rows, axis=0).astype(np.float32)
    m_all = np.stack([np.tile(m, (1, heads)) for m in masks]).astype(np.float32)
    rh = np.repeat(np.arange(heads), L)[:, None]
    bdk = (rh == np.repeat(np.arange(heads), dk)[None, :]).astype(np.float32)
    bdv = (rh == np.repeat(np.arange(heads), dv)[None, :]).astype(np.float32)
    bds = (np.repeat(np.arange(heads), dv)[:, None]
           == np.repeat(np.arange(heads), dk)[None, :]).astype(np.float32)
    ones_v = (np.repeat(np.arange(heads), dv)[:, None]
              == np.repeat(np.arange(heads), dv)[None, :]).astype(np.float32)
    return (jnp.asarray(w_all, BF16), jnp.asarray(m_all, F32), jnp.asarray(bdk, BF16),
            jnp.asarray(bdv, BF16), jnp.asarray(bds, F32), jnp.asarray(ones_v, BF16))


def _lin_group(q, k, v, la, g, gain, st_ref, w_ref, m_ref, bdk_ref, bdv_ref, bds_ref, ones_ref,
               heads, dv):
    L = LIN_CHUNK
    G = q.shape[0] // L
    nlev = m_ref.shape[0] - 1
    K = q.shape[1]
    chunks = [slice(c * L, (c + 1) * L) for c in range(G)]
    lacat = jnp.concatenate(_hi_lo(la), axis=1)
    v_bf = v.astype(BF16)
    es = []
    for c in chunks:
        z2 = _dot(w_ref[...], lacat[c])
        es.append(jnp.exp(z2[:, :K] + z2[:, K:]))
    bdk = bdk_ref[...]
    scores = [None] * G
    for lev in range(nlev + 1):
        for i, c in enumerate(chunks):
            if lev < nlev:
                el = es[i][(2 + lev) * L:(3 + lev) * L]
                ql = (q[c] * el).astype(BF16)
                kl = (k[c] * el).astype(BF16)
            else:
                ql = q[c].astype(BF16)
                kl = k[c].astype(BF16)
            kbd = jnp.concatenate([kl] * heads, axis=0) * bdk
            s = _dot_nt(ql, kbd) * m_ref[lev]
            scores[i] = s if scores[i] is None else scores[i] + s
    bdv = bdv_ref[...]
    intra, kvs = [], []
    for i, c in enumerate(chunks):
        vbd = jnp.concatenate([v_bf[c]] * heads, axis=0) * bdv
        intra.append(_dot(scores[i].astype(BF16), vbd))
        kb = (k[c] * es[i][L:2 * L]).astype(BF16)
        kvs.append(_dot_tn(v_bf[c], kb) * bds_ref[...])
    st = st_ref[...]
    outs = []
    for i, c in enumerate(chunks):
        outs.append(intra[i] + _dot_nt((q[c] * es[i][0:L]).astype(BF16), st.astype(BF16)))
        st = st * es[i][L - 1:L] + kvs[i]
    st_ref[...] = st
    o = jnp.concatenate(outs, axis=0)
    ms = _dot((o * o).astype(BF16), ones_ref[...]) * (1.0 / dv)
    gf = g.astype(F32)
    return o * lax.rsqrt(ms + NORM_EPS) * gain * (gf * _sigmoid(gf))


def _gla_kernel(heads, dk, dv, x_ref, lr_ref, w2_ref, b_ref, gain_ref,
                w_ref, m_ref, bdk_ref, bdv_ref, bds_ref, ones_ref, o_ref, st_ref):
    K, V = heads * dk, heads * dv
    L = LIN_CHUNK

    @pl.when(pl.program_id(1) == 0)
    def _():
        st_ref[...] = jnp.zeros_like(st_ref)

    gain = gain_ref[...]
    w2 = w2_ref[...]
    bias = b_ref[...]

    R = LIN_GROUP * L

    def body(c, carry):
        rows = pl.ds(pl.multiple_of(c * R, R), R)
        q = x_ref[rows, 0:K].astype(F32) * (dk ** -0.5)
        k = x_ref[rows, K:2 * K].astype(F32)
        v = x_ref[rows, 2 * K:2 * K + V]
        g = x_ref[rows, 2 * K + V:2 * K + 2 * V]
        gk = _dot3(lr_ref[rows, :], w2) + bias
        la = (jnp.minimum(gk, 0.0) - jnp.log1p(jnp.exp(-jnp.abs(gk)))) * (1.0 / GLA_GATE_TAU)
        o_ref[rows, :] = _lin_group(q, k, v, la, g, gain, st_ref, w_ref, m_ref, bdk_ref, bdv_ref,
                                    bds_ref, ones_ref, heads, dv).astype(o_ref.dtype)
        return carry

    lax.fori_loop(0, x_ref.shape[0] // R, body, 0)


def _hgrn_kernel(heads, dk, dv, layer, x_ref, f_ref, lbp_ref, gain_ref,
                 w_ref, m_ref, bdk_ref, bdv_ref, bds_ref, ones_ref, o_ref, st_ref):
    K, V = heads * dk, heads * dv
    L = LIN_CHUNK

    @pl.when(pl.program_id(1) == 0)
    def _():
        st_ref[...] = jnp.zeros_like(st_ref)

    lbp = lbp_ref[...]
    depth = lbp.shape[0]
    mx = lbp[0:1]
    for r in range(1, depth):
        mx = jnp.maximum(mx, lbp[r:r + 1])
    ex = [jnp.exp(lbp[r:r + 1] - mx) for r in range(depth)]
    den = ex[0]
    for r in range(1, depth):
        den = den + ex[r]
    lb = jnp.zeros_like(den)
    for r in range(1, layer + 1):
        lb = lb + ex[r] / den
    gain = gain_ref[...]

    R = LIN_GROUP * L

    def body(c, carry):
        rows = pl.ds(pl.multiple_of(c * R, R), R)
        cq = x_ref[rows, 0:K].astype(F32)
        q = cq * _sigmoid(cq)
        v = x_ref[rows, K:K + V]
        g = x_ref[rows, K + V:K + 2 * V]
        f = f_ref[rows, :]
        la = jnp.log(lb + (1.0 - lb) * _sigmoid(f))
        k = (1.0 - lb) * _sigmoid(-f)
        o_ref[rows, :] = _lin_group(q, k, v, la, g, gain, st_ref, w_ref, m_ref, bdk_ref, bdv_ref,
                                    bds_ref, ones_ref, heads, dv).astype(o_ref.dtype)
        return carry

    lax.fori_loop(0, x_ref.shape[0] // R, body, 0)


def _lin_call(kernel_fn, name, batch, seq, heads, dk, dv, row_inputs, small_inputs):
    T = LIN_TILE
    nt = seq // T
    consts = _lin_constants(heads, dk, dv)
    row_specs = [pl.BlockSpec((T, a.shape[1]), lambda b, t: (b * nt + t, 0)) for a in row_inputs]
    full = lambda a: pl.BlockSpec(a.shape, lambda b, t: (0,) * a.ndim)
    return pl.pallas_call(
        kernel_fn,
        grid=(batch, nt),
        in_specs=row_specs + [full(a) for a in small_inputs] + [full(a) for a in consts],
        out_specs=pl.BlockSpec((T, heads * dv), lambda b, t: (b * nt + t, 0)),
        out_shape=jax.ShapeDtypeStruct((batch * seq, heads * dv), BF16),
        scratch_shapes=[pltpu.VMEM((heads * dv, heads * dk), F32)],
        compiler_params=_cparams("parallel", "arbitrary"),
        name=name,
    )(*row_inputs, *small_inputs, *consts)


def _moba_constants(heads, hd, seq):
    W = heads * hd
    nb = LANES // heads
    rot = hd // 4
    half = rot // 2
    inv_freq = ROPE_THETA ** (-(np.arange(0, rot, 2, dtype=np.float64) / rot))
    ang = np.arange(seq, dtype=np.float64)[:, None] * inv_freq[None, :]
    cos, sin = np.cos(ang), np.sin(ang)
    c = np.ones((seq, hd), np.float32)
    sp = np.zeros((seq, hd), np.float32)
    sm = np.zeros((seq, hd), np.float32)
    c[:, :half] = cos
    c[:, half:rot] = cos
    sm[:, :half] = -sin
    sp[:, half:rot] = sin
    reps = LANES // hd
    tabs = [jnp.asarray(np.tile(a, (1, reps)), F32) for a in (c, sp, sm)]
    hl = np.repeat(np.arange(heads), hd)
    ones_h = (hl[:, None] == hl[None, :]).astype(np.float32)
    hm = (np.arange(heads)[:, None] == hl[None, :]).astype(np.float32)
    pb = np.zeros((LANES, heads * LANES), np.float32)
    gl = np.arange(LANES)
    pb[gl, (gl % heads) * LANES + hd + gl // heads] = 1.0
    return tabs, (jnp.asarray(ones_h, BF16), jnp.asarray(hm, F32), jnp.asarray(pb, BF16))


def _moba_prep_kernel(heads, hd, x_ref, qg_ref, kg_ref, c_ref, sp_ref, sm_ref,
                      ones_ref, hm_ref, pexp_ref, qa_ref, ka_ref, vt_ref, kmt_ref):
    W = heads * hd
    t = pl.program_id(1)
    half = hd // 8
    reps = W // LANES

    @pl.when(t == 0)
    def _():
        kmt_ref[...] = jnp.zeros_like(kmt_ref)

    wide = lambda ref: jnp.concatenate([ref[...]] * reps, axis=1)
    c, sp, sm = wide(c_ref), wide(sp_ref), wide(sm_ref)

    def norm_rot(x, gain):
        ms = _dot((x * x).astype(BF16), ones_ref[...]) * (1.0 / hd)
        y = x * lax.rsqrt(ms + NORM_EPS) * gain
        return y * c + pltpu.roll(y, half, 1) * sp + pltpu.roll(y, W - half, 1) * sm

    q = norm_rot(x_ref[:, 0:W].astype(F32), qg_ref[...])
    k = norm_rot(x_ref[:, W:2 * W].astype(F32), kg_ref[...])

    gate = _dot3(q, kmt_ref[...], nt=True)
    lane = lax.broadcasted_iota(jnp.int32, gate.shape, 1)
    blk = lax.shift_right_logical(lane, int(np.log2(heads)))
    valid = blk < t
    gate = jnp.where(valid, gate, -jnp.inf)
    rank = jnp.zeros(gate.shape, F32)
    nslots = LANES // heads
    for s in range(1, nslots):
        other = pltpu.roll(gate, s * heads, 1)
        ahead = (other > gate) | ((other == gate) & (blk >= s))
        rank = rank + ahead.astype(F32)
    sel = valid & (rank < float(MOBA_TOPK))
    bias = jnp.where(sel | (blk == t), 0.0, NEG)

    bias_slab = _dot(bias.astype(BF16), pexp_ref[...])
    qs = q * (hd ** -0.5)
    lane1 = lax.broadcasted_iota(jnp.int32, (q.shape[0], LANES), 1)
    onehot = jnp.where(lane1 == hd + t, 1.0, 0.0)
    per_vreg = LANES // hd
    for h in range(heads):
        src = slice((h // per_vreg) * LANES, (h // per_vreg + 1) * LANES)
        dst = slice(h * LANES, (h + 1) * LANES)
        shift = (LANES - (h % per_vreg) * hd) % LANES
        move = (lambda a: a) if shift == 0 else (lambda a: pltpu.roll(a, shift, 1))
        qa_ref[:, dst] = jnp.where(lane1 < hd, move(qs[:, src]), bias_slab[:, dst]).astype(qa_ref.dtype)
        ka_ref[:, dst] = jnp.where(lane1 < hd, move(k[:, src]), onehot).astype(ka_ref.dtype)

    kmean = jnp.mean(k, axis=0, keepdims=True)
    row0 = pl.multiple_of(t * heads, heads)
    kmt_ref[pl.ds(row0, heads), :] = kmean * hm_ref[...]

    vt = x_ref[:, 2 * W:3 * W].astype(F32).T
    ones = jnp.ones((LANES - hd, vt.shape[1]), vt_ref.dtype)
    for h in range(heads):
        vt_ref[0, h, 0, 0:hd, :] = vt[h * hd:(h + 1) * hd].astype(vt_ref.dtype)
        vt_ref[0, h, 0, hd:LANES, :] = ones


def _moba_prep(x, qg, kg, batch, seq, heads, hd):
    W = heads * hd
    nt = seq // MOBA_BLOCK
    tabs, consts = _moba_constants(heads, hd, seq)
    full = lambda a: pl.BlockSpec(a.shape, lambda b, t: (0,) * a.ndim)
    tile_g = lambda g: jnp.tile(g.astype(F32), heads).reshape(1, W)
    small = (tile_g(qg), tile_g(kg))
    out_sd = jax.ShapeDtypeStruct((batch * seq, heads * LANES), BF16)
    out_spec = pl.BlockSpec((MOBA_BLOCK, heads * LANES), lambda b, t: (b * nt + t, 0))
    vt_sd = jax.ShapeDtypeStruct((batch, heads, nt, LANES, MOBA_BLOCK), BF16)
    vt_spec = pl.BlockSpec((1, heads, 1, LANES, MOBA_BLOCK), lambda b, t: (b, 0, t, 0, 0))
    return pl.pallas_call(
        functools.partial(_moba_prep_kernel, heads, hd),
        grid=(batch, nt),
        in_specs=[pl.BlockSpec((MOBA_BLOCK, 3 * W), lambda b, t: (b * nt + t, 0))]
        + [full(a) for a in small]
        + [pl.BlockSpec((MOBA_BLOCK, LANES), lambda b, t: (t, 0)) for _ in tabs]
        + [full(a) for a in consts],
        out_specs=[out_spec, out_spec, vt_spec],
        out_shape=[out_sd, out_sd, vt_sd],
        scratch_shapes=[pltpu.VMEM((LANES, W), F32)],
        compiler_params=_cparams("parallel", "arbitrary"),
        name="moba_prep",
    )(x, *small, *tabs, *consts)


def _moba_attn_kernel(hd, qa_ref, ka_ref, vt_ref, o_ref, acc_ref, sa_ref, sb_ref):
    t = pl.program_id(1)
    tq = qa_ref.shape[0]
    blk = MOBA_BLOCK
    nh = vt_ref.shape[1]

    def scores_to(dst_ref, n, hh):
        cols = slice(hh * LANES, (hh + 1) * LANES)
        rows = pl.ds(pl.multiple_of(n * blk, blk), blk)
        dst_ref[hh] = _dot_nt(ka_ref[rows, cols], qa_ref[:, cols])

    def update(s, m, n, hh):
        m_new = jnp.maximum(m, jnp.max(s, axis=0, keepdims=True))
        alpha = jnp.exp(m - m_new)
        p = jnp.exp(s - m_new).astype(BF16)
        acc_ref[hh] = alpha * acc_ref[hh] + _dot(vt_ref[0, hh, n], p)
        return m_new

    def step(cur_ref, nxt_ref, n, ms):
        new = []
        for hh in range(nh):
            scores_to(nxt_ref, n + 1, hh)
            new.append(update(cur_ref[hh], ms[hh], n, hh))
        return tuple(new)

    acc_ref[...] = jnp.zeros_like(acc_ref)
    for hh in range(nh):
        scores_to(sa_ref, 0, hh)

    def pair(i, ms):
        ms = step(sa_ref, sb_ref, 2 * i, ms)
        return step(sb_ref, sa_ref, 2 * i + 1, ms)

    m0 = jnp.full((1, tq), 2 * NEG, F32)
    ms = lax.fori_loop(0, t // 2, pair, (m0,) * nh)

    def odd_block(ms):
        new = []
        for hh in range(nh):
            new.append(update(sa_ref[hh], ms[hh], t - 1, hh))
            scores_to(sa_ref, t, hh)
        return tuple(new)

    ms = lax.cond(t % 2 == 1, odd_block, lambda ms: ms, ms)

    key = lax.broadcasted_iota(jnp.int32, (blk, tq), 0)
    qry = lax.broadcasted_iota(jnp.int32, (blk, tq), 1)
    outs = []
    for hh in range(nh):
        s = jnp.where(key <= qry, sa_ref[hh], NEG)
        update(s, ms[hh], t, hh)
        a = acc_ref[hh]
        outs.append(a[0:hd] / a[hd:hd + 1])
    o_ref[...] = jnp.concatenate(outs, axis=0).T.astype(o_ref.dtype)


def _moba_attn(qa, ka, vt, batch, seq, heads, hd):
    nt = seq // MOBA_BLOCK
    return pl.pallas_call(
        functools.partial(_moba_attn_kernel, hd),
        grid=(batch, nt),
        in_specs=[pl.BlockSpec((MOBA_BLOCK, heads * LANES), lambda b, t: (b * nt + t, 0)),
                  pl.BlockSpec((seq, heads * LANES), lambda b, t: (b, 0)),
                  pl.BlockSpec((1, heads, nt, LANES, MOBA_BLOCK), lambda b, t: (b, 0, 0, 0, 0))],
        out_specs=pl.BlockSpec((MOBA_BLOCK, heads * hd), lambda b, t: (b * nt + t, 0)),
        out_shape=jax.ShapeDtypeStruct((batch * seq, heads * hd), BF16),
        scratch_shapes=[pltpu.VMEM((heads, LANES, MOBA_BLOCK), F32),
                        pltpu.VMEM((heads, MOBA_BLOCK, MOBA_BLOCK), F32),
                        pltpu.VMEM((heads, MOBA_BLOCK, MOBA_BLOCK), F32)],
        compiler_params=_cparams("parallel", "arbitrary"),
        name="moba_attn",
    )(qa, ka, vt)


def _out_proj_kernel(widths, x_ref, oa_ref, ob_ref, oc_ref, w_ref, g_ref, wr_ref,
                     x1_ref, hx_ref):
    acc = x_ref[...]
    r0 = 0
    for o_ref, wd in zip((oa_ref, ob_ref, oc_ref), widths):
        acc = acc + _dot(o_ref[...], w_ref[r0:r0 + wd, :])
        r0 += wd
    x1_ref[...] = acc
    ms = jnp.mean(acc * acc, axis=-1, keepdims=True)
    h2 = acc * lax.rsqrt(ms + NORM_EPS) * g_ref[...]
    hw = h2.shape[1]
    hx_ref[:, 0:hw] = h2

    G, E = MOE_GROUPS, MOE_EXPERTS_PER_GROUP
    logits = _dot3(h2, wr_ref[...])
    lane = lax.broadcasted_iota(jnp.int32, logits.shape, 1).astype(F32)
    big = float(LANES)
    is_g = lane < G
    lg = jnp.where(is_g, logits, -jnp.inf)
    mg = jnp.max(lg, axis=-1, keepdims=True)
    gate_group = 1.0 / jnp.sum(jnp.exp(lg - mg), axis=-1, keepdims=True)
    g_sel = jnp.min(jnp.where(lg == mg, lane, big), axis=-1, keepdims=True)
    lo = G + E * g_sel
    in_grp = (lane >= lo) & (lane < lo + E)
    v = jnp.where(in_grp, logits, -jnp.inf)
    v1 = jnp.max(v, axis=-1, keepdims=True)
    i1 = jnp.min(jnp.where(v == v1, lane, big), axis=-1, keepdims=True)
    v = jnp.where(lane == i1, -jnp.inf, v)
    v2 = jnp.max(v, axis=-1, keepdims=True)
    i2 = jnp.min(jnp.where(v == v2, lane, big), axis=-1, keepdims=True)
    e2 = jnp.exp(v2 - v1)
    w1 = gate_group / (1.0 + e2)
    w2 = gate_group * e2 / (1.0 + e2)
    first = i1 < i2
    a = jnp.minimum(i1, i2) - lo
    b = jnp.maximum(i1, i2) - lo
    pair = a * (2 * E - 1 - a) * 0.5 + (b - a - 1.0)
    cls = g_sel * float(E * (E - 1) // 2) + pair
    route = jnp.where(lane == 0.0, cls, 0.0)
    route = jnp.where(lane == 1.0, jnp.where(first, w1, w2), route)
    route = jnp.where(lane == 2.0, jnp.where(first, w2, w1), route)
    hx_ref[:, hw:hw + LANES] = route


def _out_proj(x, oa, ob, oc, w, g, wr):
    n, d = x.shape
    tm = ROW_TILE
    widths = (oa.shape[1], ob.shape[1], oc.shape[1])
    row = lambda width: pl.BlockSpec((tm, width), lambda i: (i, 0))
    full = lambda a: pl.BlockSpec(a.shape, lambda i: (0, 0))
    g2 = g.reshape(1, d)
    hxw = d + LANES
    return pl.pallas_call(
        functools.partial(_out_proj_kernel, widths),
        grid=(n // tm,),
        in_specs=[row(d), row(widths[0]), row(widths[1]), row(widths[2]), full(w), full(g2), full(wr)],
        out_specs=[row(d), row(hxw)],
        out_shape=[jax.ShapeDtypeStruct((n, d), F32), jax.ShapeDtypeStruct((n, hxw), F32)],
        compiler_params=_cparams("parallel"),
        name="out_proj",
    )(x, oa, ob, oc, w, g2, wr)


def _moe_classes():
    G, E = MOE_GROUPS, MOE_EXPERTS_PER_GROUP
    lo, hi = [], []
    for g in range(G):
        for a in range(E):
            for b in range(a + 1, E):
                lo.append(g * E + a)
                hi.append(g * E + b)
    return lo, hi


def _moe_plan_kernel(elo, ehi, cls_ref, dest_ref, tab_ref):
    R = MOE_TILE
    cls = cls_ref[...]
    rows = cls.shape[0]
    ri = lax.broadcasted_iota(jnp.int32, (LANES, LANES), 0)
    ci = lax.broadcasted_iota(jnp.int32, (LANES, LANES), 1)
    before_lane = (ri < ci).astype(BF16)
    all_lane = jnp.ones((LANES, LANES), BF16)
    rr = lax.broadcasted_iota(jnp.int32, (rows, rows), 0)
    rc = lax.broadcasted_iota(jnp.int32, (rows, rows), 1)
    before_row = (rc < rr).astype(BF16)
    all_row = jnp.ones((rows, rows), BF16)
    tile_row0 = lax.broadcasted_iota(jnp.int32, (8, LANES), 1).astype(F32) * float(R)
    off = jnp.zeros((rows, LANES), F32)
    dest = jnp.zeros((rows, LANES), F32)
    tile_cls = jnp.zeros((8, LANES), F32)
    for k in range(len(elo)):
        mask = (cls == k).astype(BF16)
        row_tot = _dot(mask, all_lane)
        rank = _dot(mask, before_lane) + _dot(before_row, row_tot.astype(BF16))
        count = _dot(all_row, row_tot.astype(BF16))
        dest = dest + mask.astype(F32) * (off + rank)
        off = off + jnp.floor((count + float(R - 1)) * (1.0 / R)) * float(R)
        tile_cls = tile_cls + jnp.where(tile_row0 >= off[0:8], 1.0, 0.0)
    dest_ref[...] = dest.astype(jnp.int32)
    tile_cls = jnp.minimum(tile_cls, float(len(elo) - 1))
    t_lo = jnp.zeros((8, LANES), F32)
    t_hi = jnp.zeros((8, LANES), F32)
    for k in range(len(elo)):
        t_lo = jnp.where(tile_cls == k, float(elo[k]), t_lo)
        t_hi = jnp.where(tile_cls == k, float(ehi[k]), t_hi)
    sub = lax.broadcasted_iota(jnp.int32, (8, LANES), 0)
    tab = jnp.where(sub == 0, t_lo, jnp.where(sub == 1, t_hi, off[0:8] * (1.0 / R)))
    tab_ref[...] = tab.astype(jnp.int32)


def _moe_plan(cls):
    elo, ehi = _moe_classes()
    rows = cls.shape[0]
    return pl.pallas_call(
        functools.partial(_moe_plan_kernel, elo, ehi),
        out_shape=[jax.ShapeDtypeStruct((rows, LANES), jnp.int32),
                   jax.ShapeDtypeStruct((8, LANES), jnp.int32)],
        name="moe_plan",
    )(cls)


SUBLANES = 8


def _row_copies(n, make):
    def start(g, c):
        base = pl.multiple_of(g * SUBLANES, SUBLANES)
        for s in range(SUBLANES):
            make(base + s).start()
        return c

    def wait(g, c):
        base = pl.multiple_of(g * SUBLANES, SUBLANES)
        for s in range(SUBLANES):
            make(base + s).wait()
        return c

    lax.fori_loop(0, n // SUBLANES, start, 0)
    lax.fori_loop(0, n // SUBLANES, wait, 0)


def _moe_dispatch_kernel(dest_ref, hx_ref, xs_in_ref, xs_ref, sem):
    del xs_in_ref
    _row_copies(hx_ref.shape[0], lambda r: pltpu.make_async_copy(
        hx_ref.at[pl.ds(r, 1)], xs_ref.at[pl.ds(dest_ref[r], 1)], sem))


def _moe_dispatch(dest, hx, n_rows):
    n, w = hx.shape
    tm = ROW_TILE
    return pl.pallas_call(
        _moe_dispatch_kernel,
        grid=(n // tm,),
        in_specs=[pl.BlockSpec((tm,), lambda i: (i,), memory_space=pltpu.SMEM),
                  pl.BlockSpec((tm, w), lambda i: (i, 0)),
                  pl.BlockSpec(memory_space=pl.ANY)],
        out_specs=pl.BlockSpec(memory_space=pl.ANY),
        out_shape=jax.ShapeDtypeStruct((n_rows, w), hx.dtype),
        scratch_shapes=[pltpu.SemaphoreType.DMA],
        input_output_aliases={2: 0},
        compiler_params=_cparams("arbitrary"),
        name="moe_dispatch",
    )(dest, hx, jnp.zeros((n_rows, w), hx.dtype))


def _moe_experts_kernel(elo_ref, ehi_ref, nt_ref, xs_ref, wg0, wu0, wd0, wg1, wu1, wd1, y_ref):
    del elo_ref, ehi_ref
    j = pl.program_id(0)
    hw = xs_ref.shape[1] - LANES

    @pl.when(j < nt_ref[0])
    def _():
        x = xs_ref[:, 0:hw].astype(BF16)
        route = xs_ref[:, hw:hw + LANES]
        lane = lax.broadcasted_iota(jnp.int32, route.shape, 1)
        y = None
        for slot, (wg, wu, wd) in enumerate(((wg0, wu0, wd0), (wg1, wu1, wd1))):
            wt = jnp.sum(jnp.where(lane == 1 + slot, route, 0.0), axis=-1, keepdims=True)
            a = _dot(x, wg[0])
            u = _dot(x, wu[0])
            he = (a * _sigmoid(a)) * u * wt
            part = _dot(he.astype(BF16), wd[0])
            y = part if y is None else y + part
        y_ref[...] = y

    @pl.when(j >= nt_ref[0])
    def _():
        y_ref[...] = jnp.zeros_like(y_ref)


def _moe_experts(tab, xs, wg, wu, wd):
    n_rows, w = xs.shape
    ne, d, ff = wg.shape
    R = MOE_TILE
    last = lambda j, nt: jnp.minimum(j, nt[0] - 1)
    wspec = lambda shape, which: pl.BlockSpec(
        shape, lambda j, elo, ehi, nt: ((elo, ehi)[which][last(j, nt)], 0, 0))
    grid_spec = pltpu.PrefetchScalarGridSpec(
        num_scalar_prefetch=3,
        grid=(n_rows // R,),
        in_specs=[pl.BlockSpec((R, w), lambda j, elo, ehi, nt: (last(j, nt), 0)),
                  wspec((1, d, ff), 0), wspec((1, d, ff), 0), wspec((1, ff, d), 0),
                  wspec((1, d, ff), 1), wspec((1, d, ff), 1), wspec((1, ff, d), 1)],
        out_specs=pl.BlockSpec((R, d), lambda j, elo, ehi, nt: (j, 0)),
    )
    return pl.pallas_call(
        _moe_experts_kernel,
        grid_spec=grid_spec,
        out_shape=jax.ShapeDtypeStruct((n_rows, d), F32),
        compiler_params=_cparams("arbitrary"),
        name="moe_experts",
    )(tab[0], tab[1], tab[2, 0:1], xs, wg, wu, wd, wg, wu, wd)


def _moe_combine_kernel(dest_ref, x1_ref, y_ref, o_ref, buf_ref, sem):
    _row_copies(buf_ref.shape[0], lambda r: pltpu.make_async_copy(
        y_ref.at[pl.ds(dest_ref[r], 1)], buf_ref.at[pl.ds(r, 1)], sem))
    o_ref[...] = x1_ref[...] + buf_ref[...]


def _moe_combine(dest, x1, y):
    n, d = x1.shape
    tm = ROW_TILE
    return pl.pallas_call(
        _moe_combine_kernel,
        grid=(n // tm,),
        in_specs=[pl.BlockSpec((tm,), lambda i: (i,), memory_space=pltpu.SMEM),
                  pl.BlockSpec((tm, d), lambda i: (i, 0)),
                  pl.BlockSpec(memory_space=pl.ANY)],
        out_specs=pl.BlockSpec((tm, d), lambda i: (i, 0)),
        out_shape=jax.ShapeDtypeStruct((n, d), F32),
        scratch_shapes=[pltpu.VMEM((tm, d), F32), pltpu.SemaphoreType.DMA],
        compiler_params=_cparams("arbitrary"),
        name="moe_combine",
    )(dest, x1, y)


def _moe(x1, hx, wg, wu, wd):
    n, d = x1.shape
    elo, _ = _moe_classes()
    n_rows = n + len(elo) * MOE_TILE
    assert n % LANES == 0 and n_rows // MOE_TILE <= LANES
    cls = hx[:, d].astype(jnp.int32).reshape(n // LANES, LANES)
    dest, tab = _moe_plan(cls)
    dest = dest.reshape(n)
    xs = _moe_dispatch(dest, hx, n_rows)
    y = _moe_experts(tab, xs, wg, wu, wd)
    return _moe_combine(dest, x1, y)


def _pad_cols(w, width):
    return jnp.pad(w, ((0, 0), (0, width - w.shape[1])))


def kernel(x, attn_norm_g, w_in, gla_gk_w2, gla_gk_b, gla_norm_g, moba_qnorm_g, moba_knorm_g,
           hgrn_lb_param, hgrn_norm_g, w_out, ffn_norm_g, w_router_group, w_router_expert,
           w_exp_gate, w_exp_up, w_exp_down):
    batch, seq, d = x.shape
    depth = w_in.shape[0]
    lowrank, gla_kw = gla_gk_w2.shape[1:]
    gla_dv = gla_norm_g.shape[1]
    moba_hd = moba_qnorm_g.shape[1]
    hgrn_kw = hgrn_lb_param.shape[1]
    hgrn_dv = hgrn_norm_g.shape[1]
    mix_w = w_out.shape[1]
    hgrn_heads = 4
    gla_heads = 4
    gla_vw = gla_heads * gla_dv
    hgrn_vw = hgrn_heads * hgrn_dv
    moba_w = mix_w - gla_vw - hgrn_vw
    moba_heads = moba_w // moba_hd
    gla_dk = gla_kw // gla_heads
    hgrn_dk = hgrn_kw // hgrn_heads

    splits = (gla_kw, gla_kw, gla_vw, gla_vw, lowrank, moba_w, moba_w, moba_w,
              hgrn_kw, hgrn_kw, hgrn_vw, hgrn_vw)
    offs = np.concatenate([[0], np.cumsum(splits)]).tolist()
    col = lambda w, i: w[:, offs[i]:offs[i + 1]]

    xf = x.reshape(batch * seq, d)
    for l in range(depth):
        w = w_in[l]
        wcat = jnp.concatenate(
            [col(w, 0), col(w, 1), col(w, 2), col(w, 3), _pad_cols(col(w, 4), LANES),
             col(w, 5), col(w, 6), col(w, 7), col(w, 8), col(w, 10), col(w, 11), col(w, 9)],
            axis=1).astype(BF16)
        g_gla = 2 * gla_kw + 2 * gla_vw
        g_hg = hgrn_kw + 2 * hgrn_vw
        bounds = np.cumsum([0, g_gla, LANES, 3 * moba_w, g_hg, hgrn_kw]).tolist()
        groups = tuple((bounds[i], bounds[i + 1]) for i in range(5))
        y_gla, y_lr, y_moba, y_hg, y_f = _in_proj(
            xf, attn_norm_g[l], wcat, groups, (BF16, F32, BF16, BF16, F32))

        w2 = jnp.pad(gla_gk_w2[l], ((0, LANES - lowrank), (0, 0)))
        o_a = _lin_call(
            functools.partial(_gla_kernel, gla_heads, gla_dk, gla_dv), "gla",
            batch, seq, gla_heads, gla_dk, gla_dv, (y_gla, y_lr),
            (w2, gla_gk_b[l].reshape(1, gla_kw), jnp.tile(gla_norm_g[l], gla_heads).reshape(1, gla_vw)))
        o_c = _lin_call(
            functools.partial(_hgrn_kernel, hgrn_heads, hgrn_dk, hgrn_dv, l), "hgrn",
            batch, seq, hgrn_heads, hgrn_dk, hgrn_dv, (y_hg, y_f),
            (hgrn_lb_param, jnp.tile(hgrn_norm_g[l], hgrn_heads).reshape(1, hgrn_vw)))

        qa, ka, vt = _moba_prep(y_moba, moba_qnorm_g[l], moba_knorm_g[l], batch, seq, moba_heads, moba_hd)
        o_b = _moba_attn(qa, ka, vt, batch, seq, moba_heads, moba_hd)

        wr = _pad_cols(jnp.concatenate([w_router_group[l], w_router_expert[l]], axis=1), LANES)
        x1, hx = _out_proj(xf, o_a, o_b, o_c, w_out[l].astype(BF16), ffn_norm_g[l], wr)
        xf = _moe(x1, hx, w_exp_gate[l].astype(BF16), w_exp_up[l].astype(BF16),
                  w_exp_down[l].astype(BF16))
    return xf.reshape(batch, seq, d)
```

```python
import functools

import numpy as np
import jax
import jax.numpy as jnp
from jax import lax
from jax.experimental import pallas as pl
from jax.experimental.pallas import tpu as pltpu

F32 = jnp.float32
BF16 = jnp.bfloat16

NORM_EPS = 1e-6
GLA_GATE_TAU = 16.0
ROPE_THETA = 500000.0
MOBA_BLOCK = 256
MOBA_TOPK = 3
MOE_GROUPS = 4
MOE_EXPERTS_PER_GROUP = 4
MOE_TOPK = 2

LANES = 128
VMEM_LIMIT = 56 * 1024 * 1024
LIN_CHUNK = 64
LIN_TILE = 512
LIN_GROUP = 8
ROW_TILE = 512
MOE_TILE = 256
NEG = -1e30
LOG2E = 1.4426950408889634


def _cparams(*sem):
    return pltpu.CompilerParams(dimension_semantics=sem, vmem_limit_bytes=VMEM_LIMIT)


def _sigmoid(x):
    return 1.0 / (1.0 + jnp.exp(-x))


def _dot(a, b):
    return jnp.dot(a, b, preferred_element_type=F32)


def _dot_nt(a, b):
    return lax.dot_general(a, b, (((1,), (1,)), ((), ())), preferred_element_type=F32)


def _dot_tn(a, b):
    return lax.dot_general(a, b, (((0,), (0,)), ((), ())), preferred_element_type=F32)


def _hi_lo(x):
    hi = x.astype(BF16)
    return hi, (x - hi.astype(F32)).astype(BF16)


def _dot3(a, b, nt=False):
    a_hi, a_lo = _hi_lo(a)
    b_hi, b_lo = _hi_lo(b)
    lhs = jnp.concatenate([a_hi, a_hi, a_lo], axis=1)
    if nt:
        return _dot_nt(lhs, jnp.concatenate([b_hi, b_lo, b_hi], axis=1))
    return _dot(lhs, jnp.concatenate([b_hi, b_lo, b_hi], axis=0))


def _in_proj_kernel(pieces, groups, x_ref, g_ref, w_ref, *refs):
    out_refs, wcat_ref = refs[:-1], refs[-1]

    @pl.when(pl.program_id(0) == 0)
    def _():
        for src, width, dst, padded in pieces:
            blk = w_ref[:, src:src + width].astype(BF16)
            if padded > width:
                blk = jnp.concatenate([blk, jnp.zeros((blk.shape[0], padded - width), BF16)], axis=1)
            wcat_ref[:, dst:dst + padded] = blk

    x = x_ref[...]
    ms = jnp.mean(x * x, axis=-1, keepdims=True)
    h = (x * lax.rsqrt(ms + NORM_EPS) * g_ref[...]).astype(BF16)
    for (a, b), o_ref in zip(groups, out_refs):
        o_ref[...] = _dot(h, wcat_ref[:, a:b]).astype(o_ref.dtype)


def _in_proj(x, g, w_all, layer, pieces, groups, dtypes):
    n, d = x.shape
    tm = ROW_TILE
    cols = w_all.shape[2]
    out_shape = [jax.ShapeDtypeStruct((n, b - a), dt) for (a, b), dt in zip(groups, dtypes)]
    out_specs = [pl.BlockSpec((tm, b - a), lambda i: (i, 0)) for (a, b) in groups]
    return pl.pallas_call(
        functools.partial(_in_proj_kernel, pieces, groups),
        grid=(n // tm,),
        in_specs=[pl.BlockSpec((tm, d), lambda i: (i, 0)),
                  pl.BlockSpec((1, d), lambda i: (0, 0)),
                  pl.BlockSpec((None, d, cols), lambda i: (layer, 0, 0), pipeline_mode=pl.Buffered(1))],
        out_specs=out_specs,
        out_shape=out_shape,
        scratch_shapes=[pltpu.VMEM((d, groups[-1][1]), BF16)],
        compiler_params=_cparams("arbitrary"),
        name="in_proj",
    )(x, g.reshape(1, d), w_all)


def _lin_constants(heads, dk, dv):
    L = LIN_CHUNK
    nlev = int(np.log2(L))
    K, V = heads * dk, heads * dv
    i = np.arange(L)[:, None]
    t = np.arange(L)[None, :]
    w_rows = [(t <= i), (t > i)]
    masks = []
    for lev in range(nlev):
        b = L >> lev
        half = b // 2
        r = (i // b) * b + half - 1
        w_rows.append((t > np.minimum(i, r)) & (t <= np.maximum(i, r)))
        j = t
        masks.append((i // b == j // b) & (i % b >= half) & (j % b < half))
    masks.append(i == t)
    w_all = np.concatenate(w_rows, axis=0).astype(np.float32)
    m_all = np.stack([np.tile(m, (1, heads)) for m in masks]).astype(np.float32)
    rh = np.repeat(np.arange(heads), L)[:, None]
    bdk = (rh == np.repeat(np.arange(heads), dk)[None, :]).astype(np.float32)
    bdv = (rh == np.repeat(np.arange(heads), dv)[None, :]).astype(np.float32)
    bds = (np.repeat(np.arange(heads), dv)[:, None]
           == np.repeat(np.arange(heads), dk)[None, :]).astype(np.float32)
    ones_v = (np.repeat(np.arange(heads), dv)[:, None]
              == np.repeat(np.arange(heads), dv)[None, :]).astype(np.float32)
    return (jnp.asarray(w_all, BF16), jnp.asarray(m_all, F32), jnp.asarray(bdk, BF16),
            jnp.asarray(bdv, BF16), jnp.asarray(bds, F32), jnp.asarray(ones_v, BF16))


def _lin_group(q, k, v, la, g, gain, st_ref, w_ref, m_ref, bdk_ref, bdv_ref, bds_ref, ones_ref,
               heads, dv):
    L = LIN_CHUNK
    G = q.shape[0] // L
    nlev = m_ref.shape[0] - 1
    K = q.shape[1]
    chunks = [slice(c * L, (c + 1) * L) for c in range(G)]
    lacat = jnp.concatenate(_hi_lo(la), axis=1)
    v_bf = v.astype(BF16)
    es = []
    for c in chunks:
        z2 = _dot(w_ref[...], lacat[c])
        es.append(jnp.exp(z2[:, :K] + z2[:, K:]))
    bdk = bdk_ref[...]
    scores = [None] * G
    for lev in range(nlev + 1):
        for i, c in enumerate(chunks):
            if lev < nlev:
                el = es[i][(2 + lev) * L:(3 + lev) * L]
                ql = (q[c] * el).astype(BF16)
                kl = (k[c] * el).astype(BF16)
            else:
                ql = q[c].astype(BF16)
                kl = k[c].astype(BF16)
            kbd = jnp.concatenate([kl] * heads, axis=0) * bdk
            s = _dot_nt(ql, kbd) * m_ref[lev]
            scores[i] = s if scores[i] is None else scores[i] + s
    bdv = bdv_ref[...]
    intra, kvs = [], []
    for i, c in enumerate(chunks):
        vbd = jnp.concatenate([v_bf[c]] * heads, axis=0) * bdv
        intra.append(_dot(scores[i].astype(BF16), vbd))
        kb = (k[c] * es[i][L:2 * L]).astype(BF16)
        kvs.append(_dot_tn(v_bf[c], kb) * bds_ref[...])
    st = st_ref[...]
    outs = []
    for i, c in enumerate(chunks):
        outs.append(intra[i] + _dot_nt((q[c] * es[i][0:L]).astype(BF16), st.astype(BF16)))
        st = st * es[i][L - 1:L] + kvs[i]
    st_ref[...] = st
    o = jnp.concatenate(outs, axis=0)
    ms = _dot((o * o).astype(BF16), ones_ref[...]) * (1.0 / dv)
    gf = g.astype(F32)
    return o * lax.rsqrt(ms + NORM_EPS) * gain * (gf * _sigmoid(gf))


def _gla_kernel(heads, dk, dv, x_ref, lr_ref, w2_ref, b_ref, gain_ref,
                w_ref, m_ref, bdk_ref, bdv_ref, bds_ref, ones_ref, o_ref, st_ref):
    K, V = heads * dk, heads * dv
    L = LIN_CHUNK

    @pl.when(pl.program_id(1) == 0)
    def _():
        st_ref[...] = jnp.zeros_like(st_ref)

    gain = gain_ref[...]
    w2 = w2_ref[...]
    bias = b_ref[...]

    R = LIN_GROUP * L

    def body(c, carry):
        rows = pl.ds(pl.multiple_of(c * R, R), R)
        q = x_ref[rows, 0:K].astype(F32) * (dk ** -0.5)
        k = x_ref[rows, K:2 * K].astype(F32)
        v = x_ref[rows, 2 * K:2 * K + V]
        g = x_ref[rows, 2 * K + V:2 * K + 2 * V]
        gk = _dot3(lr_ref[rows, :], w2) + bias
        la = (jnp.minimum(gk, 0.0) - jnp.log1p(jnp.exp(-jnp.abs(gk)))) * (1.0 / GLA_GATE_TAU)
        o_ref[rows, :] = _lin_group(q, k, v, la, g, gain, st_ref, w_ref, m_ref, bdk_ref, bdv_ref,
                                    bds_ref, ones_ref, heads, dv).astype(o_ref.dtype)
        return carry

    lax.fori_loop(0, x_ref.shape[0] // R, body, 0)


def _hgrn_kernel(heads, dk, dv, layer, x_ref, f_ref, lbp_ref, gain_ref,
                 w_ref, m_ref, bdk_ref, bdv_ref, bds_ref, ones_ref, o_ref, st_ref):
    K, V = heads * dk, heads * dv
    L = LIN_CHUNK

    @pl.when(pl.program_id(1) == 0)
    def _():
        st_ref[...] = jnp.zeros_like(st_ref)

    lbp = lbp_ref[...]
    depth = lbp.shape[0]
    mx = lbp[0:1]
    for r in range(1, depth):
        mx = jnp.maximum(mx, lbp[r:r + 1])
    ex = [jnp.exp(lbp[r:r + 1] - mx) for r in range(depth)]
    den = ex[0]
    for r in range(1, depth):
        den = den + ex[r]
    lb = jnp.zeros_like(den)
    for r in range(1, layer + 1):
        lb = lb + ex[r] / den
    gain = gain_ref[...]

    R = LIN_GROUP * L

    def body(c, carry):
        rows = pl.ds(pl.multiple_of(c * R, R), R)
        cq = x_ref[rows, 0:K].astype(F32)
        q = cq * _sigmoid(cq)
        v = x_ref[rows, K:K + V]
        g = x_ref[rows, K + V:K + 2 * V]
        f = f_ref[rows, :]
        la = jnp.log(lb + (1.0 - lb) * _sigmoid(f))
        k = (1.0 - lb) * _sigmoid(-f)
        o_ref[rows, :] = _lin_group(q, k, v, la, g, gain, st_ref, w_ref, m_ref, bdk_ref, bdv_ref,
                                    bds_ref, ones_ref, heads, dv).astype(o_ref.dtype)
        return carry

    lax.fori_loop(0, x_ref.shape[0] // R, body, 0)


def _lin_call(kernel_fn, name, batch, seq, heads, dk, dv, row_inputs, small_inputs):
    T = LIN_TILE
    nt = seq // T
    consts = _lin_constants(heads, dk, dv)
    row_specs = [pl.BlockSpec((T, a.shape[1]), lambda b, t: (b * nt + t, 0)) for a in row_inputs]
    full = lambda a: pl.BlockSpec(a.shape, lambda b, t: (0,) * a.ndim)
    return pl.pallas_call(
        kernel_fn,
        grid=(batch, nt),
        in_specs=row_specs + [full(a) for a in small_inputs] + [full(a) for a in consts],
        out_specs=pl.BlockSpec((T, heads * dv), lambda b, t: (b * nt + t, 0)),
        out_shape=jax.ShapeDtypeStruct((batch * seq, heads * dv), BF16),
        scratch_shapes=[pltpu.VMEM((heads * dv, heads * dk), F32)],
        compiler_params=_cparams("parallel", "arbitrary"),
        name=name,
    )(*row_inputs, *small_inputs, *consts)


def _moba_constants(heads, hd, seq):
    W = heads * hd
    nb = LANES // heads
    rot = hd // 4
    half = rot // 2
    inv_freq = ROPE_THETA ** (-(np.arange(0, rot, 2, dtype=np.float64) / rot))
    ang = np.arange(seq, dtype=np.float64)[:, None] * inv_freq[None, :]
    cos, sin = np.cos(ang), np.sin(ang)
    c = np.ones((seq, hd), np.float32)
    sp = np.zeros((seq, hd), np.float32)
    sm = np.zeros((seq, hd), np.float32)
    c[:, :half] = cos
    c[:, half:rot] = cos
    sm[:, :half] = -sin
    sp[:, half:rot] = sin
    reps = LANES // hd
    tabs = [jnp.asarray(np.tile(a, (1, reps)), F32) for a in (c, sp, sm)]
    hl = np.repeat(np.arange(heads), hd)
    ones_h = (hl[:, None] == hl[None, :]).astype(np.float32)
    hm = (np.arange(heads)[:, None] == hl[None, :]).astype(np.float32)
    pb = np.zeros((LANES, heads * LANES), np.float32)
    gl = np.arange(LANES)
    pb[gl, (gl % heads) * LANES + hd + gl // heads] = 1.0
    return tabs, (jnp.asarray(ones_h, BF16), jnp.asarray(hm, F32), jnp.asarray(pb, BF16))


def _moba_prep_kernel(heads, hd, x_ref, qg_ref, kg_ref, c_ref, sp_ref, sm_ref,
                      ones_ref, hm_ref, pexp_ref, qa_ref, ka_ref, vt_ref, kmt_ref):
    W = heads * hd
    t = pl.program_id(1)
    half = hd // 8
    reps = W // LANES

    @pl.when(t == 0)
    def _():
        kmt_ref[...] = jnp.zeros_like(kmt_ref)

    wide = lambda ref: jnp.concatenate([ref[...]] * reps, axis=1)
    c, sp, sm = wide(c_ref), wide(sp_ref), wide(sm_ref)

    def norm_rot(x, gain):
        ms = _dot((x * x).astype(BF16), ones_ref[...]) * (1.0 / hd)
        y = x * lax.rsqrt(ms + NORM_EPS) * gain
        return y * c + pltpu.roll(y, half, 1) * sp + pltpu.roll(y, W - half, 1) * sm

    q = norm_rot(x_ref[:, 0:W].astype(F32), qg_ref[...])
    k = norm_rot(x_ref[:, W:2 * W].astype(F32), kg_ref[...])

    gate = _dot3(q, kmt_ref[...], nt=True)
    lane = lax.broadcasted_iota(jnp.int32, gate.shape, 1)
    blk = lax.shift_right_logical(lane, int(np.log2(heads)))
    valid = blk < t
    gate = jnp.where(valid, gate, -jnp.inf)
    rank = jnp.zeros(gate.shape, F32)
    nslots = LANES // heads
    for s in range(1, nslots):
        other = pltpu.roll(gate, s * heads, 1)
        ahead = (other > gate) | ((other == gate) & (blk >= s))
        rank = rank + ahead.astype(F32)
    sel = valid & (rank < float(MOBA_TOPK))
    bias = jnp.where(sel | (blk == t), 0.0, NEG)

    bias_slab = _dot(bias.astype(BF16), pexp_ref[...])
    qs = q * (hd ** -0.5 * LOG2E)
    lane1 = lax.broadcasted_iota(jnp.int32, (q.shape[0], LANES), 1)
    onehot = jnp.where(lane1 == hd + t, 1.0, 0.0)
    per_vreg = LANES // hd
    for h in range(heads):
        src = slice((h // per_vreg) * LANES, (h // per_vreg + 1) * LANES)
        dst = slice(h * LANES, (h + 1) * LANES)
        shift = (LANES - (h % per_vreg) * hd) % LANES
        move = (lambda a: a) if shift == 0 else (lambda a: pltpu.roll(a, shift, 1))
        qa_ref[:, dst] = jnp.where(lane1 < hd, move(qs[:, src]), bias_slab[:, dst]).astype(qa_ref.dtype)
        ka_ref[:, dst] = jnp.where(lane1 < hd, move(k[:, src]), onehot).astype(ka_ref.dtype)

    kmean = jnp.mean(k, axis=0, keepdims=True)
    row0 = pl.multiple_of(t * heads, heads)
    kmt_ref[pl.ds(row0, heads), :] = kmean * hm_ref[...]

    vt = x_ref[:, 2 * W:3 * W].astype(F32).T
    ones = jnp.ones((LANES - hd, vt.shape[1]), vt_ref.dtype)
    for h in range(heads):
        vt_ref[0, h, 0, 0:hd, :] = vt[h * hd:(h + 1) * hd].astype(vt_ref.dtype)
        vt_ref[0, h, 0, hd:LANES, :] = ones


def _moba_prep(x, qg, kg, batch, seq, heads, hd):
    W = heads * hd
    nt = seq // MOBA_BLOCK
    tabs, consts = _moba_constants(heads, hd, seq)
    full = lambda a: pl.BlockSpec(a.shape, lambda b, t: (0,) * a.ndim)
    tile_g = lambda g: jnp.tile(g.astype(F32), heads).reshape(1, W)
    small = (tile_g(qg), tile_g(kg))
    out_sd = jax.ShapeDtypeStruct((batch * seq, heads * LANES), BF16)
    out_spec = pl.BlockSpec((MOBA_BLOCK, heads * LANES), lambda b, t: (b * nt + t, 0))
    vt_sd = jax.ShapeDtypeStruct((batch, heads, nt, LANES, MOBA_BLOCK), BF16)
    vt_spec = pl.BlockSpec((1, heads, 1, LANES, MOBA_BLOCK), lambda b, t: (b, 0, t, 0, 0))
    return pl.pallas_call(
        functools.partial(_moba_prep_kernel, heads, hd),
        grid=(batch, nt),
        in_specs=[pl.BlockSpec((MOBA_BLOCK, 3 * W), lambda b, t: (b * nt + t, 0))]
        + [full(a) for a in small]
        + [pl.BlockSpec((MOBA_BLOCK, LANES), lambda b, t: (t, 0)) for _ in tabs]
        + [full(a) for a in consts],
        out_specs=[out_spec, out_spec, vt_spec],
        out_shape=[out_sd, out_sd, vt_sd],
        scratch_shapes=[pltpu.VMEM((LANES, W), F32)],
        compiler_params=_cparams("parallel", "arbitrary"),
        name="moba_prep",
    )(x, *small, *tabs, *consts)


def _moba_attn_kernel(hd, qa_ref, ka_ref, vt_ref, o_ref, acc_ref, sa_ref, sb_ref):
    t = pl.program_id(1)
    tq = qa_ref.shape[0]
    blk = MOBA_BLOCK
    nh = vt_ref.shape[1]

    def scores_to(dst_ref, n, hh):
        cols = slice(hh * LANES, (hh + 1) * LANES)
        rows = pl.ds(pl.multiple_of(n * blk, blk), blk)
        dst_ref[hh] = _dot_nt(ka_ref[rows, cols], qa_ref[:, cols])

    def update(s, m, n, hh):
        m_new = jnp.maximum(m, jnp.max(s, axis=0, keepdims=True))
        alpha = jnp.exp2(m - m_new)
        p = jnp.exp2(s - m_new).astype(BF16)
        acc_ref[hh] = alpha * acc_ref[hh] + _dot(vt_ref[0, hh, n], p)
        return m_new

    def step(cur_ref, nxt_ref, n, ms):
        new = []
        for hh in range(nh):
            scores_to(nxt_ref, n + 1, hh)
            new.append(update(cur_ref[hh], ms[hh], n, hh))
        return tuple(new)

    acc_ref[...] = jnp.zeros_like(acc_ref)
    for hh in range(nh):
        scores_to(sa_ref, 0, hh)

    def pair(i, ms):
        ms = step(sa_ref, sb_ref, 2 * i, ms)
        return step(sb_ref, sa_ref, 2 * i + 1, ms)

    m0 = jnp.full((1, tq), 2 * NEG, F32)
    ms = lax.fori_loop(0, t // 2, pair, (m0,) * nh)

    def odd_block(ms):
        new = []
        for hh in range(nh):
            new.append(update(sa_ref[hh], ms[hh], t - 1, hh))
            scores_to(sa_ref, t, hh)
        return tuple(new)

    ms = lax.cond(t % 2 == 1, odd_block, lambda ms: ms, ms)

    key = lax.broadcasted_iota(jnp.int32, (blk, tq), 0)
    qry = lax.broadcasted_iota(jnp.int32, (blk, tq), 1)
    outs = []
    for hh in range(nh):
        s = jnp.where(key <= qry, sa_ref[hh], NEG)
        update(s, ms[hh], t, hh)
        a = acc_ref[hh]
        outs.append(a[0:hd] / a[hd:hd + 1])
    o_ref[...] = jnp.concatenate(outs, axis=0).T.astype(o_ref.dtype)


def _moba_attn(qa, ka, vt, batch, seq, heads, hd):
    nt = seq // MOBA_BLOCK
    return pl.pallas_call(
        functools.partial(_moba_attn_kernel, hd),
        grid=(batch, nt),
        in_specs=[pl.BlockSpec((MOBA_BLOCK, heads * LANES), lambda b, t: (b * nt + t, 0)),
                  pl.BlockSpec((seq, heads * LANES), lambda b, t: (b, 0)),
                  pl.BlockSpec((1, heads, nt, LANES, MOBA_BLOCK), lambda b, t: (b, 0, 0, 0, 0))],
        out_specs=pl.BlockSpec((MOBA_BLOCK, heads * hd), lambda b, t: (b * nt + t, 0)),
        out_shape=jax.ShapeDtypeStruct((batch * seq, heads * hd), BF16),
        scratch_shapes=[pltpu.VMEM((heads, LANES, MOBA_BLOCK), F32),
                        pltpu.VMEM((heads, MOBA_BLOCK, MOBA_BLOCK), F32),
                        pltpu.VMEM((heads, MOBA_BLOCK, MOBA_BLOCK), F32)],
        compiler_params=_cparams("parallel", "arbitrary"),
        name="moba_attn",
    )(qa, ka, vt)


def _out_proj_kernel(widths, x_ref, oa_ref, ob_ref, oc_ref, w_ref, g_ref, wr_ref,
                     x1_ref, hx_ref):
    acc = x_ref[...]
    r0 = 0
    for o_ref, wd in zip((oa_ref, ob_ref, oc_ref), widths):
        acc = acc + _dot(o_ref[...], w_ref[r0:r0 + wd, :])
        r0 += wd
    x1_ref[...] = acc
    ms = jnp.mean(acc * acc, axis=-1, keepdims=True)
    h2 = acc * lax.rsqrt(ms + NORM_EPS) * g_ref[...]
    hw = h2.shape[1]
    hx_ref[:, 0:hw] = h2

    G, E = MOE_GROUPS, MOE_EXPERTS_PER_GROUP
    logits = _dot3(h2, wr_ref[...])
    lane = lax.broadcasted_iota(jnp.int32, logits.shape, 1).astype(F32)
    big = float(LANES)
    is_g = lane < G
    lg = jnp.where(is_g, logits, -jnp.inf)
    mg = jnp.max(lg, axis=-1, keepdims=True)
    gate_group = 1.0 / jnp.sum(jnp.exp(lg - mg), axis=-1, keepdims=True)
    g_sel = jnp.min(jnp.where(lg == mg, lane, big), axis=-1, keepdims=True)
    lo = G + E * g_sel
    in_grp = (lane >= lo) & (lane < lo + E)
    v = jnp.where(in_grp, logits, -jnp.inf)
    v1 = jnp.max(v, axis=-1, keepdims=True)
    i1 = jnp.min(jnp.where(v == v1, lane, big), axis=-1, keepdims=True)
    v = jnp.where(lane == i1, -jnp.inf, v)
    v2 = jnp.max(v, axis=-1, keepdims=True)
    i2 = jnp.min(jnp.where(v == v2, lane, big), axis=-1, keepdims=True)
    e2 = jnp.exp(v2 - v1)
    w1 = gate_group / (1.0 + e2)
    w2 = gate_group * e2 / (1.0 + e2)
    first = i1 < i2
    a = jnp.minimum(i1, i2) - lo
    b = jnp.maximum(i1, i2) - lo
    pair = a * (2 * E - 1 - a) * 0.5 + (b - a - 1.0)
    cls = g_sel * float(E * (E - 1) // 2) + pair
    route = jnp.where(lane == 0.0, cls, 0.0)
    route = jnp.where(lane == 1.0, jnp.where(first, w1, w2), route)
    route = jnp.where(lane == 2.0, jnp.where(first, w2, w1), route)
    hx_ref[:, hw:hw + LANES] = route


def _out_proj(x, oa, ob, oc, w, g, wr):
    n, d = x.shape
    tm = ROW_TILE
    widths = (oa.shape[1], ob.shape[1], oc.shape[1])
    row = lambda width: pl.BlockSpec((tm, width), lambda i: (i, 0))
    full = lambda a: pl.BlockSpec(a.shape, lambda i: (0, 0))
    g2 = g.reshape(1, d)
    hxw = d + LANES
    return pl.pallas_call(
        functools.partial(_out_proj_kernel, widths),
        grid=(n // tm,),
        in_specs=[row(d), row(widths[0]), row(widths[1]), row(widths[2]), full(w), full(g2), full(wr)],
        out_specs=[row(d), row(hxw)],
        out_shape=[jax.ShapeDtypeStruct((n, d), F32), jax.ShapeDtypeStruct((n, hxw), F32)],
        compiler_params=_cparams("parallel"),
        name="out_proj",
    )(x, oa, ob, oc, w, g2, wr)


def _moe_classes():
    G, E = MOE_GROUPS, MOE_EXPERTS_PER_GROUP
    lo, hi = [], []
    for g in range(G):
        for a in range(E):
            for b in range(a + 1, E):
                lo.append(g * E + a)
                hi.append(g * E + b)
    return lo, hi


def _moe_plan_kernel(elo, ehi, cls_ref, dest_ref, tab_ref):
    R = MOE_TILE
    cls = cls_ref[...]
    rows = cls.shape[0]
    ri = lax.broadcasted_iota(jnp.int32, (LANES, LANES), 0)
    ci = lax.broadcasted_iota(jnp.int32, (LANES, LANES), 1)
    before_lane = (ri < ci).astype(BF16)
    all_lane = jnp.ones((LANES, LANES), BF16)
    rr = lax.broadcasted_iota(jnp.int32, (rows, rows), 0)
    rc = lax.broadcasted_iota(jnp.int32, (rows, rows), 1)
    before_row = (rc < rr).astype(BF16)
    all_row = jnp.ones((rows, rows), BF16)
    tile_row0 = lax.broadcasted_iota(jnp.int32, (8, LANES), 1).astype(F32) * float(R)
    off = jnp.zeros((rows, LANES), F32)
    dest = jnp.zeros((rows, LANES), F32)
    tile_cls = jnp.zeros((8, LANES), F32)
    for k in range(len(elo)):
        mask = (cls == k).astype(BF16)
        row_tot = _dot(mask, all_lane)
        rank = _dot(mask, before_lane) + _dot(before_row, row_tot.astype(BF16))
        count = _dot(all_row, row_tot.astype(BF16))
        dest = dest + mask.astype(F32) * (off + rank)
        off = off + jnp.floor((count + float(R - 1)) * (1.0 / R)) * float(R)
        tile_cls = tile_cls + jnp.where(tile_row0 >= off[0:8], 1.0, 0.0)
    dest_ref[...] = dest.astype(jnp.int32)
    tile_cls = jnp.minimum(tile_cls, float(len(elo) - 1))
    t_lo = jnp.zeros((8, LANES), F32)
    t_hi = jnp.zeros((8, LANES), F32)
    for k in range(len(elo)):
        t_lo = jnp.where(tile_cls == k, float(elo[k]), t_lo)
        t_hi = jnp.where(tile_cls == k, float(ehi[k]), t_hi)
    sub = lax.broadcasted_iota(jnp.int32, (8, LANES), 0)
    tab = jnp.where(sub == 0, t_lo, jnp.where(sub == 1, t_hi, off[0:8] * (1.0 / R)))
    tab_ref[...] = tab.astype(jnp.int32)


def _moe_plan(cls):
    elo, ehi = _moe_classes()
    rows = cls.shape[0]
    return pl.pallas_call(
        functools.partial(_moe_plan_kernel, elo, ehi),
        out_shape=[jax.ShapeDtypeStruct((rows, LANES), jnp.int32),
                   jax.ShapeDtypeStruct((8, LANES), jnp.int32)],
        name="moe_plan",
    )(cls)


SUBLANES = 8


def _row_copies(n, make):
    def start(g, c):
        base = pl.multiple_of(g * SUBLANES, SUBLANES)
        for s in range(SUBLANES):
            make(base + s).start()
        return c

    def wait(g, c):
        base = pl.multiple_of(g * SUBLANES, SUBLANES)
        for s in range(SUBLANES):
            make(base + s).wait()
        return c

    lax.fori_loop(0, n // SUBLANES, start, 0)
    lax.fori_loop(0, n // SUBLANES, wait, 0)


def _moe_dispatch_kernel(dest_ref, hx_ref, xs_in_ref, xs_ref, sem):
    del xs_in_ref
    _row_copies(hx_ref.shape[0], lambda r: pltpu.make_async_copy(
        hx_ref.at[pl.ds(r, 1)], xs_ref.at[pl.ds(dest_ref[r], 1)], sem))


def _moe_dispatch(dest, hx, n_rows):
    n, w = hx.shape
    tm = ROW_TILE
    return pl.pallas_call(
        _moe_dispatch_kernel,
        grid=(n // tm,),
        in_specs=[pl.BlockSpec((tm,), lambda i: (i,), memory_space=pltpu.SMEM),
                  pl.BlockSpec((tm, w), lambda i: (i, 0)),
                  pl.BlockSpec(memory_space=pl.ANY)],
        out_specs=pl.BlockSpec(memory_space=pl.ANY),
        out_shape=jax.ShapeDtypeStruct((n_rows, w), hx.dtype),
        scratch_shapes=[pltpu.SemaphoreType.DMA],
        input_output_aliases={2: 0},
        compiler_params=_cparams("arbitrary"),
        name="moe_dispatch",
    )(dest, hx, jnp.zeros((n_rows, w), hx.dtype))


def _moe_experts_kernel(elo_ref, ehi_ref, nt_ref, xs_ref, wg0, wu0, wd0, wg1, wu1, wd1, y_ref,
                        wgu_ref, wdn_ref):
    j = pl.program_id(0)
    hw = xs_ref.shape[1] - LANES
    jp = jnp.maximum(j - 1, 0)
    fresh = (j == 0) | (elo_ref[j] != elo_ref[jp]) | (ehi_ref[j] != ehi_ref[jp])

    @pl.when((j < nt_ref[0]) & fresh)
    def _():
        for slot, (wg, wu, wd) in enumerate(((wg0, wu0, wd0), (wg1, wu1, wd1))):
            wgu_ref[slot, 0] = wg[0].astype(BF16)
            wgu_ref[slot, 1] = wu[0].astype(BF16)
            wdn_ref[slot] = wd[0].astype(BF16)

    @pl.when(j < nt_ref[0])
    def _():
        x = xs_ref[:, 0:hw].astype(BF16)
        route = xs_ref[:, hw:hw + LANES]
        lane = lax.broadcasted_iota(jnp.int32, route.shape, 1)
        y = None
        for slot in range(2):
            wt = jnp.sum(jnp.where(lane == 1 + slot, route, 0.0), axis=-1, keepdims=True)
            a = _dot(x, wgu_ref[slot, 0])
            u = _dot(x, wgu_ref[slot, 1])
            he = (a * _sigmoid(a)) * u * wt
            part = _dot(he.astype(BF16), wdn_ref[slot])
            y = part if y is None else y + part
        y_ref[...] = y

    @pl.when(j >= nt_ref[0])
    def _():
        y_ref[...] = jnp.zeros_like(y_ref)


def _moe_experts(tab, xs, wg, wu, wd, layer):
    n_rows, w = xs.shape
    _, ne, d, ff = wg.shape
    R = MOE_TILE
    last = lambda j, nt: jnp.minimum(j, nt[0] - 1)
    wspec = lambda shape, which: pl.BlockSpec(
        (None,) + shape, lambda j, elo, ehi, nt: (layer, (elo, ehi)[which][last(j, nt)], 0, 0))
    grid_spec = pltpu.PrefetchScalarGridSpec(
        num_scalar_prefetch=3,
        grid=(n_rows // R,),
        in_specs=[pl.BlockSpec((R, w), lambda j, elo, ehi, nt: (last(j, nt), 0)),
                  wspec((1, d, ff), 0), wspec((1, d, ff), 0), wspec((1, ff, d), 0),
                  wspec((1, d, ff), 1), wspec((1, d, ff), 1), wspec((1, ff, d), 1)],
        out_specs=pl.BlockSpec((R, d), lambda j, elo, ehi, nt: (j, 0)),
        scratch_shapes=[pltpu.VMEM((2, 2, d, ff), BF16), pltpu.VMEM((2, ff, d), BF16)],
    )
    return pl.pallas_call(
        _moe_experts_kernel,
        grid_spec=grid_spec,
        out_shape=jax.ShapeDtypeStruct((n_rows, d), F32),
        compiler_params=_cparams("arbitrary"),
        name="moe_experts",
    )(tab[0], tab[1], tab[2, 0:1], xs, wg, wu, wd, wg, wu, wd)


def _moe_combine_kernel(dest_ref, x1_ref, y_ref, o_ref, buf_ref, sem):
    _row_copies(buf_ref.shape[0], lambda r: pltpu.make_async_copy(
        y_ref.at[pl.ds(dest_ref[r], 1)], buf_ref.at[pl.ds(r, 1)], sem))
    o_ref[...] = x1_ref[...] + buf_ref[...]


def _moe_combine(dest, x1, y):
    n, d = x1.shape
    tm = ROW_TILE
    return pl.pallas_call(
        _moe_combine_kernel,
        grid=(n // tm,),
        in_specs=[pl.BlockSpec((tm,), lambda i: (i,), memory_space=pltpu.SMEM),
                  pl.BlockSpec((tm, d), lambda i: (i, 0)),
                  pl.BlockSpec(memory_space=pl.ANY)],
        out_specs=pl.BlockSpec((tm, d), lambda i: (i, 0)),
        out_shape=jax.ShapeDtypeStruct((n, d), F32),
        scratch_shapes=[pltpu.VMEM((tm, d), F32), pltpu.SemaphoreType.DMA],
        compiler_params=_cparams("arbitrary"),
        name="moe_combine",
    )(dest, x1, y)


def _moe(x1, hx, wg, wu, wd, layer):
    n, d = x1.shape
    elo, _ = _moe_classes()
    n_rows = n + len(elo) * MOE_TILE
    assert n % LANES == 0 and n_rows // MOE_TILE <= LANES
    cls = hx[:, d].astype(jnp.int32).reshape(n // LANES, LANES)
    dest, tab = _moe_plan(cls)
    dest = dest.reshape(n)
    xs = _moe_dispatch(dest, hx, n_rows)
    y = _moe_experts(tab, xs, wg, wu, wd, layer)
    return _moe_combine(dest, x1, y)


def _pad_cols(w, width):
    return jnp.pad(w, ((0, 0), (0, width - w.shape[1])))


def kernel(x, attn_norm_g, w_in, gla_gk_w2, gla_gk_b, gla_norm_g, moba_qnorm_g, moba_knorm_g,
           hgrn_lb_param, hgrn_norm_g, w_out, ffn_norm_g, w_router_group, w_router_expert,
           w_exp_gate, w_exp_up, w_exp_down):
    batch, seq, d = x.shape
    depth = w_in.shape[0]
    lowrank, gla_kw = gla_gk_w2.shape[1:]
    gla_dv = gla_norm_g.shape[1]
    moba_hd = moba_qnorm_g.shape[1]
    hgrn_kw = hgrn_lb_param.shape[1]
    hgrn_dv = hgrn_norm_g.shape[1]
    mix_w = w_out.shape[1]
    hgrn_heads = 4
    gla_heads = 4
    gla_vw = gla_heads * gla_dv
    hgrn_vw = hgrn_heads * hgrn_dv
    moba_w = mix_w - gla_vw - hgrn_vw
    moba_heads = moba_w // moba_hd
    gla_dk = gla_kw // gla_heads
    hgrn_dk = hgrn_kw // hgrn_heads

    splits = (gla_kw, gla_kw, gla_vw, gla_vw, lowrank, moba_w, moba_w, moba_w,
              hgrn_kw, hgrn_kw, hgrn_vw, hgrn_vw)
    offs = np.concatenate([[0], np.cumsum(splits)]).tolist()
    order = (0, 1, 2, 3, 4, 5, 6, 7, 8, 10, 11, 9)
    pieces, dst = [], 0
    for i in order:
        padded = -(-splits[i] // LANES) * LANES
        pieces.append((offs[i], splits[i], dst, padded))
        dst += padded
    g_gla = 2 * gla_kw + 2 * gla_vw
    g_hg = hgrn_kw + 2 * hgrn_vw
    bounds = np.cumsum([0, g_gla, LANES, 3 * moba_w, g_hg, hgrn_kw]).tolist()
    groups = tuple((bounds[i], bounds[i + 1]) for i in range(5))
    assert dst == bounds[-1]

    xf = x.reshape(batch * seq, d)
    for l in range(depth):
        y_gla, y_lr, y_moba, y_hg, y_f = _in_proj(
            xf, attn_norm_g[l], w_in, l, tuple(pieces), groups, (BF16, F32, BF16, BF16, F32))

        w2 = jnp.pad(gla_gk_w2[l], ((0, LANES - lowrank), (0, 0)))
        o_a = _lin_call(
            functools.partial(_gla_kernel, gla_heads, gla_dk, gla_dv), "gla",
            batch, seq, gla_heads, gla_dk, gla_dv, (y_gla, y_lr),
            (w2, gla_gk_b[l].reshape(1, gla_kw), jnp.tile(gla_norm_g[l], gla_heads).reshape(1, gla_vw)))
        o_c = _lin_call(
            functools.partial(_hgrn_kernel, hgrn_heads, hgrn_dk, hgrn_dv, l), "hgrn",
            batch, seq, hgrn_heads, hgrn_dk, hgrn_dv, (y_hg, y_f),
            (hgrn_lb_param, jnp.tile(hgrn_norm_g[l], hgrn_heads).reshape(1, hgrn_vw)))

        qa, ka, vt = _moba_prep(y_moba, moba_qnorm_g[l], moba_knorm_g[l], batch, seq, moba_heads, moba_hd)
        o_b = _moba_attn(qa, ka, vt, batch, seq, moba_heads, moba_hd)

        wr = _pad_cols(jnp.concatenate([w_router_group[l], w_router_expert[l]], axis=1), LANES)
        x1, hx = _out_proj(xf, o_a, o_b, o_c, w_out[l].astype(BF16), ffn_norm_g[l], wr)
        xf = _moe(x1, hx, w_exp_gate, w_exp_up, w_exp_down, l)
    return xf.reshape(batch, seq, d)
```

```python
import functools

import numpy as np
import jax
import jax.numpy as jnp
from jax import lax
from jax.experimental import pallas as pl
from jax.experimental.pallas import tpu as pltpu

F32 = jnp.float32
BF16 = jnp.bfloat16

NORM_EPS = 1e-6
GLA_GATE_TAU = 16.0
ROPE_THETA = 500000.0
MOBA_BLOCK = 256
MOBA_TOPK = 3
MOE_GROUPS = 4
MOE_EXPERTS_PER_GROUP = 4
MOE_TOPK = 2

LANES = 128
VMEM_LIMIT = 56 * 1024 * 1024
LIN_CHUNK = 64
LIN_TILE = 512
LIN_GROUP = 8
ROW_TILE = 512
MOE_TILE = 256
NEG = -1e30
LOG2E = 1.4426950408889634


def _cparams(*sem):
    return pltpu.CompilerParams(dimension_semantics=sem, vmem_limit_bytes=VMEM_LIMIT)


def _sigmoid(x):
    return 1.0 / (1.0 + jnp.exp(-x))


def _dot(a, b):
    return jnp.dot(a, b, preferred_element_type=F32)


def _dot_nt(a, b):
    return lax.dot_general(a, b, (((1,), (1,)), ((), ())), preferred_element_type=F32)


def _dot_tn(a, b):
    return lax.dot_general(a, b, (((0,), (0,)), ((), ())), preferred_element_type=F32)


def _hi_lo(x):
    hi = x.astype(BF16)
    return hi, (x - hi.astype(F32)).astype(BF16)


def _dot3(a, b, nt=False):
    a_hi, a_lo = _hi_lo(a)
    b_hi, b_lo = _hi_lo(b)
    lhs = jnp.concatenate([a_hi, a_hi, a_lo], axis=1)
    if nt:
        return _dot_nt(lhs, jnp.concatenate([b_hi, b_lo, b_hi], axis=1))
    return _dot(lhs, jnp.concatenate([b_hi, b_lo, b_hi], axis=0))


def _in_proj_kernel(pieces, groups, moba_cols, heads, hd, tiles_per_seq,
                    x_ref, g_ref, w_ref, qg_ref, kg_ref, c_ref, sp_ref, sm_ref,
                    ones_ref, hm_ref, pexp_ref, *refs):
    out_refs = refs[:len(groups)]
    qa_ref, ka_ref, vt_ref, wcat_ref, kmt_ref = refs[len(groups):]
    i = pl.program_id(0)

    @pl.when(i == 0)
    def _():
        for src, width, dst, padded in pieces:
            blk = w_ref[:, src:src + width].astype(BF16)
            if padded > width:
                blk = jnp.concatenate([blk, jnp.zeros((blk.shape[0], padded - width), BF16)], axis=1)
            wcat_ref[:, dst:dst + padded] = blk

    tile = lax.rem(i, tiles_per_seq)

    @pl.when(tile == 0)
    def _():
        kmt_ref[...] = jnp.zeros_like(kmt_ref)

    x = x_ref[...]
    ms = jnp.mean(x * x, axis=-1, keepdims=True)
    h = (x * lax.rsqrt(ms + NORM_EPS) * g_ref[...]).astype(BF16)
    ym = _dot(h, wcat_ref[:, moba_cols[0]:moba_cols[1]])
    for (a, b), o_ref in zip(groups, out_refs):
        o_ref[...] = _dot(h, wcat_ref[:, a:b]).astype(o_ref.dtype)

    W = heads * hd
    blocks = x.shape[0] // MOBA_BLOCK
    for blk in range(blocks):
        rows = slice(blk * MOBA_BLOCK, (blk + 1) * MOBA_BLOCK)

        def write_vt(hh, vrows, value, blk=blk):
            vt_ref[0, hh, blk, vrows, :] = value.astype(vt_ref.dtype)

        _moba_prep_block(
            heads, hd, tile * blocks + blk, ym[rows, 0:W], ym[rows, W:2 * W], ym[rows, 2 * W:3 * W],
            (c_ref[rows, :], sp_ref[rows, :], sm_ref[rows, :]), qg_ref[...], kg_ref[...],
            ones_ref, hm_ref, pexp_ref, kmt_ref, qa_ref.at[rows], ka_ref.at[rows], write_vt)


def _in_proj(x, g, w_all, layer, pieces, groups, dtypes, moba_cols, qg, kg, batch, seq, heads, hd):
    n, d = x.shape
    tm = ROW_TILE
    cols = w_all.shape[2]
    W = heads * hd
    nt = seq // MOBA_BLOCK
    blocks = tm // MOBA_BLOCK
    tiles_per_seq = seq // tm
    tabs, consts = _moba_constants(heads, hd, seq)
    tile_g = lambda v: jnp.tile(v.astype(F32), heads).reshape(1, W)
    row = lambda width: pl.BlockSpec((tm, width), lambda i: (i, 0))
    full = lambda a: pl.BlockSpec(a.shape, lambda i: (0,) * a.ndim)
    wide_sd = jax.ShapeDtypeStruct((n, heads * LANES), BF16)
    out_shape = [jax.ShapeDtypeStruct((n, b - a), dt) for (a, b), dt in zip(groups, dtypes)]
    out_shape += [wide_sd, wide_sd, jax.ShapeDtypeStruct((batch, heads, nt, LANES, MOBA_BLOCK), BF16)]
    out_specs = [row(b - a) for (a, b) in groups] + [row(heads * LANES), row(heads * LANES)]
    out_specs.append(pl.BlockSpec((1, heads, blocks, LANES, MOBA_BLOCK),
                                  lambda i: (i // tiles_per_seq, 0, i % tiles_per_seq, 0, 0)))
    return pl.pallas_call(
        functools.partial(_in_proj_kernel, pieces, groups, moba_cols, heads, hd, tiles_per_seq),
        grid=(n // tm,),
        in_specs=[row(d),
                  pl.BlockSpec((1, d), lambda i: (0, 0)),
                  pl.BlockSpec((None, d, cols), lambda i: (layer, 0, 0), pipeline_mode=pl.Buffered(1)),
                  full(tile_g(qg)), full(tile_g(kg))]
        + [pl.BlockSpec((tm, LANES), lambda i: (i % tiles_per_seq, 0)) for _ in tabs]
        + [full(a) for a in consts],
        out_specs=out_specs,
        out_shape=out_shape,
        scratch_shapes=[pltpu.VMEM((d, pieces[-1][2] + pieces[-1][3]), BF16),
                        pltpu.VMEM((LANES, W), F32)],
        compiler_params=_cparams("arbitrary"),
        name="in_proj",
    )(x, g.reshape(1, d), w_all, tile_g(qg), tile_g(kg), *tabs, *consts)


def _lin_constants(heads, dk, dv):
    L = LIN_CHUNK
    nlev = int(np.log2(L))
    K, V = heads * dk, heads * dv
    i = np.arange(L)[:, None]
    t = np.arange(L)[None, :]
    w_rows = [(t <= i), (t > i)]
    masks = []
    for lev in range(nlev):
        b = L >> lev
        half = b // 2
        r = (i // b) * b + half - 1
        w_rows.append((t > np.minimum(i, r)) & (t <= np.maximum(i, r)))
        j = t
        masks.append((i // b == j // b) & (i % b >= half) & (j % b < half))
    masks.append(i == t)
    w_all = np.concatenate(w_rows, axis=0).astype(np.float32)
    m_all = np.stack([np.tile(m, (1, heads)) for m in masks]).astype(np.float32)
    rh = np.repeat(np.arange(heads), L)[:, None]
    bdk = (rh == np.repeat(np.arange(heads), dk)[None, :]).astype(np.float32)
    bdv = (rh == np.repeat(np.arange(heads), dv)[None, :]).astype(np.float32)
    bds = (np.repeat(np.arange(heads), dv)[:, None]
           == np.repeat(np.arange(heads), dk)[None, :]).astype(np.float32)
    ones_v = (np.repeat(np.arange(heads), dv)[:, None]
              == np.repeat(np.arange(heads), dv)[None, :]).astype(np.float32)
    return (jnp.asarray(w_all, BF16), jnp.asarray(m_all, F32), jnp.asarray(bdk, BF16),
            jnp.asarray(bdv, BF16), jnp.asarray(bds, F32), jnp.asarray(ones_v, BF16))


def _lin_group(q, k, v, la, g, gain, st_ref, w_ref, m_ref, bdk_ref, bdv_ref, bds_ref, ones_ref,
               heads, dv):
    L = LIN_CHUNK
    G = q.shape[0] // L
    nlev = m_ref.shape[0] - 1
    K = q.shape[1]
    chunks = [slice(c * L, (c + 1) * L) for c in range(G)]
    lacat = jnp.concatenate(_hi_lo(la), axis=1)
    v_bf = v.astype(BF16)
    es = []
    for c in chunks:
        z2 = _dot(w_ref[...], lacat[c])
        es.append(jnp.exp(z2[:, :K] + z2[:, K:]))
    bdk = bdk_ref[...]
    scores = [None] * G
    for lev in range(nlev + 1):
        for i, c in enumerate(chunks):
            if lev < nlev:
                el = es[i][(2 + lev) * L:(3 + lev) * L]
                ql = (q[c] * el).astype(BF16)
                kl = (k[c] * el).astype(BF16)
            else:
                ql = q[c].astype(BF16)
                kl = k[c].astype(BF16)
            kbd = jnp.concatenate([kl] * heads, axis=0) * bdk
            s = _dot_nt(ql, kbd) * m_ref[lev]
            scores[i] = s if scores[i] is None else scores[i] + s
    bdv = bdv_ref[...]
    intra, kvs = [], []
    for i, c in enumerate(chunks):
        vbd = jnp.concatenate([v_bf[c]] * heads, axis=0) * bdv
        intra.append(_dot(scores[i].astype(BF16), vbd))
        kb = (k[c] * es[i][L:2 * L]).astype(BF16)
        kvs.append(_dot_tn(v_bf[c], kb) * bds_ref[...])
    st = st_ref[...]
    outs = []
    for i, c in enumerate(chunks):
        outs.append(intra[i] + _dot_nt((q[c] * es[i][0:L]).astype(BF16), st.astype(BF16)))
        st = st * es[i][L - 1:L] + kvs[i]
    st_ref[...] = st
    o = jnp.concatenate(outs, axis=0)
    ms = _dot((o * o).astype(BF16), ones_ref[...]) * (1.0 / dv)
    gf = g.astype(F32)
    return o * lax.rsqrt(ms + NORM_EPS) * gain * (gf * _sigmoid(gf))


def _gla_kernel(heads, dk, dv, x_ref, lr_ref, w2_ref, b_ref, gain_ref,
                w_ref, m_ref, bdk_ref, bdv_ref, bds_ref, ones_ref, o_ref, st_ref):
    K, V = heads * dk, heads * dv
    L = LIN_CHUNK

    @pl.when(pl.program_id(1) == 0)
    def _():
        st_ref[...] = jnp.zeros_like(st_ref)

    gain = gain_ref[...]
    w2 = w2_ref[...]
    bias = b_ref[...]

    R = LIN_GROUP * L

    def body(c, carry):
        rows = pl.ds(pl.multiple_of(c * R, R), R)
        q = x_ref[rows, 0:K].astype(F32) * (dk ** -0.5)
        k = x_ref[rows, K:2 * K].astype(F32)
        v = x_ref[rows, 2 * K:2 * K + V]
        g = x_ref[rows, 2 * K + V:2 * K + 2 * V]
        gk = _dot3(lr_ref[rows, :], w2) + bias
        la = (jnp.minimum(gk, 0.0) - jnp.log1p(jnp.exp(-jnp.abs(gk)))) * (1.0 / GLA_GATE_TAU)
        o_ref[rows, :] = _lin_group(q, k, v, la, g, gain, st_ref, w_ref, m_ref, bdk_ref, bdv_ref,
                                    bds_ref, ones_ref, heads, dv).astype(o_ref.dtype)
        return carry

    lax.fori_loop(0, x_ref.shape[0] // R, body, 0)


def _hgrn_kernel(heads, dk, dv, layer, x_ref, f_ref, lbp_ref, gain_ref,
                 w_ref, m_ref, bdk_ref, bdv_ref, bds_ref, ones_ref, o_ref, st_ref):
    K, V = heads * dk, heads * dv
    L = LIN_CHUNK

    @pl.when(pl.program_id(1) == 0)
    def _():
        st_ref[...] = jnp.zeros_like(st_ref)

    lbp = lbp_ref[...]
    depth = lbp.shape[0]
    mx = lbp[0:1]
    for r in range(1, depth):
        mx = jnp.maximum(mx, lbp[r:r + 1])
    ex = [jnp.exp(lbp[r:r + 1] - mx) for r in range(depth)]
    den = ex[0]
    for r in range(1, depth):
        den = den + ex[r]
    lb = jnp.zeros_like(den)
    for r in range(1, layer + 1):
        lb = lb + ex[r] / den
    gain = gain_ref[...]

    R = LIN_GROUP * L

    def body(c, carry):
        rows = pl.ds(pl.multiple_of(c * R, R), R)
        cq = x_ref[rows, 0:K].astype(F32)
        q = cq * _sigmoid(cq)
        v = x_ref[rows, K:K + V]
        g = x_ref[rows, K + V:K + 2 * V]
        f = f_ref[rows, :]
        la = jnp.log(lb + (1.0 - lb) * _sigmoid(f))
        k = (1.0 - lb) * _sigmoid(-f)
        o_ref[rows, :] = _lin_group(q, k, v, la, g, gain, st_ref, w_ref, m_ref, bdk_ref, bdv_ref,
                                    bds_ref, ones_ref, heads, dv).astype(o_ref.dtype)
        return carry

    lax.fori_loop(0, x_ref.shape[0] // R, body, 0)


def _lin_call(kernel_fn, name, batch, seq, heads, dk, dv, row_inputs, small_inputs):
    T = LIN_TILE
    nt = seq // T
    consts = _lin_constants(heads, dk, dv)
    row_specs = [pl.BlockSpec((T, a.shape[1]), lambda b, t: (b * nt + t, 0)) for a in row_inputs]
    full = lambda a: pl.BlockSpec(a.shape, lambda b, t: (0,) * a.ndim)
    return pl.pallas_call(
        kernel_fn,
        grid=(batch, nt),
        in_specs=row_specs + [full(a) for a in small_inputs] + [full(a) for a in consts],
        out_specs=pl.BlockSpec((T, heads * dv), lambda b, t: (b * nt + t, 0)),
        out_shape=jax.ShapeDtypeStruct((batch * seq, heads * dv), BF16),
        scratch_shapes=[pltpu.VMEM((heads * dv, heads * dk), F32)],
        compiler_params=_cparams("parallel", "arbitrary"),
        name=name,
    )(*row_inputs, *small_inputs, *consts)


def _moba_constants(heads, hd, seq):
    W = heads * hd
    nb = LANES // heads
    rot = hd // 4
    half = rot // 2
    inv_freq = ROPE_THETA ** (-(np.arange(0, rot, 2, dtype=np.float64) / rot))
    ang = np.arange(seq, dtype=np.float64)[:, None] * inv_freq[None, :]
    cos, sin = np.cos(ang), np.sin(ang)
    c = np.ones((seq, hd), np.float32)
    sp = np.zeros((seq, hd), np.float32)
    sm = np.zeros((seq, hd), np.float32)
    c[:, :half] = cos
    c[:, half:rot] = cos
    sm[:, :half] = -sin
    sp[:, half:rot] = sin
    reps = LANES // hd
    tabs = [jnp.asarray(np.tile(a, (1, reps)), F32) for a in (c, sp, sm)]
    hl = np.repeat(np.arange(heads), hd)
    ones_h = (hl[:, None] == hl[None, :]).astype(np.float32)
    hm = (np.arange(heads)[:, None] == hl[None, :]).astype(np.float32)
    pb = np.zeros((LANES, heads * LANES), np.float32)
    gl = np.arange(LANES)
    pb[gl, (gl % heads) * LANES + hd + gl // heads] = 1.0
    return tabs, (jnp.asarray(ones_h, BF16), jnp.asarray(hm, F32), jnp.asarray(pb, BF16))


def _moba_prep_block(heads, hd, t, xq, xk, xv, tabs, qg, kg, ones_ref, hm_ref, pexp_ref, kmt_ref,
                     qa_ref, ka_ref, write_vt):
    W = heads * hd
    half = hd // 8
    reps = W // LANES
    c, sp, sm = (jnp.concatenate([a] * reps, axis=1) for a in tabs)

    def norm_rot(x, gain):
        ms = _dot((x * x).astype(BF16), ones_ref[...]) * (1.0 / hd)
        y = x * lax.rsqrt(ms + NORM_EPS) * gain
        return y * c + pltpu.roll(y, half, 1) * sp + pltpu.roll(y, W - half, 1) * sm

    q = norm_rot(xq, qg)
    k = norm_rot(xk, kg)

    gate = _dot3(q, kmt_ref[...], nt=True)
    lane = lax.broadcasted_iota(jnp.int32, gate.shape, 1)
    blk = lax.shift_right_logical(lane, int(np.log2(heads)))
    valid = blk < t
    gate = jnp.where(valid, gate, -jnp.inf)
    rank = jnp.zeros(gate.shape, F32)
    nslots = LANES // heads
    for s in range(1, nslots):
        other = pltpu.roll(gate, s * heads, 1)
        ahead = (other > gate) | ((other == gate) & (blk >= s))
        rank = rank + ahead.astype(F32)
    sel = valid & (rank < float(MOBA_TOPK))
    bias = jnp.where(sel | (blk == t), 0.0, NEG)

    bias_slab = _dot(bias.astype(BF16), pexp_ref[...])
    qs = q * (hd ** -0.5 * LOG2E)
    lane1 = lax.broadcasted_iota(jnp.int32, (q.shape[0], LANES), 1)
    onehot = jnp.where(lane1 == hd + t, 1.0, 0.0)
    per_vreg = LANES // hd
    for h in range(heads):
        src = slice((h // per_vreg) * LANES, (h // per_vreg + 1) * LANES)
        dst = slice(h * LANES, (h + 1) * LANES)
        shift = (LANES - (h % per_vreg) * hd) % LANES
        move = (lambda a: a) if shift == 0 else (lambda a: pltpu.roll(a, shift, 1))
        qa_ref[:, dst] = jnp.where(lane1 < hd, move(qs[:, src]), bias_slab[:, dst]).astype(qa_ref.dtype)
        ka_ref[:, dst] = jnp.where(lane1 < hd, move(k[:, src]), onehot).astype(ka_ref.dtype)

    kmean = jnp.mean(k, axis=0, keepdims=True)
    row0 = pl.multiple_of(t * heads, heads)
    kmt_ref[pl.ds(row0, heads), :] = kmean * hm_ref[...]

    vt = xv.T
    ones = jnp.ones((LANES - hd, vt.shape[1]), F32)
    for h in range(heads):
        write_vt(h, slice(0, hd), vt[h * hd:(h + 1) * hd])
        write_vt(h, slice(hd, LANES), ones)


def _moba_attn_kernel(hd, qa_ref, ka_ref, vt_ref, o_ref, acc_ref, sa_ref, sb_ref):
    t = pl.program_id(1)
    tq = qa_ref.shape[0]
    blk = MOBA_BLOCK
    nh = vt_ref.shape[1]

    def scores_to(dst_ref, n, hh):
        cols = slice(hh * LANES, (hh + 1) * LANES)
        rows = pl.ds(pl.multiple_of(n * blk, blk), blk)
        dst_ref[hh] = _dot_nt(ka_ref[rows, cols], qa_ref[:, cols])

    def update(s, m, n, hh):
        m_new = jnp.maximum(m, jnp.max(s, axis=0, keepdims=True))
        alpha = jnp.exp2(m - m_new)
        p = jnp.exp2(s - m_new).astype(BF16)
        acc_ref[hh] = alpha * acc_ref[hh] + _dot(vt_ref[0, hh, n], p)
        return m_new

    def step(cur_ref, nxt_ref, n, ms):
        new = []
        for hh in range(nh):
            scores_to(nxt_ref, n + 1, hh)
            new.append(update(cur_ref[hh], ms[hh], n, hh))
        return tuple(new)

    acc_ref[...] = jnp.zeros_like(acc_ref)
    for hh in range(nh):
        scores_to(sa_ref, 0, hh)

    def pair(i, ms):
        ms = step(sa_ref, sb_ref, 2 * i, ms)
        return step(sb_ref, sa_ref, 2 * i + 1, ms)

    m0 = jnp.full((1, tq), 2 * NEG, F32)
    ms = lax.fori_loop(0, t // 2, pair, (m0,) * nh)

    def odd_block(ms):
        new = []
        for hh in range(nh):
            new.append(update(sa_ref[hh], ms[hh], t - 1, hh))
            scores_to(sa_ref, t, hh)
        return tuple(new)

    ms = lax.cond(t % 2 == 1, odd_block, lambda ms: ms, ms)

    key = lax.broadcasted_iota(jnp.int32, (blk, tq), 0)
    qry = lax.broadcasted_iota(jnp.int32, (blk, tq), 1)
    outs = []
    for hh in range(nh):
        s = jnp.where(key <= qry, sa_ref[hh], NEG)
        update(s, ms[hh], t, hh)
        a = acc_ref[hh]
        outs.append(a[0:hd] / a[hd:hd + 1])
    o_ref[...] = jnp.concatenate(outs, axis=0).T.astype(o_ref.dtype)


def _moba_attn(qa, ka, vt, batch, seq, heads, hd):
    nt = seq // MOBA_BLOCK
    return pl.pallas_call(
        functools.partial(_moba_attn_kernel, hd),
        grid=(batch, nt),
        in_specs=[pl.BlockSpec((MOBA_BLOCK, heads * LANES), lambda b, t: (b * nt + t, 0)),
                  pl.BlockSpec((seq, heads * LANES), lambda b, t: (b, 0)),
                  pl.BlockSpec((1, heads, nt, LANES, MOBA_BLOCK), lambda b, t: (b, 0, 0, 0, 0))],
        out_specs=pl.BlockSpec((MOBA_BLOCK, heads * hd), lambda b, t: (b * nt + t, 0)),
        out_shape=jax.ShapeDtypeStruct((batch * seq, heads * hd), BF16),
        scratch_shapes=[pltpu.VMEM((heads, LANES, MOBA_BLOCK), F32),
                        pltpu.VMEM((heads, MOBA_BLOCK, MOBA_BLOCK), F32),
                        pltpu.VMEM((heads, MOBA_BLOCK, MOBA_BLOCK), F32)],
        compiler_params=_cparams("parallel", "arbitrary"),
        name="moba_attn",
    )(qa, ka, vt)


def _out_proj_kernel(widths, x_ref, oa_ref, ob_ref, oc_ref, w_ref, g_ref, wr_ref,
                     x1_ref, hx_ref):
    acc = x_ref[...]
    r0 = 0
    for o_ref, wd in zip((oa_ref, ob_ref, oc_ref), widths):
        acc = acc + _dot(o_ref[...], w_ref[r0:r0 + wd, :])
        r0 += wd
    x1_ref[...] = acc
    ms = jnp.mean(acc * acc, axis=-1, keepdims=True)
    h2 = acc * lax.rsqrt(ms + NORM_EPS) * g_ref[...]
    hw = h2.shape[1]
    hx_ref[:, 0:hw] = h2

    G, E = MOE_GROUPS, MOE_EXPERTS_PER_GROUP
    logits = _dot3(h2, wr_ref[...])
    lane = lax.broadcasted_iota(jnp.int32, logits.shape, 1).astype(F32)
    big = float(LANES)
    is_g = lane < G
    lg = jnp.where(is_g, logits, -jnp.inf)
    mg = jnp.max(lg, axis=-1, keepdims=True)
    gate_group = 1.0 / jnp.sum(jnp.exp(lg - mg), axis=-1, keepdims=True)
    g_sel = jnp.min(jnp.where(lg == mg, lane, big), axis=-1, keepdims=True)
    lo = G + E * g_sel
    in_grp = (lane >= lo) & (lane < lo + E)
    v = jnp.where(in_grp, logits, -jnp.inf)
    v1 = jnp.max(v, axis=-1, keepdims=True)
    i1 = jnp.min(jnp.where(v == v1, lane, big), axis=-1, keepdims=True)
    v = jnp.where(lane == i1, -jnp.inf, v)
    v2 = jnp.max(v, axis=-1, keepdims=True)
    i2 = jnp.min(jnp.where(v == v2, lane, big), axis=-1, keepdims=True)
    e2 = jnp.exp(v2 - v1)
    w1 = gate_group / (1.0 + e2)
    w2 = gate_group * e2 / (1.0 + e2)
    first = i1 < i2
    a = jnp.minimum(i1, i2) - lo
    b = jnp.maximum(i1, i2) - lo
    w_lo = jnp.where(first, w1, w2)
    w_hi = jnp.where(first, w2, w1)
    pos = jnp.zeros_like(a)
    swap = a < 0.0
    for p, (ea, eb) in enumerate(_moe_pair_order()):
        hit = (a == float(min(ea, eb))) & (b == float(max(ea, eb)))
        pos = jnp.where(hit, float(p), pos)
        if ea > eb:
            swap = swap | hit
    cls = g_sel * float(E * (E - 1) // 2) + pos
    route = jnp.where(lane == 0.0, cls, 0.0)
    route = jnp.where(lane == 1.0, jnp.where(swap, w_hi, w_lo), route)
    route = jnp.where(lane == 2.0, jnp.where(swap, w_lo, w_hi), route)
    hx_ref[:, hw:hw + LANES] = route


def _out_proj(x, oa, ob, oc, w, g, wr):
    n, d = x.shape
    tm = ROW_TILE
    widths = (oa.shape[1], ob.shape[1], oc.shape[1])
    row = lambda width: pl.BlockSpec((tm, width), lambda i: (i, 0))
    full = lambda a: pl.BlockSpec(a.shape, lambda i: (0, 0))
    g2 = g.reshape(1, d)
    hxw = d + LANES
    return pl.pallas_call(
        functools.partial(_out_proj_kernel, widths),
        grid=(n // tm,),
        in_specs=[row(d), row(widths[0]), row(widths[1]), row(widths[2]), full(w), full(g2), full(wr)],
        out_specs=[row(d), row(hxw)],
        out_shape=[jax.ShapeDtypeStruct((n, d), F32), jax.ShapeDtypeStruct((n, hxw), F32)],
        compiler_params=_cparams("parallel"),
        name="out_proj",
    )(x, oa, ob, oc, w, g2, wr)


@functools.lru_cache(maxsize=None)
def _moe_pair_order():
    E = MOE_EXPERTS_PER_GROUP
    todo = {(a, b) for a in range(E) for b in range(a + 1, E)}

    def extend(seq, todo):
        if not todo:
            return seq
        for pr in sorted(todo):
            for cand in (pr, pr[::-1]):
                if not seq or cand[0] == seq[-1][0] or cand[1] == seq[-1][1]:
                    out = extend(seq + [cand], todo - {pr})
                    if out:
                        return out
        return None

    return tuple(extend([], todo))


def _moe_classes():
    G, E = MOE_GROUPS, MOE_EXPERTS_PER_GROUP
    ea = [g * E + a for g in range(G) for a, _ in _moe_pair_order()]
    eb = [g * E + b for g in range(G) for _, b in _moe_pair_order()]
    return ea, eb


def _moe_plan_kernel(elo, ehi, cls_ref, dest_ref, tab_ref):
    R = MOE_TILE
    cls = cls_ref[...]
    rows = cls.shape[0]
    ri = lax.broadcasted_iota(jnp.int32, (LANES, LANES), 0)
    ci = lax.broadcasted_iota(jnp.int32, (LANES, LANES), 1)
    before_lane = (ri < ci).astype(BF16)
    all_lane = jnp.ones((LANES, LANES), BF16)
    rr = lax.broadcasted_iota(jnp.int32, (rows, rows), 0)
    rc = lax.broadcasted_iota(jnp.int32, (rows, rows), 1)
    before_row = (rc < rr).astype(BF16)
    all_row = jnp.ones((rows, rows), BF16)
    tile_row0 = lax.broadcasted_iota(jnp.int32, (8, LANES), 1).astype(F32) * float(R)
    off = jnp.zeros((rows, LANES), F32)
    dest = jnp.zeros((rows, LANES), F32)
    tile_cls = jnp.zeros((8, LANES), F32)
    for k in range(len(elo)):
        mask = (cls == k).astype(BF16)
        row_tot = _dot(mask, all_lane)
        rank = _dot(mask, before_lane) + _dot(before_row, row_tot.astype(BF16))
        count = _dot(all_row, row_tot.astype(BF16))
        dest = dest + mask.astype(F32) * (off + rank)
        off = off + jnp.floor((count + float(R - 1)) * (1.0 / R)) * float(R)
        tile_cls = tile_cls + jnp.where(tile_row0 >= off[0:8], 1.0, 0.0)
    dest_ref[...] = dest.astype(jnp.int32)
    tile_cls = jnp.minimum(tile_cls, float(len(elo) - 1))
    t_lo = jnp.zeros((8, LANES), F32)
    t_hi = jnp.zeros((8, LANES), F32)
    for k in range(len(elo)):
        t_lo = jnp.where(tile_cls == k, float(elo[k]), t_lo)
        t_hi = jnp.where(tile_cls == k, float(ehi[k]), t_hi)
    sub = lax.broadcasted_iota(jnp.int32, (8, LANES), 0)
    tab = jnp.where(sub == 0, t_lo, jnp.where(sub == 1, t_hi, off[0:8] * (1.0 / R)))
    tab_ref[...] = tab.astype(jnp.int32)


def _moe_plan(cls):
    elo, ehi = _moe_classes()
    rows = cls.shape[0]
    return pl.pallas_call(
        functools.partial(_moe_plan_kernel, elo, ehi),
        out_shape=[jax.ShapeDtypeStruct((rows, LANES), jnp.int32),
                   jax.ShapeDtypeStruct((8, LANES), jnp.int32)],
        name="moe_plan",
    )(cls)


SUBLANES = 8


def _row_copies(n, make):
    def start(g, c):
        base = pl.multiple_of(g * SUBLANES, SUBLANES)
        for s in range(SUBLANES):
            make(base + s).start()
        return c

    def wait(g, c):
        base = pl.multiple_of(g * SUBLANES, SUBLANES)
        for s in range(SUBLANES):
            make(base + s).wait()
        return c

    lax.fori_loop(0, n // SUBLANES, start, 0)
    lax.fori_loop(0, n // SUBLANES, wait, 0)


def _moe_dispatch_kernel(dest_ref, hx_ref, xs_in_ref, xs_ref, sem):
    del xs_in_ref
    _row_copies(hx_ref.shape[0], lambda r: pltpu.make_async_copy(
        hx_ref.at[pl.ds(r, 1)], xs_ref.at[pl.ds(dest_ref[r], 1)], sem))


def _moe_dispatch(dest, hx, n_rows):
    n, w = hx.shape
    tm = ROW_TILE
    return pl.pallas_call(
        _moe_dispatch_kernel,
        grid=(n // tm,),
        in_specs=[pl.BlockSpec((tm,), lambda i: (i,), memory_space=pltpu.SMEM),
                  pl.BlockSpec((tm, w), lambda i: (i, 0)),
                  pl.BlockSpec(memory_space=pl.ANY)],
        out_specs=pl.BlockSpec(memory_space=pl.ANY),
        out_shape=jax.ShapeDtypeStruct((n_rows, w), hx.dtype),
        scratch_shapes=[pltpu.SemaphoreType.DMA],
        input_output_aliases={2: 0},
        compiler_params=_cparams("arbitrary"),
        name="moe_dispatch",
    )(dest, hx, jnp.zeros((n_rows, w), hx.dtype))


def _moe_experts_kernel(elo_ref, ehi_ref, nt_ref, xs_ref, wg0, wu0, wd0, wg1, wu1, wd1, y_ref,
                        wgu_ref, wdn_ref):
    j = pl.program_id(0)
    hw = xs_ref.shape[1] - LANES
    jp = jnp.maximum(j - 1, 0)
    fresh = (j == 0) | (elo_ref[j] != elo_ref[jp]) | (ehi_ref[j] != ehi_ref[jp])

    @pl.when((j < nt_ref[0]) & fresh)
    def _():
        for slot, (wg, wu, wd) in enumerate(((wg0, wu0, wd0), (wg1, wu1, wd1))):
            wgu_ref[slot, 0] = wg[0].astype(BF16)
            wgu_ref[slot, 1] = wu[0].astype(BF16)
            wdn_ref[slot] = wd[0].astype(BF16)

    @pl.when(j < nt_ref[0])
    def _():
        x = xs_ref[:, 0:hw].astype(BF16)
        route = xs_ref[:, hw:hw + LANES]
        lane = lax.broadcasted_iota(jnp.int32, route.shape, 1)
        y = None
        for slot in range(2):
            wt = jnp.sum(jnp.where(lane == 1 + slot, route, 0.0), axis=-1, keepdims=True)
            a = _dot(x, wgu_ref[slot, 0])
            u = _dot(x, wgu_ref[slot, 1])
            he = (a * _sigmoid(a)) * u * wt
            part = _dot(he.astype(BF16), wdn_ref[slot])
            y = part if y is None else y + part
        y_ref[...] = y

    @pl.when(j >= nt_ref[0])
    def _():
        y_ref[...] = jnp.zeros_like(y_ref)


def _moe_experts(tab, xs, wg, wu, wd, layer):
    n_rows, w = xs.shape
    _, ne, d, ff = wg.shape
    R = MOE_TILE
    last = lambda j, nt: jnp.minimum(j, nt[0] - 1)
    wspec = lambda shape, which: pl.BlockSpec(
        (None,) + shape, lambda j, elo, ehi, nt: (layer, (elo, ehi)[which][last(j, nt)], 0, 0))
    grid_spec = pltpu.PrefetchScalarGridSpec(
        num_scalar_prefetch=3,
        grid=(n_rows // R,),
        in_specs=[pl.BlockSpec((R, w), lambda j, elo, ehi, nt: (last(j, nt), 0)),
                  wspec((1, d, ff), 0), wspec((1, d, ff), 0), wspec((1, ff, d), 0),
                  wspec((1, d, ff), 1), wspec((1, d, ff), 1), wspec((1, ff, d), 1)],
        out_specs=pl.BlockSpec((R, d), lambda j, elo, ehi, nt: (j, 0)),
        scratch_shapes=[pltpu.VMEM((2, 2, d, ff), BF16), pltpu.VMEM((2, ff, d), BF16)],
    )
    return pl.pallas_call(
        _moe_experts_kernel,
        grid_spec=grid_spec,
        out_shape=jax.ShapeDtypeStruct((n_rows, d), F32),
        compiler_params=_cparams("arbitrary"),
        name="moe_experts",
    )(tab[0], tab[1], tab[2, 0:1], xs, wg, wu, wd, wg, wu, wd)


def _moe_combine_kernel(dest_ref, x1_ref, y_ref, o_ref, buf_ref, sem):
    _row_copies(buf_ref.shape[0], lambda r: pltpu.make_async_copy(
        y_ref.at[pl.ds(dest_ref[r], 1)], buf_ref.at[pl.ds(r, 1)], sem))
    o_ref[...] = x1_ref[...] + buf_ref[...]


def _moe_combine(dest, x1, y):
    n, d = x1.shape
    tm = ROW_TILE
    return pl.pallas_call(
        _moe_combine_kernel,
        grid=(n // tm,),
        in_specs=[pl.BlockSpec((tm,), lambda i: (i,), memory_space=pltpu.SMEM),
                  pl.BlockSpec((tm, d), lambda i: (i, 0)),
                  pl.BlockSpec(memory_space=pl.ANY)],
        out_specs=pl.BlockSpec((tm, d), lambda i: (i, 0)),
        out_shape=jax.ShapeDtypeStruct((n, d), F32),
        scratch_shapes=[pltpu.VMEM((tm, d), F32), pltpu.SemaphoreType.DMA],
        compiler_params=_cparams("arbitrary"),
        name="moe_combine",
    )(dest, x1, y)


def _moe(x1, hx, wg, wu, wd, layer):
    n, d = x1.shape
    elo, _ = _moe_classes()
    n_rows = n + len(elo) * MOE_TILE
    assert n % LANES == 0 and n_rows // MOE_TILE <= LANES
    cls = hx[:, d].astype(jnp.int32).reshape(n // LANES, LANES)
    dest, tab = _moe_plan(cls)
    dest = dest.reshape(n)
    xs = _moe_dispatch(dest, hx, n_rows)
    y = _moe_experts(tab, xs, wg, wu, wd, layer)
    return _moe_combine(dest, x1, y)


def _pad_cols(w, width):
    return jnp.pad(w, ((0, 0), (0, width - w.shape[1])))


def kernel(x, attn_norm_g, w_in, gla_gk_w2, gla_gk_b, gla_norm_g, moba_qnorm_g, moba_knorm_g,
           hgrn_lb_param, hgrn_norm_g, w_out, ffn_norm_g, w_router_group, w_router_expert,
           w_exp_gate, w_exp_up, w_exp_down):
    batch, seq, d = x.shape
    depth = w_in.shape[0]
    lowrank, gla_kw = gla_gk_w2.shape[1:]
    gla_dv = gla_norm_g.shape[1]
    moba_hd = moba_qnorm_g.shape[1]
    hgrn_kw = hgrn_lb_param.shape[1]
    hgrn_dv = hgrn_norm_g.shape[1]
    mix_w = w_out.shape[1]
    hgrn_heads = 4
    gla_heads = 4
    gla_vw = gla_heads * gla_dv
    hgrn_vw = hgrn_heads * hgrn_dv
    moba_w = mix_w - gla_vw - hgrn_vw
    moba_heads = moba_w // moba_hd
    gla_dk = gla_kw // gla_heads
    hgrn_dk = hgrn_kw // hgrn_heads

    splits = (gla_kw, gla_kw, gla_vw, gla_vw, lowrank, moba_w, moba_w, moba_w,
              hgrn_kw, hgrn_kw, hgrn_vw, hgrn_vw)
    offs = np.concatenate([[0], np.cumsum(splits)]).tolist()
    order = (0, 1, 2, 3, 4, 5, 6, 7, 8, 10, 11, 9)
    pieces, dst = [], 0
    for i in order:
        padded = -(-splits[i] // LANES) * LANES
        pieces.append((offs[i], splits[i], dst, padded))
        dst += padded
    g_gla = 2 * gla_kw + 2 * gla_vw
    g_hg = hgrn_kw + 2 * hgrn_vw
    bounds = np.cumsum([0, g_gla, LANES, 3 * moba_w, g_hg, hgrn_kw]).tolist()
    spans = [(bounds[i], bounds[i + 1]) for i in range(5)]
    groups = (spans[0], spans[1], spans[3], spans[4])
    assert dst == bounds[-1]

    xf = x.reshape(batch * seq, d)
    for l in range(depth):
        y_gla, y_lr, y_hg, y_f, qa, ka, vt = _in_proj(
            xf, attn_norm_g[l], w_in, l, tuple(pieces), groups, (BF16, F32, BF16, F32), spans[2],
            moba_qnorm_g[l], moba_knorm_g[l], batch, seq, moba_heads, moba_hd)

        w2 = jnp.pad(gla_gk_w2[l], ((0, LANES - lowrank), (0, 0)))
        o_a = _lin_call(
            functools.partial(_gla_kernel, gla_heads, gla_dk, gla_dv), "gla",
            batch, seq, gla_heads, gla_dk, gla_dv, (y_gla, y_lr),
            (w2, gla_gk_b[l].reshape(1, gla_kw), jnp.tile(gla_norm_g[l], gla_heads).reshape(1, gla_vw)))
        o_c = _lin_call(
            functools.partial(_hgrn_kernel, hgrn_heads, hgrn_dk, hgrn_dv, l), "hgrn",
            batch, seq, hgrn_heads, hgrn_dk, hgrn_dv, (y_hg, y_f),
            (hgrn_lb_param, jnp.tile(hgrn_norm_g[l], hgrn_heads).reshape(1, hgrn_vw)))

        o_b = _moba_attn(qa, ka, vt, batch, seq, moba_heads, moba_hd)

        wr = _pad_cols(jnp.concatenate([w_router_group[l], w_router_expert[l]], axis=1), LANES)
        x1, hx = _out_proj(xf, o_a, o_b, o_c, w_out[l].astype(BF16), ffn_norm_g[l], wr)
        xf = _moe(x1, hx, w_exp_gate, w_exp_up, w_exp_down, l)
    return xf.reshape(batch, seq, d)
```

```python
import functools

import numpy as np
import jax
import jax.numpy as jnp
from jax import lax
from jax.experimental import pallas as pl
from jax.experimental.pallas import tpu as pltpu

F32 = jnp.float32
BF16 = jnp.bfloat16

NORM_EPS = 1e-6
GLA_GATE_TAU = 16.0
ROPE_THETA = 500000.0
MOBA_BLOCK = 256
MOBA_TOPK = 3
MOE_GROUPS = 4
MOE_EXPERTS_PER_GROUP = 4
MOE_TOPK = 2

LANES = 128
SUBLANES = 8
VMEM_LIMIT = 56 * 1024 * 1024
LIN_CHUNK = 64
LIN_TILE = 512
LIN_GROUP = 8
ROW_TILE = 512
OUT_PROJ_CHUNK = 128
MOE_TILE = 256
NEG = -1e30
LOG2E = 1.4426950408889634


def _cparams(*sem):
    return pltpu.CompilerParams(dimension_semantics=sem, vmem_limit_bytes=VMEM_LIMIT)


def _sigmoid(x):
    return 1.0 / (1.0 + jnp.exp(-x))


def _dot(a, b):
    return jnp.dot(a, b, preferred_element_type=F32)


def _dot_nt(a, b):
    return lax.dot_general(a, b, (((1,), (1,)), ((), ())), preferred_element_type=F32)


def _dot_tn(a, b):
    return lax.dot_general(a, b, (((0,), (0,)), ((), ())), preferred_element_type=F32)


def _hi_lo(x):
    hi = x.astype(BF16)
    return hi, (x - hi.astype(F32)).astype(BF16)


def _dot3(a, b, nt=False):
    a_hi, a_lo = _hi_lo(a)
    b_hi, b_lo = _hi_lo(b)
    lhs = jnp.concatenate([a_hi, a_hi, a_lo], axis=1)
    if nt:
        return _dot_nt(lhs, jnp.concatenate([b_hi, b_lo, b_hi], axis=1))
    return _dot(lhs, jnp.concatenate([b_hi, b_lo, b_hi], axis=0))


def _in_proj_kernel(pieces, groups, moba_cols, heads, hd, tiles_per_seq,
                    x_ref, g_ref, w_ref, qg_ref, kg_ref, c_ref, sp_ref, sm_ref,
                    ones_ref, hm_ref, pexp_ref, *refs):
    out_refs = refs[:len(groups)]
    qa_ref, ka_ref, vt_ref, wcat_ref, kmt_ref = refs[len(groups):]
    i = pl.program_id(0)

    @pl.when(i == 0)
    def _():
        for src, width, dst, padded in pieces:
            blk = w_ref[:, src:src + width].astype(BF16)
            if padded > width:
                blk = jnp.concatenate([blk, jnp.zeros((blk.shape[0], padded - width), BF16)], axis=1)
            wcat_ref[:, dst:dst + padded] = blk

    tile = lax.rem(i, tiles_per_seq)

    @pl.when(tile == 0)
    def _():
        kmt_ref[...] = jnp.zeros_like(kmt_ref)

    x = x_ref[...]
    ms = jnp.mean(x * x, axis=-1, keepdims=True)
    h = (x * lax.rsqrt(ms + NORM_EPS) * g_ref[...]).astype(BF16)
    ym = _dot(h, wcat_ref[:, moba_cols[0]:moba_cols[1]])
    for (a, b), o_ref in zip(groups, out_refs):
        o_ref[...] = _dot(h, wcat_ref[:, a:b]).astype(o_ref.dtype)

    W = heads * hd
    blocks = x.shape[0] // MOBA_BLOCK
    for blk in range(blocks):
        rows = slice(blk * MOBA_BLOCK, (blk + 1) * MOBA_BLOCK)

        def write_vt(hh, vrows, value, blk=blk):
            vt_ref[0, hh, blk, vrows, :] = value.astype(vt_ref.dtype)

        _moba_prep_block(
            heads, hd, tile * blocks + blk, ym[rows, 0:W], ym[rows, W:2 * W], ym[rows, 2 * W:3 * W],
            (c_ref[rows, :], sp_ref[rows, :], sm_ref[rows, :]), qg_ref[...], kg_ref[...],
            ones_ref, hm_ref, pexp_ref, kmt_ref, qa_ref.at[rows], ka_ref.at[rows], write_vt)


def _in_proj(x, g, w_all, layer, pieces, groups, dtypes, moba_cols, qg, kg, batch, seq, heads, hd):
    n, d = x.shape
    tm = ROW_TILE
    cols = w_all.shape[2]
    W = heads * hd
    nt = seq // MOBA_BLOCK
    blocks = tm // MOBA_BLOCK
    tiles_per_seq = seq // tm
    tabs, consts = _moba_constants(heads, hd, seq)
    tile_g = lambda v: jnp.tile(v.astype(F32), heads).reshape(1, W)
    row = lambda width: pl.BlockSpec((tm, width), lambda i: (i, 0))
    full = lambda a: pl.BlockSpec(a.shape, lambda i: (0,) * a.ndim)
    wide_sd = jax.ShapeDtypeStruct((n, heads * LANES), BF16)
    out_shape = [jax.ShapeDtypeStruct((n, b - a), dt) for (a, b), dt in zip(groups, dtypes)]
    vt_rows = _moba_vt_rows(hd)
    out_shape += [wide_sd, wide_sd, jax.ShapeDtypeStruct((batch, heads, nt, vt_rows, MOBA_BLOCK), BF16)]
    out_specs = [row(b - a) for (a, b) in groups] + [row(heads * LANES), row(heads * LANES)]
    out_specs.append(pl.BlockSpec((1, heads, blocks, vt_rows, MOBA_BLOCK),
                                  lambda i: (i // tiles_per_seq, 0, i % tiles_per_seq, 0, 0)))
    return pl.pallas_call(
        functools.partial(_in_proj_kernel, pieces, groups, moba_cols, heads, hd, tiles_per_seq),
        grid=(n // tm,),
        in_specs=[row(d),
                  pl.BlockSpec((1, d), lambda i: (0, 0)),
                  pl.BlockSpec((None, d, cols), lambda i: (layer, 0, 0), pipeline_mode=pl.Buffered(1)),
                  full(tile_g(qg)), full(tile_g(kg))]
        + [pl.BlockSpec((tm, LANES), lambda i: (i % tiles_per_seq, 0)) for _ in tabs]
        + [full(a) for a in consts],
        out_specs=out_specs,
        out_shape=out_shape,
        scratch_shapes=[pltpu.VMEM((d, pieces[-1][2] + pieces[-1][3]), BF16),
                        pltpu.VMEM((LANES, W), F32)],
        compiler_params=_cparams("arbitrary"),
        name="in_proj",
    )(x, g.reshape(1, d), w_all, tile_g(qg), tile_g(kg), *tabs, *consts)


def _lin_constants(heads, dk, dv):
    L = LIN_CHUNK
    nlev = int(np.log2(L))
    K, V = heads * dk, heads * dv
    i = np.arange(L)[:, None]
    t = np.arange(L)[None, :]
    w_rows = [(t <= i), (t > i)]
    masks = []
    for lev in range(nlev):
        b = L >> lev
        half = b // 2
        r = (i // b) * b + half - 1
        w_rows.append((t > np.minimum(i, r)) & (t <= np.maximum(i, r)))
        j = t
        masks.append((i // b == j // b) & (i % b >= half) & (j % b < half))
    masks.append(i == t)
    w_all = np.concatenate(w_rows, axis=0).astype(np.float32)
    m_all = np.stack([np.tile(m, (1, heads)) for m in masks]).astype(np.float32)
    rh = np.repeat(np.arange(heads), L)[:, None]
    bdk = (rh == np.repeat(np.arange(heads), dk)[None, :]).astype(np.float32)
    bdv = (rh == np.repeat(np.arange(heads), dv)[None, :]).astype(np.float32)
    bds = (np.repeat(np.arange(heads), dv)[:, None]
           == np.repeat(np.arange(heads), dk)[None, :]).astype(np.float32)
    ones_v = (np.repeat(np.arange(heads), dv)[:, None]
              == np.repeat(np.arange(heads), dv)[None, :]).astype(np.float32)
    return (jnp.asarray(w_all, BF16), jnp.asarray(m_all, F32), jnp.asarray(bdk, BF16),
            jnp.asarray(bdv, BF16), jnp.asarray(bds, F32), jnp.asarray(ones_v, BF16))


def _lin_group(q, k, v, la, g, gain, st_ref, w_ref, m_ref, bdk_ref, bdv_ref, bds_ref, ones_ref,
               heads, dv):
    L = LIN_CHUNK
    G = q.shape[0] // L
    nlev = m_ref.shape[0] - 1
    K = q.shape[1]
    chunks = [slice(c * L, (c + 1) * L) for c in range(G)]
    lacat = jnp.concatenate(_hi_lo(la), axis=1)
    v_bf = v.astype(BF16)
    es = []
    for c in chunks:
        z2 = _dot(w_ref[...], lacat[c])
        es.append(jnp.exp(z2[:, :K] + z2[:, K:]))
    bdk = bdk_ref[...]
    scores = [None] * G
    for lev in range(nlev + 1):
        for i, c in enumerate(chunks):
            if lev < nlev:
                el = es[i][(2 + lev) * L:(3 + lev) * L]
                ql = (q[c] * el).astype(BF16)
                kl = (k[c] * el).astype(BF16)
            else:
                ql = q[c].astype(BF16)
                kl = k[c].astype(BF16)
            kbd = jnp.concatenate([kl] * heads, axis=0) * bdk
            s = _dot_nt(ql, kbd) * m_ref[lev]
            scores[i] = s if scores[i] is None else scores[i] + s
    bdv = bdv_ref[...]
    intra, kvs = [], []
    for i, c in enumerate(chunks):
        vbd = jnp.concatenate([v_bf[c]] * heads, axis=0) * bdv
        intra.append(_dot(scores[i].astype(BF16), vbd))
        kb = (k[c] * es[i][L:2 * L]).astype(BF16)
        kvs.append(_dot_tn(v_bf[c], kb) * bds_ref[...])
    st = st_ref[...]
    outs = []
    for i, c in enumerate(chunks):
        outs.append(intra[i] + _dot_nt((q[c] * es[i][0:L]).astype(BF16), st.astype(BF16)))
        st = st * es[i][L - 1:L] + kvs[i]
    st_ref[...] = st
    o = jnp.concatenate(outs, axis=0)
    ms = _dot((o * o).astype(BF16), ones_ref[...]) * (1.0 / dv)
    gf = g.astype(F32)
    return o * lax.rsqrt(ms + NORM_EPS) * gain * (gf * _sigmoid(gf))


def _gla_kernel(heads, dk, dv, x_ref, lr_ref, w2_ref, b_ref, gain_ref,
                w_ref, m_ref, bdk_ref, bdv_ref, bds_ref, ones_ref, o_ref, st_ref):
    K, V = heads * dk, heads * dv
    L = LIN_CHUNK

    @pl.when(pl.program_id(1) == 0)
    def _():
        st_ref[...] = jnp.zeros_like(st_ref)

    gain = gain_ref[...]
    w2 = w2_ref[...]
    bias = b_ref[...]

    R = LIN_GROUP * L

    def body(c, carry):
        rows = pl.ds(pl.multiple_of(c * R, R), R)
        q = x_ref[rows, 0:K].astype(F32) * (dk ** -0.5)
        k = x_ref[rows, K:2 * K].astype(F32)
        v = x_ref[rows, 2 * K:2 * K + V]
        g = x_ref[rows, 2 * K + V:2 * K + 2 * V]
        gk = _dot3(lr_ref[rows, :], w2) + bias
        la = (jnp.minimum(gk, 0.0) - jnp.log1p(jnp.exp(-jnp.abs(gk)))) * (1.0 / GLA_GATE_TAU)
        o_ref[rows, :] = _lin_group(q, k, v, la, g, gain, st_ref, w_ref, m_ref, bdk_ref, bdv_ref,
                                    bds_ref, ones_ref, heads, dv).astype(o_ref.dtype)
        return carry

    lax.fori_loop(0, x_ref.shape[0] // R, body, 0)


def _hgrn_kernel(heads, dk, dv, layer, x_ref, f_ref, lbp_ref, gain_ref,
                 w_ref, m_ref, bdk_ref, bdv_ref, bds_ref, ones_ref, o_ref, st_ref):
    K, V = heads * dk, heads * dv
    L = LIN_CHUNK

    @pl.when(pl.program_id(1) == 0)
    def _():
        st_ref[...] = jnp.zeros_like(st_ref)

    lbp = lbp_ref[...]
    depth = lbp.shape[0]
    mx = lbp[0:1]
    for r in range(1, depth):
        mx = jnp.maximum(mx, lbp[r:r + 1])
    ex = [jnp.exp(lbp[r:r + 1] - mx) for r in range(depth)]
    den = ex[0]
    for r in range(1, depth):
        den = den + ex[r]
    lb = jnp.zeros_like(den)
    for r in range(1, layer + 1):
        lb = lb + ex[r] / den
    gain = gain_ref[...]

    R = LIN_GROUP * L

    def body(c, carry):
        rows = pl.ds(pl.multiple_of(c * R, R), R)
        cq = x_ref[rows, 0:K].astype(F32)
        q = cq * _sigmoid(cq)
        v = x_ref[rows, K:K + V]
        g = x_ref[rows, K + V:K + 2 * V]
        f = f_ref[rows, :]
        la = jnp.log(lb + (1.0 - lb) * _sigmoid(f))
        k = (1.0 - lb) * _sigmoid(-f)
        o_ref[rows, :] = _lin_group(q, k, v, la, g, gain, st_ref, w_ref, m_ref, bdk_ref, bdv_ref,
                                    bds_ref, ones_ref, heads, dv).astype(o_ref.dtype)
        return carry

    lax.fori_loop(0, x_ref.shape[0] // R, body, 0)


def _lin_call(kernel_fn, name, batch, seq, heads, dk, dv, row_inputs, small_inputs):
    T = LIN_TILE
    nt = seq // T
    consts = _lin_constants(heads, dk, dv)
    row_specs = [pl.BlockSpec((T, a.shape[1]), lambda b, t: (b * nt + t, 0)) for a in row_inputs]
    full = lambda a: pl.BlockSpec(a.shape, lambda b, t: (0,) * a.ndim)
    return pl.pallas_call(
        kernel_fn,
        grid=(batch, nt),
        in_specs=row_specs + [full(a) for a in small_inputs] + [full(a) for a in consts],
        out_specs=pl.BlockSpec((T, heads * dv), lambda b, t: (b * nt + t, 0)),
        out_shape=jax.ShapeDtypeStruct((batch * seq, heads * dv), BF16),
        scratch_shapes=[pltpu.VMEM((heads * dv, heads * dk), F32)],
        compiler_params=_cparams("parallel", "arbitrary"),
        name=name,
    )(*row_inputs, *small_inputs, *consts)


def _moba_constants(heads, hd, seq):
    W = heads * hd
    nb = LANES // heads
    rot = hd // 4
    half = rot // 2
    inv_freq = ROPE_THETA ** (-(np.arange(0, rot, 2, dtype=np.float64) / rot))
    ang = np.arange(seq, dtype=np.float64)[:, None] * inv_freq[None, :]
    cos, sin = np.cos(ang), np.sin(ang)
    c = np.ones((seq, hd), np.float32)
    sp = np.zeros((seq, hd), np.float32)
    sm = np.zeros((seq, hd), np.float32)
    c[:, :half] = cos
    c[:, half:rot] = cos
    sm[:, :half] = -sin
    sp[:, half:rot] = sin
    reps = LANES // hd
    tabs = [jnp.asarray(np.tile(a, (1, reps)), F32) for a in (c, sp, sm)]
    hl = np.repeat(np.arange(heads), hd)
    ones_h = (hl[:, None] == hl[None, :]).astype(np.float32)
    hm = (np.arange(heads)[:, None] == hl[None, :]).astype(np.float32)
    pb = np.zeros((LANES, heads * LANES), np.float32)
    gl = np.arange(LANES)
    pb[gl, (gl % heads) * LANES + hd + gl // heads] = 1.0
    return tabs, (jnp.asarray(ones_h, BF16), jnp.asarray(hm, F32), jnp.asarray(pb, BF16))


def _moba_prep_block(heads, hd, t, xq, xk, xv, tabs, qg, kg, ones_ref, hm_ref, pexp_ref, kmt_ref,
                     qa_ref, ka_ref, write_vt):
    W = heads * hd
    half = hd // 8
    reps = W // LANES
    c, sp, sm = (jnp.concatenate([a] * reps, axis=1) for a in tabs)

    def norm_rot(x, gain):
        ms = _dot((x * x).astype(BF16), ones_ref[...]) * (1.0 / hd)
        y = x * lax.rsqrt(ms + NORM_EPS) * gain
        return y * c + pltpu.roll(y, half, 1) * sp + pltpu.roll(y, W - half, 1) * sm

    q = norm_rot(xq, qg)
    k = norm_rot(xk, kg)

    gate = _dot3(q, kmt_ref[...], nt=True)
    lane = lax.broadcasted_iota(jnp.int32, gate.shape, 1)
    blk = lax.shift_right_logical(lane, int(np.log2(heads)))
    valid = blk < t
    gate = jnp.where(valid, gate, -jnp.inf)
    rank = jnp.zeros(gate.shape, F32)
    nslots = LANES // heads
    for s in range(1, nslots):
        other = pltpu.roll(gate, s * heads, 1)
        ahead = (other > gate) | ((other == gate) & (blk >= s))
        rank = rank + ahead.astype(F32)
    sel = valid & (rank < float(MOBA_TOPK))
    bias = jnp.where(sel | (blk == t), 0.0, NEG)

    bias_slab = _dot(bias.astype(BF16), pexp_ref[...])
    qs = q * (hd ** -0.5 * LOG2E)
    lane1 = lax.broadcasted_iota(jnp.int32, (q.shape[0], LANES), 1)
    onehot = jnp.where(lane1 == hd + t, 1.0, 0.0)
    per_vreg = LANES // hd
    for h in range(heads):
        src = slice((h // per_vreg) * LANES, (h // per_vreg + 1) * LANES)
        dst = slice(h * LANES, (h + 1) * LANES)
        shift = (LANES - (h % per_vreg) * hd) % LANES
        move = (lambda a: a) if shift == 0 else (lambda a: pltpu.roll(a, shift, 1))
        qa_ref[:, dst] = jnp.where(lane1 < hd, move(qs[:, src]), bias_slab[:, dst]).astype(qa_ref.dtype)
        ka_ref[:, dst] = jnp.where(lane1 < hd, move(k[:, src]), onehot).astype(ka_ref.dtype)

    kmean = jnp.mean(k, axis=0, keepdims=True)
    row0 = pl.multiple_of(t * heads, heads)
    kmt_ref[pl.ds(row0, heads), :] = kmean * hm_ref[...]

    vt = xv.T
    vt_rows = _moba_vt_rows(hd)
    ones = jnp.ones((vt_rows - hd, vt.shape[1]), F32)
    for h in range(heads):
        write_vt(h, slice(0, hd), vt[h * hd:(h + 1) * hd])
        write_vt(h, slice(hd, vt_rows), ones)


def _moba_vt_rows(hd):
    return hd + 2 * SUBLANES


def _moba_attn_kernel(hd, qa_ref, ka_ref, vt_ref, o_ref, acc_ref, sa_ref, sb_ref):
    t = pl.program_id(1)
    tq = qa_ref.shape[0]
    blk = MOBA_BLOCK
    nh = vt_ref.shape[1]

    def scores_to(dst_ref, n, hh):
        cols = slice(hh * LANES, (hh + 1) * LANES)
        rows = pl.ds(pl.multiple_of(n * blk, blk), blk)
        dst_ref[hh] = _dot_nt(ka_ref[rows, cols], qa_ref[:, cols])

    def update(s, m, n, hh):
        m_new = jnp.maximum(m, jnp.max(s, axis=0, keepdims=True))
        alpha = jnp.exp2(m - m_new)
        p = jnp.exp2(s - m_new).astype(BF16)
        acc_ref[hh] = alpha * acc_ref[hh] + _dot(vt_ref[0, hh, n], p)
        return m_new

    def step(cur_ref, nxt_ref, n, ms):
        new = []
        for hh in range(nh):
            scores_to(nxt_ref, n + 1, hh)
            new.append(update(cur_ref[hh], ms[hh], n, hh))
        return tuple(new)

    acc_ref[...] = jnp.zeros_like(acc_ref)
    for hh in range(nh):
        scores_to(sa_ref, 0, hh)

    def pair(i, ms):
        ms = step(sa_ref, sb_ref, 2 * i, ms)
        return step(sb_ref, sa_ref, 2 * i + 1, ms)

    m0 = jnp.full((1, tq), 2 * NEG, F32)
    ms = lax.fori_loop(0, t // 2, pair, (m0,) * nh)

    def odd_block(ms):
        new = []
        for hh in range(nh):
            new.append(update(sa_ref[hh], ms[hh], t - 1, hh))
            scores_to(sa_ref, t, hh)
        return tuple(new)

    ms = lax.cond(t % 2 == 1, odd_block, lambda ms: ms, ms)

    key = lax.broadcasted_iota(jnp.int32, (blk, tq), 0)
    qry = lax.broadcasted_iota(jnp.int32, (blk, tq), 1)
    outs = []
    for hh in range(nh):
        s = jnp.where(key <= qry, sa_ref[hh], NEG)
        update(s, ms[hh], t, hh)
        a = acc_ref[hh]
        outs.append(a[0:hd] / a[hd:hd + 1])
    o_ref[...] = jnp.concatenate(outs, axis=0).T.astype(o_ref.dtype)


def _moba_attn(qa, ka, vt, batch, seq, heads, hd):
    nt = seq // MOBA_BLOCK
    vt_rows = vt.shape[3]
    return pl.pallas_call(
        functools.partial(_moba_attn_kernel, hd),
        grid=(batch, nt),
        in_specs=[pl.BlockSpec((MOBA_BLOCK, heads * LANES), lambda b, t: (b * nt + t, 0)),
                  pl.BlockSpec((seq, heads * LANES), lambda b, t: (b, 0)),
                  pl.BlockSpec((1, heads, nt, vt_rows, MOBA_BLOCK), lambda b, t: (b, 0, 0, 0, 0))],
        out_specs=pl.BlockSpec((MOBA_BLOCK, heads * hd), lambda b, t: (b * nt + t, 0)),
        out_shape=jax.ShapeDtypeStruct((batch * seq, heads * hd), BF16),
        scratch_shapes=[pltpu.VMEM((heads, vt_rows, MOBA_BLOCK), F32),
                        pltpu.VMEM((heads, MOBA_BLOCK, MOBA_BLOCK), F32),
                        pltpu.VMEM((heads, MOBA_BLOCK, MOBA_BLOCK), F32)],
        compiler_params=_cparams("parallel", "arbitrary"),
        name="moba_attn",
    )(qa, ka, vt)


def _out_proj_kernel(widths, x_ref, oa_ref, ob_ref, oc_ref, w_ref, g_ref, wr_ref,
                     x1_ref, hx_ref, route_ref):
    rows = x_ref.shape[0]
    hw = x_ref.shape[1]
    chunks = [slice(r, r + OUT_PROJ_CHUNK) for r in range(0, rows, OUT_PROJ_CHUNK)]
    accs = []
    for c in chunks:
        acc = x_ref[c, :]
        r0 = 0
        for o_ref, wd in zip((oa_ref, ob_ref, oc_ref), widths):
            acc = acc + _dot(o_ref[c, :], w_ref[r0:r0 + wd, :])
            r0 += wd
        x1_ref[c, :] = acc
        accs.append(acc)
    wr_hi, wr_lo = _hi_lo(wr_ref[...])
    wr_cat = jnp.concatenate([wr_hi, wr_lo, wr_hi], axis=0)
    all_logits = []
    for c, acc in zip(chunks, accs):
        ms = jnp.mean(acc * acc, axis=-1, keepdims=True)
        h2 = acc * lax.rsqrt(ms + NORM_EPS) * g_ref[...]
        hx_ref[c, 0:hw] = h2
        h_hi, h_lo = _hi_lo(h2)
        all_logits.append(_dot(jnp.concatenate([h_hi, h_hi, h_lo], axis=1), wr_cat))
    for c, logits in zip(chunks, all_logits):
        route = _route(logits)
        hx_ref[c, hw:hw + LANES] = route
        route_ref[c, :] = route


def _route(logits):
    G, E = MOE_GROUPS, MOE_EXPERTS_PER_GROUP
    lane = lax.broadcasted_iota(jnp.int32, logits.shape, 1).astype(F32)
    big = float(LANES)
    is_g = lane < G
    lg = jnp.where(is_g, logits, -jnp.inf)
    mg = jnp.max(lg, axis=-1, keepdims=True)
    gate_group = 1.0 / jnp.sum(jnp.exp(lg - mg), axis=-1, keepdims=True)
    g_sel = jnp.min(jnp.where(lg == mg, lane, big), axis=-1, keepdims=True)
    lo = G + E * g_sel
    in_grp = (lane >= lo) & (lane < lo + E)
    v = jnp.where(in_grp, logits, -jnp.inf)
    v1 = jnp.max(v, axis=-1, keepdims=True)
    i1 = jnp.min(jnp.where(v == v1, lane, big), axis=-1, keepdims=True)
    v = jnp.where(lane == i1, -jnp.inf, v)
    v2 = jnp.max(v, axis=-1, keepdims=True)
    i2 = jnp.min(jnp.where(v == v2, lane, big), axis=-1, keepdims=True)
    e2 = jnp.exp(v2 - v1)
    w1 = gate_group / (1.0 + e2)
    w2 = gate_group * e2 / (1.0 + e2)
    first = i1 < i2
    a = jnp.minimum(i1, i2) - lo
    b = jnp.maximum(i1, i2) - lo
    w_lo = jnp.where(first, w1, w2)
    w_hi = jnp.where(first, w2, w1)
    pos = jnp.zeros_like(a)
    swap = a < 0.0
    for p, (ea, eb) in enumerate(_moe_pair_order()):
        hit = (a == float(min(ea, eb))) & (b == float(max(ea, eb)))
        pos = jnp.where(hit, float(p), pos)
        if ea > eb:
            swap = swap | hit
    cls = g_sel * float(E * (E - 1) // 2) + pos
    route = jnp.where(lane == 0.0, cls, 0.0)
    route = jnp.where(lane == 1.0, jnp.where(swap, w_hi, w_lo), route)
    return jnp.where(lane == 2.0, jnp.where(swap, w_lo, w_hi), route)


def _out_proj(x, oa, ob, oc, w, g, wr):
    n, d = x.shape
    tm = ROW_TILE
    widths = (oa.shape[1], ob.shape[1], oc.shape[1])
    row = lambda width: pl.BlockSpec((tm, width), lambda i: (i, 0))
    full = lambda a: pl.BlockSpec(a.shape, lambda i: (0, 0))
    g2 = g.reshape(1, d)
    hxw = d + LANES
    return pl.pallas_call(
        functools.partial(_out_proj_kernel, widths),
        grid=(n // tm,),
        in_specs=[row(d), row(widths[0]), row(widths[1]), row(widths[2]), full(w), full(g2), full(wr)],
        out_specs=[row(d), row(hxw), row(LANES)],
        out_shape=[jax.ShapeDtypeStruct((n, d), F32), jax.ShapeDtypeStruct((n, hxw), F32),
                   jax.ShapeDtypeStruct((n, LANES), F32)],
        compiler_params=_cparams("parallel"),
        name="out_proj",
    )(x, oa, ob, oc, w, g2, wr)


@functools.lru_cache(maxsize=None)
def _moe_pair_order():
    E = MOE_EXPERTS_PER_GROUP
    todo = {(a, b) for a in range(E) for b in range(a + 1, E)}

    def extend(seq, todo):
        if not todo:
            return seq
        for pr in sorted(todo):
            for cand in (pr, pr[::-1]):
                if not seq or cand[0] == seq[-1][0] or cand[1] == seq[-1][1]:
                    out = extend(seq + [cand], todo - {pr})
                    if out:
                        return out
        return None

    return tuple(extend([], todo))


def _moe_classes():
    G, E = MOE_GROUPS, MOE_EXPERTS_PER_GROUP
    ea = [g * E + a for g in range(G) for a, _ in _moe_pair_order()]
    eb = [g * E + b for g in range(G) for _, b in _moe_pair_order()]
    return ea, eb


def _moe_plan_kernel(elo, ehi, cls_ref, dest_ref, tab_ref):
    R = MOE_TILE
    cls = cls_ref[...]
    rows = cls.shape[0]
    ri = lax.broadcasted_iota(jnp.int32, (LANES, LANES), 0)
    ci = lax.broadcasted_iota(jnp.int32, (LANES, LANES), 1)
    before_lane = (ri < ci).astype(BF16)
    all_lane = jnp.ones((LANES, LANES), BF16)
    rr = lax.broadcasted_iota(jnp.int32, (rows, rows), 0)
    rc = lax.broadcasted_iota(jnp.int32, (rows, rows), 1)
    before_row = (rc < rr).astype(BF16)
    all_row = jnp.ones((rows, rows), BF16)
    tile_row0 = lax.broadcasted_iota(jnp.int32, (8, LANES), 1).astype(F32) * float(R)
    off = jnp.zeros((rows, LANES), F32)
    dest = jnp.zeros((rows, LANES), F32)
    tile_cls = jnp.zeros((8, LANES), F32)
    for k in range(len(elo)):
        mask = (cls == k).astype(BF16)
        row_tot = _dot(mask, all_lane)
        rank = _dot(mask, before_lane) + _dot(before_row, row_tot.astype(BF16))
        count = _dot(all_row, row_tot.astype(BF16))
        dest = dest + mask.astype(F32) * (off + rank)
        off = off + jnp.floor((count + float(R - 1)) * (1.0 / R)) * float(R)
        tile_cls = tile_cls + jnp.where(tile_row0 >= off[0:8], 1.0, 0.0)
    dest_ref[...] = dest.astype(jnp.int32)
    tile_cls = jnp.minimum(tile_cls, float(len(elo) - 1))
    t_lo = jnp.zeros((8, LANES), F32)
    t_hi = jnp.zeros((8, LANES), F32)
    for k in range(len(elo)):
        t_lo = jnp.where(tile_cls == k, float(elo[k]), t_lo)
        t_hi = jnp.where(tile_cls == k, float(ehi[k]), t_hi)
    sub = lax.broadcasted_iota(jnp.int32, (8, LANES), 0)
    tab = jnp.where(sub == 0, t_lo, jnp.where(sub == 1, t_hi, off[0:8] * (1.0 / R)))
    tab_ref[...] = tab.astype(jnp.int32)


def _moe_plan(cls):
    elo, ehi = _moe_classes()
    rows = cls.shape[0]
    return pl.pallas_call(
        functools.partial(_moe_plan_kernel, elo, ehi),
        out_shape=[jax.ShapeDtypeStruct((rows, LANES), jnp.int32),
                   jax.ShapeDtypeStruct((8, LANES), jnp.int32)],
        name="moe_plan",
    )(cls)


def _row_copies(n, make):
    def start(g, c):
        base = pl.multiple_of(g * SUBLANES, SUBLANES)
        for s in range(SUBLANES):
            make(base + s).start()
        return c

    def wait(g, c):
        base = pl.multiple_of(g * SUBLANES, SUBLANES)
        for s in range(SUBLANES):
            make(base + s).wait()
        return c

    lax.fori_loop(0, n // SUBLANES, start, 0)
    lax.fori_loop(0, n // SUBLANES, wait, 0)


def _moe_dispatch_kernel(dest_ref, hx_ref, xs_in_ref, xs_ref, sem):
    del xs_in_ref
    _row_copies(hx_ref.shape[0], lambda r: pltpu.make_async_copy(
        hx_ref.at[pl.ds(r, 1)], xs_ref.at[pl.ds(dest_ref[r], 1)], sem))


def _moe_dispatch(dest, hx, n_rows):
    n, w = hx.shape
    tm = ROW_TILE
    return pl.pallas_call(
        _moe_dispatch_kernel,
        grid=(n // tm,),
        in_specs=[pl.BlockSpec((tm,), lambda i: (i,), memory_space=pltpu.SMEM),
                  pl.BlockSpec((tm, w), lambda i: (i, 0)),
                  pl.BlockSpec(memory_space=pl.ANY)],
        out_specs=pl.BlockSpec(memory_space=pl.ANY),
        out_shape=jax.ShapeDtypeStruct((n_rows, w), hx.dtype),
        scratch_shapes=[pltpu.SemaphoreType.DMA],
        input_output_aliases={2: 0},
        compiler_params=_cparams("arbitrary"),
        name="moe_dispatch",
    )(dest, hx, jnp.zeros((n_rows, w), hx.dtype))


def _moe_experts_kernel(elo_ref, ehi_ref, nt_ref, xs_ref, wg0, wu0, wd0, wg1, wu1, wd1, y_ref,
                        wgu_ref, wdn_ref):
    j = pl.program_id(0)
    hw = xs_ref.shape[1] - LANES
    jp = jnp.maximum(j - 1, 0)
    fresh = (j == 0) | (elo_ref[j] != elo_ref[jp]) | (ehi_ref[j] != ehi_ref[jp])

    @pl.when((j < nt_ref[0]) & fresh)
    def _():
        for slot, (wg, wu, wd) in enumerate(((wg0, wu0, wd0), (wg1, wu1, wd1))):
            wgu_ref[slot, 0] = wg[0].astype(BF16)
            wgu_ref[slot, 1] = wu[0].astype(BF16)
            wdn_ref[slot] = wd[0].astype(BF16)

    @pl.when(j < nt_ref[0])
    def _():
        x = xs_ref[:, 0:hw].astype(BF16)
        route = xs_ref[:, hw:hw + LANES]
        lane = lax.broadcasted_iota(jnp.int32, route.shape, 1)
        y = None
        for slot in range(2):
            wt = jnp.sum(jnp.where(lane == 1 + slot, route, 0.0), axis=-1, keepdims=True)
            a = _dot(x, wgu_ref[slot, 0])
            u = _dot(x, wgu_ref[slot, 1])
            he = (a * _sigmoid(a)) * u * wt
            part = _dot(he.astype(BF16), wdn_ref[slot])
            y = part if y is None else y + part
        y_ref[...] = y

    @pl.when(j >= nt_ref[0])
    def _():
        y_ref[...] = jnp.zeros_like(y_ref)


def _moe_experts(tab, xs, wg, wu, wd, layer):
    n_rows, w = xs.shape
    _, ne, d, ff = wg.shape
    R = MOE_TILE
    last = lambda j, nt: jnp.minimum(j, nt[0] - 1)
    wspec = lambda shape, which: pl.BlockSpec(
        (None,) + shape, lambda j, elo, ehi, nt: (layer, (elo, ehi)[which][last(j, nt)], 0, 0))
    grid_spec = pltpu.PrefetchScalarGridSpec(
        num_scalar_prefetch=3,
        grid=(n_rows // R,),
        in_specs=[pl.BlockSpec((R, w), lambda j, elo, ehi, nt: (last(j, nt), 0)),
                  wspec((1, d, ff), 0), wspec((1, d, ff), 0), wspec((1, ff, d), 0),
                  wspec((1, d, ff), 1), wspec((1, d, ff), 1), wspec((1, ff, d), 1)],
        out_specs=pl.BlockSpec((R, d), lambda j, elo, ehi, nt: (j, 0)),
        scratch_shapes=[pltpu.VMEM((2, 2, d, ff), BF16), pltpu.VMEM((2, ff, d), BF16)],
    )
    return pl.pallas_call(
        _moe_experts_kernel,
        grid_spec=grid_spec,
        out_shape=jax.ShapeDtypeStruct((n_rows, d), F32),
        compiler_params=_cparams("arbitrary"),
        name="moe_experts",
    )(tab[0], tab[1], tab[2, 0:1], xs, wg, wu, wd, wg, wu, wd)


def _moe_combine_kernel(dest_ref, x1_ref, y_ref, o_ref, buf_ref, sem):
    _row_copies(buf_ref.shape[0], lambda r: pltpu.make_async_copy(
        y_ref.at[pl.ds(dest_ref[r], 1)], buf_ref.at[pl.ds(r, 1)], sem))
    o_ref[...] = x1_ref[...] + buf_ref[...]


def _moe_combine(dest, x1, y):
    n, d = x1.shape
    tm = ROW_TILE
    return pl.pallas_call(
        _moe_combine_kernel,
        grid=(n // tm,),
        in_specs=[pl.BlockSpec((tm,), lambda i: (i,), memory_space=pltpu.SMEM),
                  pl.BlockSpec((tm, d), lambda i: (i, 0)),
                  pl.BlockSpec(memory_space=pl.ANY)],
        out_specs=pl.BlockSpec((tm, d), lambda i: (i, 0)),
        out_shape=jax.ShapeDtypeStruct((n, d), F32),
        scratch_shapes=[pltpu.VMEM((tm, d), F32), pltpu.SemaphoreType.DMA],
        compiler_params=_cparams("arbitrary"),
        name="moe_combine",
    )(dest, x1, y)


def _moe(x1, hx, route, wg, wu, wd, layer):
    n, d = x1.shape
    elo, _ = _moe_classes()
    n_rows = n + len(elo) * MOE_TILE
    assert n % LANES == 0 and n_rows // MOE_TILE <= LANES
    cls = route[:, 0].astype(jnp.int32).reshape(n // LANES, LANES)
    dest, tab = _moe_plan(cls)
    dest = dest.reshape(n)
    xs = _moe_dispatch(dest, hx, n_rows)
    y = _moe_experts(tab, xs, wg, wu, wd, layer)
    return _moe_combine(dest, x1, y)


def _pad_cols(w, width):
    return jnp.pad(w, ((0, 0), (0, width - w.shape[1])))


def kernel(x, attn_norm_g, w_in, gla_gk_w2, gla_gk_b, gla_norm_g, moba_qnorm_g, moba_knorm_g,
           hgrn_lb_param, hgrn_norm_g, w_out, ffn_norm_g, w_router_group, w_router_expert,
           w_exp_gate, w_exp_up, w_exp_down):
    batch, seq, d = x.shape
    depth = w_in.shape[0]
    lowrank, gla_kw = gla_gk_w2.shape[1:]
    gla_dv = gla_norm_g.shape[1]
    moba_hd = moba_qnorm_g.shape[1]
    hgrn_kw = hgrn_lb_param.shape[1]
    hgrn_dv = hgrn_norm_g.shape[1]
    mix_w = w_out.shape[1]
    hgrn_heads = 4
    gla_heads = 4
    gla_vw = gla_heads * gla_dv
    hgrn_vw = hgrn_heads * hgrn_dv
    moba_w = mix_w - gla_vw - hgrn_vw
    moba_heads = moba_w // moba_hd
    gla_dk = gla_kw // gla_heads
    hgrn_dk = hgrn_kw // hgrn_heads

    splits = (gla_kw, gla_kw, gla_vw, gla_vw, lowrank, moba_w, moba_w, moba_w,
              hgrn_kw, hgrn_kw, hgrn_vw, hgrn_vw)
    offs = np.concatenate([[0], np.cumsum(splits)]).tolist()
    order = (0, 1, 2, 3, 4, 5, 6, 7, 8, 10, 11, 9)
    pieces, dst = [], 0
    for i in order:
        padded = -(-splits[i] // LANES) * LANES
        pieces.append((offs[i], splits[i], dst, padded))
        dst += padded
    g_gla = 2 * gla_kw + 2 * gla_vw
    g_hg = hgrn_kw + 2 * hgrn_vw
    bounds = np.cumsum([0, g_gla, LANES, 3 * moba_w, g_hg, hgrn_kw]).tolist()
    spans = [(bounds[i], bounds[i + 1]) for i in range(5)]
    groups = (spans[0], spans[1], spans[3], spans[4])
    assert dst == bounds[-1]

    xf = x.reshape(batch * seq, d)
    for l in range(depth):
        y_gla, y_lr, y_hg, y_f, qa, ka, vt = _in_proj(
            xf, attn_norm_g[l], w_in, l, tuple(pieces), groups, (BF16, F32, BF16, F32), spans[2],
            moba_qnorm_g[l], moba_knorm_g[l], batch, seq, moba_heads, moba_hd)

        w2 = jnp.pad(gla_gk_w2[l], ((0, LANES - lowrank), (0, 0)))
        o_a = _lin_call(
            functools.partial(_gla_kernel, gla_heads, gla_dk, gla_dv), "gla",
            batch, seq, gla_heads, gla_dk, gla_dv, (y_gla, y_lr),
            (w2, gla_gk_b[l].reshape(1, gla_kw), jnp.tile(gla_norm_g[l], gla_heads).reshape(1, gla_vw)))
        o_c = _lin_call(
            functools.partial(_hgrn_kernel, hgrn_heads, hgrn_dk, hgrn_dv, l), "hgrn",
            batch, seq, hgrn_heads, hgrn_dk, hgrn_dv, (y_hg, y_f),
            (hgrn_lb_param, jnp.tile(hgrn_norm_g[l], hgrn_heads).reshape(1, hgrn_vw)))

        o_b = _moba_attn(qa, ka, vt, batch, seq, moba_heads, moba_hd)

        wr = _pad_cols(jnp.concatenate([w_router_group[l], w_router_expert[l]], axis=1), LANES)
        x1, hx, route = _out_proj(xf, o_a, o_b, o_c, w_out[l].astype(BF16), ffn_norm_g[l], wr)
        xf = _moe(x1, hx, route, w_exp_gate, w_exp_up, w_exp_down, l)
    return xf.reshape(batch, seq, d)
```

```python
import functools

import numpy as np
import jax
import jax.numpy as jnp
from jax import lax
from jax.experimental import pallas as pl
from jax.experimental.pallas import tpu as pltpu

F32 = jnp.float32
BF16 = jnp.bfloat16

NORM_EPS = 1e-6
GLA_GATE_TAU = 16.0
ROPE_THETA = 500000.0
MOBA_BLOCK = 256
MOBA_TOPK = 3
MOE_GROUPS = 4
MOE_EXPERTS_PER_GROUP = 4
MOE_TOPK = 2

LANES = 128
SUBLANES = 8
VMEM_LIMIT = 56 * 1024 * 1024
LIN_CHUNK = 64
LIN_TILE = 512
ROW_TILE = 512
OUT_PROJ_CHUNK = 128
MOE_TILE = 256
NEG = -1e30
LOG2E = 1.4426950408889634


def _cparams(*sem):
    return pltpu.CompilerParams(dimension_semantics=sem, vmem_limit_bytes=VMEM_LIMIT)


def _sigmoid(x):
    return 1.0 / (1.0 + jnp.exp(-x))


def _dot(a, b):
    return jnp.dot(a, b, preferred_element_type=F32)


def _dot_nt(a, b):
    return lax.dot_general(a, b, (((1,), (1,)), ((), ())), preferred_element_type=F32)


def _dot_tn(a, b):
    return lax.dot_general(a, b, (((0,), (0,)), ((), ())), preferred_element_type=F32)


def _hi_lo(x):
    hi = x.astype(BF16)
    return hi, (x - hi.astype(F32)).astype(BF16)


def _dot3(a, b, nt=False):
    a_hi, a_lo = _hi_lo(a)
    b_hi, b_lo = _hi_lo(b)
    lhs = jnp.concatenate([a_hi, a_hi, a_lo], axis=1)
    if nt:
        return _dot_nt(lhs, jnp.concatenate([b_hi, b_lo, b_hi], axis=1))
    return _dot(lhs, jnp.concatenate([b_hi, b_lo, b_hi], axis=0))


def _in_proj_kernel(pieces, groups, moba_cols, heads, hd, tiles_per_seq,
                    x_ref, g_ref, w_ref, qg_ref, kg_ref, c_ref, sp_ref, sm_ref,
                    ones_ref, hm_ref, pexp_ref, *refs):
    out_refs = refs[:len(groups)]
    qa_ref, ka_ref, vt_ref, wcat_ref, kmt_ref = refs[len(groups):]
    i = pl.program_id(0)

    @pl.when(i == 0)
    def _():
        for src, width, dst, padded in pieces:
            blk = w_ref[src:src + width, :].astype(BF16)
            if padded > width:
                blk = jnp.concatenate([blk, jnp.zeros((padded - width, blk.shape[1]), BF16)], axis=0)
            wcat_ref[dst:dst + padded, :] = blk

    tile = lax.rem(i, tiles_per_seq)

    @pl.when(tile == 0)
    def _():
        kmt_ref[...] = jnp.zeros_like(kmt_ref)

    x = x_ref[...]
    ms = jnp.mean(x * x, axis=-1, keepdims=True)
    h = (x * lax.rsqrt(ms + NORM_EPS) * g_ref[...]).astype(BF16)
    ym = _dot_nt(h, wcat_ref[moba_cols[0]:moba_cols[1], :])
    for (a, b), o_ref in zip(groups, out_refs):
        o_ref[...] = _dot_nt(h, wcat_ref[a:b, :]).astype(o_ref.dtype)

    W = heads * hd
    blocks = x.shape[0] // MOBA_BLOCK
    for blk in range(blocks):
        rows = slice(blk * MOBA_BLOCK, (blk + 1) * MOBA_BLOCK)

        def write_vt(hh, vrows, value, blk=blk):
            vt_ref[0, hh, blk, vrows, :] = value.astype(vt_ref.dtype)

        _moba_prep_block(
            heads, hd, tile * blocks + blk, ym[rows, 0:W], ym[rows, W:2 * W], ym[rows, 2 * W:3 * W],
            (c_ref[rows, :], sp_ref[rows, :], sm_ref[rows, :]), qg_ref[...], kg_ref[...],
            ones_ref, hm_ref, pexp_ref, kmt_ref, qa_ref.at[rows], ka_ref.at[rows], write_vt)


def _in_proj(x, g, w_all_t, layer, pieces, groups, dtypes, moba_cols, qg, kg, batch, seq, heads, hd):
    n, d = x.shape
    tm = ROW_TILE
    cols = w_all_t.shape[1]
    W = heads * hd
    nt = seq // MOBA_BLOCK
    blocks = tm // MOBA_BLOCK
    tiles_per_seq = seq // tm
    tabs, consts = _moba_constants(heads, hd, seq)
    tile_g = lambda v: jnp.tile(v.astype(F32), heads).reshape(1, W)
    row = lambda width: pl.BlockSpec((tm, width), lambda i: (i, 0))
    full = lambda a: pl.BlockSpec(a.shape, lambda i: (0,) * a.ndim)
    wide_sd = jax.ShapeDtypeStruct((n, heads * LANES), BF16)
    out_shape = [jax.ShapeDtypeStruct((n, b - a), dt) for (a, b), dt in zip(groups, dtypes)]
    vt_rows = _moba_vt_rows(hd)
    out_shape += [wide_sd, wide_sd, jax.ShapeDtypeStruct((batch, heads, nt, vt_rows, MOBA_BLOCK), BF16)]
    out_specs = [row(b - a) for (a, b) in groups] + [row(heads * LANES), row(heads * LANES)]
    out_specs.append(pl.BlockSpec((1, heads, blocks, vt_rows, MOBA_BLOCK),
                                  lambda i: (i // tiles_per_seq, 0, i % tiles_per_seq, 0, 0)))
    return pl.pallas_call(
        functools.partial(_in_proj_kernel, pieces, groups, moba_cols, heads, hd, tiles_per_seq),
        grid=(n // tm,),
        in_specs=[row(d),
                  pl.BlockSpec((1, d), lambda i: (0, 0)),
                  pl.BlockSpec((None, cols, d), lambda i: (layer, 0, 0), pipeline_mode=pl.Buffered(1)),
                  full(tile_g(qg)), full(tile_g(kg))]
        + [pl.BlockSpec((tm, LANES), lambda i: (i % tiles_per_seq, 0)) for _ in tabs]
        + [full(a) for a in consts],
        out_specs=out_specs,
        out_shape=out_shape,
        scratch_shapes=[pltpu.VMEM((pieces[-1][2] + pieces[-1][3], d), BF16),
                        pltpu.VMEM((LANES, W), F32)],
        compiler_params=_cparams("arbitrary"),
        name="in_proj",
    )(x, g.reshape(1, d), w_all_t, tile_g(qg), tile_g(kg), *tabs, *consts)


def _lin_constants(heads, dk, dv):
    L = LIN_CHUNK
    nlev = int(np.log2(L))
    K, V = heads * dk, heads * dv
    i = np.arange(L)[:, None]
    t = np.arange(L)[None, :]
    w_rows = [(t <= i), (t > i)]
    masks = []
    for lev in range(nlev):
        b = L >> lev
        half = b // 2
        r = (i // b) * b + half - 1
        w_rows.append((t > np.minimum(i, r)) & (t <= np.maximum(i, r)))
        j = t
        masks.append((i // b == j // b) & (i % b >= half) & (j % b < half))
    masks.append(i == t)
    w_all = np.concatenate(w_rows, axis=0).astype(np.float32)
    m_all = np.stack([np.tile(m, (1, heads)) for m in masks]).astype(np.float32)
    rh = np.repeat(np.arange(heads), L)[:, None]
    bdk = (rh == np.repeat(np.arange(heads), dk)[None, :]).astype(np.float32)
    bdv = (rh == np.repeat(np.arange(heads), dv)[None, :]).astype(np.float32)
    bds = (np.repeat(np.arange(heads), dv)[:, None]
           == np.repeat(np.arange(heads), dk)[None, :]).astype(np.float32)
    ones_v = (np.repeat(np.arange(heads), dv)[:, None]
              == np.repeat(np.arange(heads), dv)[None, :]).astype(np.float32)
    return (jnp.asarray(w_all, BF16), jnp.asarray(m_all, F32), jnp.asarray(bdk, BF16),
            jnp.asarray(bdv, BF16), jnp.asarray(bds, F32), jnp.asarray(ones_v, BF16))


def _lin_groups(probs):
    L = LIN_CHUNK
    items = []
    for p in probs:
        p["lacat"] = jnp.concatenate(_hi_lo(p["la"]), axis=1)
        p["v_bf"] = p["v"].astype(BF16)
        p["chunks"] = [slice(c * L, (c + 1) * L) for c in range(p["q"].shape[0] // L)]
    for p in probs:
        K = p["q"].shape[1]
        for c in p["chunks"]:
            z2 = _dot(p["w_ref"][...], p["lacat"][c])
            items.append((p, c, jnp.exp(z2[:, :K] + z2[:, K:])))
    nlev = probs[0]["m_ref"].shape[0] - 1
    scores = [None] * len(items)
    for lev in range(nlev + 1):
        for i, (p, c, e) in enumerate(items):
            q, k = p["q"], p["k"]
            if lev < nlev:
                el = e[(2 + lev) * L:(3 + lev) * L]
                ql = (q[c] * el).astype(BF16)
                kl = (k[c] * el).astype(BF16)
            else:
                ql = q[c].astype(BF16)
                kl = k[c].astype(BF16)
            kbd = jnp.concatenate([kl] * p["heads"], axis=0) * p["bdk_ref"][...]
            s = _dot_nt(ql, kbd) * p["m_ref"][lev]
            scores[i] = s if scores[i] is None else scores[i] + s
    intra, kvs = [], []
    for i, (p, c, e) in enumerate(items):
        vbd = jnp.concatenate([p["v_bf"][c]] * p["heads"], axis=0) * p["bdv_ref"][...]
        intra.append(_dot(scores[i].astype(BF16), vbd))
        kb = (p["k"][c] * e[L:2 * L]).astype(BF16)
        kvs.append(_dot_tn(p["v_bf"][c], kb) * p["bds_ref"][...])
    results = []
    i = 0
    for p in probs:
        st = p["st_ref"][...]
        outs = []
        for c in p["chunks"]:
            e = items[i][2]
            outs.append(intra[i] + _dot_nt((p["q"][c] * e[0:L]).astype(BF16), st.astype(BF16)))
            st = st * e[L - 1:L] + kvs[i]
            i += 1
        p["st_ref"][...] = st
        o = jnp.concatenate(outs, axis=0)
        ms = _dot((o * o).astype(BF16), p["ones_ref"][...]) * (1.0 / p["dv"])
        gf = p["g"].astype(F32)
        results.append(o * lax.rsqrt(ms + NORM_EPS) * p["gain"] * (gf * _sigmoid(gf)))
    return results


def _lin_kernel(gla_dims, hgrn_dims, layer, xa_ref, lr_ref, xc_ref, f_ref,
                w2_ref, b_ref, gain_a_ref, lbp_ref, gain_c_ref, *refs):
    const_a, const_c = refs[0:6], refs[6:12]
    oa_ref, oc_ref, sta_ref, stc_ref = refs[12:]

    @pl.when(pl.program_id(1) == 0)
    def _():
        sta_ref[...] = jnp.zeros_like(sta_ref)
        stc_ref[...] = jnp.zeros_like(stc_ref)

    def problem(consts, st_ref, heads, dv, **values):
        names = ("w_ref", "m_ref", "bdk_ref", "bdv_ref", "bds_ref", "ones_ref")
        return dict(zip(names, consts), st_ref=st_ref, heads=heads, dv=dv, **values)

    heads, dk, dv = gla_dims
    K, V = heads * dk, heads * dv
    gk = _dot3(lr_ref[...], w2_ref[...]) + b_ref[...]
    gla = problem(
        const_a, sta_ref, heads, dv,
        q=xa_ref[:, 0:K].astype(F32) * (dk ** -0.5), k=xa_ref[:, K:2 * K].astype(F32),
        v=xa_ref[:, 2 * K:2 * K + V], g=xa_ref[:, 2 * K + V:2 * K + 2 * V], gain=gain_a_ref[...],
        la=(jnp.minimum(gk, 0.0) - jnp.log1p(jnp.exp(-jnp.abs(gk)))) * (1.0 / GLA_GATE_TAU))

    heads, dk, dv = hgrn_dims
    K, V = heads * dk, heads * dv
    lbp = lbp_ref[...]
    depth = lbp.shape[0]
    mx = lbp[0:1]
    for r in range(1, depth):
        mx = jnp.maximum(mx, lbp[r:r + 1])
    ex = [jnp.exp(lbp[r:r + 1] - mx) for r in range(depth)]
    den = ex[0]
    for r in range(1, depth):
        den = den + ex[r]
    lb = jnp.zeros_like(den)
    for r in range(1, layer + 1):
        lb = lb + ex[r] / den
    cq = xc_ref[:, 0:K].astype(F32)
    f = f_ref[...]
    hgrn = problem(
        const_c, stc_ref, heads, dv,
        q=cq * _sigmoid(cq), k=(1.0 - lb) * _sigmoid(-f), v=xc_ref[:, K:K + V],
        g=xc_ref[:, K + V:K + 2 * V], gain=gain_c_ref[...],
        la=jnp.log(lb + (1.0 - lb) * _sigmoid(f)))

    o_a, o_c = _lin_groups([gla, hgrn])
    oa_ref[...] = o_a.astype(oa_ref.dtype)
    oc_ref[...] = o_c.astype(oc_ref.dtype)


def _lin_call(batch, seq, layer, gla_dims, hgrn_dims, row_inputs, small_inputs):
    T = LIN_TILE
    nt = seq // T
    consts = _lin_constants(*gla_dims) + _lin_constants(*hgrn_dims)
    vw_a, vw_c = gla_dims[0] * gla_dims[2], hgrn_dims[0] * hgrn_dims[2]
    row = lambda width: pl.BlockSpec((T, width), lambda b, t: (b * nt + t, 0))
    full = lambda a: pl.BlockSpec(a.shape, lambda b, t: (0,) * a.ndim)
    return pl.pallas_call(
        functools.partial(_lin_kernel, gla_dims, hgrn_dims, layer),
        grid=(batch, nt),
        in_specs=[row(a.shape[1]) for a in row_inputs] + [full(a) for a in small_inputs]
        + [full(a) for a in consts],
        out_specs=[row(vw_a), row(vw_c)],
        out_shape=[jax.ShapeDtypeStruct((batch * seq, vw_a), BF16),
                   jax.ShapeDtypeStruct((batch * seq, vw_c), BF16)],
        scratch_shapes=[pltpu.VMEM((vw_a, gla_dims[0] * gla_dims[1]), F32),
                        pltpu.VMEM((vw_c, hgrn_dims[0] * hgrn_dims[1]), F32)],
        compiler_params=_cparams("parallel", "arbitrary"),
        name="lin_attn",
    )(*row_inputs, *small_inputs, *consts)


def _moba_constants(heads, hd, seq):
    W = heads * hd
    nb = LANES // heads
    rot = hd // 4
    half = rot // 2
    inv_freq = ROPE_THETA ** (-(np.arange(0, rot, 2, dtype=np.float64) / rot))
    ang = np.arange(seq, dtype=np.float64)[:, None] * inv_freq[None, :]
    cos, sin = np.cos(ang), np.sin(ang)
    c = np.ones((seq, hd), np.float32)
    sp = np.zeros((seq, hd), np.float32)
    sm = np.zeros((seq, hd), np.float32)
    c[:, :half] = cos
    c[:, half:rot] = cos
    sm[:, :half] = -sin
    sp[:, half:rot] = sin
    reps = LANES // hd
    tabs = [jnp.asarray(np.tile(a, (1, reps)), F32) for a in (c, sp, sm)]
    hl = np.repeat(np.arange(heads), hd)
    ones_h = (hl[:, None] == hl[None, :]).astype(np.float32)
    hm = (np.arange(heads)[:, None] == hl[None, :]).astype(np.float32)
    pb = np.zeros((LANES, heads * LANES), np.float32)
    gl = np.arange(LANES)
    pb[gl, (gl % heads) * LANES + hd + gl // heads] = 1.0
    return tabs, (jnp.asarray(ones_h, BF16), jnp.asarray(hm, F32), jnp.asarray(pb, BF16))


def _moba_prep_block(heads, hd, t, xq, xk, xv, tabs, qg, kg, ones_ref, hm_ref, pexp_ref, kmt_ref,
                     qa_ref, ka_ref, write_vt):
    W = heads * hd
    half = hd // 8
    reps = W // LANES
    c, sp, sm = (jnp.concatenate([a] * reps, axis=1) for a in tabs)

    def norm_rot(x, gain):
        ms = _dot((x * x).astype(BF16), ones_ref[...]) * (1.0 / hd)
        y = x * lax.rsqrt(ms + NORM_EPS) * gain
        return y * c + pltpu.roll(y, half, 1) * sp + pltpu.roll(y, W - half, 1) * sm

    q = norm_rot(xq, qg)
    k = norm_rot(xk, kg)

    gate = _dot3(q, kmt_ref[...], nt=True)
    lane = lax.broadcasted_iota(jnp.int32, gate.shape, 1)
    blk = lax.shift_right_logical(lane, int(np.log2(heads)))
    valid = blk < t
    gate = jnp.where(valid, gate, -jnp.inf)
    rank = jnp.zeros(gate.shape, F32)
    nslots = LANES // heads
    for s in range(1, nslots):
        other = pltpu.roll(gate, s * heads, 1)
        ahead = (other > gate) | ((other == gate) & (blk >= s))
        rank = rank + ahead.astype(F32)
    sel = valid & (rank < float(MOBA_TOPK))
    bias = jnp.where(sel | (blk == t), 0.0, NEG)

    bias_slab = _dot(bias.astype(BF16), pexp_ref[...])
    qs = q * (hd ** -0.5 * LOG2E)
    lane1 = lax.broadcasted_iota(jnp.int32, (q.shape[0], LANES), 1)
    onehot = jnp.where(lane1 == hd + t, 1.0, 0.0)
    per_vreg = LANES // hd
    for h in range(heads):
        src = slice((h // per_vreg) * LANES, (h // per_vreg + 1) * LANES)
        dst = slice(h * LANES, (h + 1) * LANES)
        shift = (LANES - (h % per_vreg) * hd) % LANES
        move = (lambda a: a) if shift == 0 else (lambda a: pltpu.roll(a, shift, 1))
        qa_ref[:, dst] = jnp.where(lane1 < hd, move(qs[:, src]), bias_slab[:, dst]).astype(qa_ref.dtype)
        ka_ref[:, dst] = jnp.where(lane1 < hd, move(k[:, src]), onehot).astype(ka_ref.dtype)

    kmean = jnp.mean(k, axis=0, keepdims=True)
    row0 = pl.multiple_of(t * heads, heads)
    kmt_ref[pl.ds(row0, heads), :] = kmean * hm_ref[...]

    vt = xv.T
    vt_rows = _moba_vt_rows(hd)
    ones = jnp.ones((vt_rows - hd, vt.shape[1]), F32)
    for h in range(heads):
        write_vt(h, slice(0, hd), vt[h * hd:(h + 1) * hd])
        write_vt(h, slice(hd, vt_rows), ones)


def _moba_vt_rows(hd):
    return hd + 2 * SUBLANES


def _moba_attn_kernel(hd, qa_ref, ka_ref, vt_ref, o_ref, acc_ref, sa_ref, sb_ref):
    t = pl.program_id(1)
    tq = qa_ref.shape[0]
    blk = MOBA_BLOCK
    nh = vt_ref.shape[1]

    def scores_to(dst_ref, n, hh):
        cols = slice(hh * LANES, (hh + 1) * LANES)
        rows = pl.ds(pl.multiple_of(n * blk, blk), blk)
        dst_ref[hh] = _dot_nt(ka_ref[rows, cols], qa_ref[:, cols])

    def update(s, m, n, hh):
        m_new = jnp.maximum(m, jnp.max(s, axis=0, keepdims=True))
        alpha = jnp.exp2(m - m_new)
        p = jnp.exp2(s - m_new).astype(BF16)
        acc_ref[hh] = alpha * acc_ref[hh] + _dot(vt_ref[0, hh, n], p)
        return m_new

    def step(cur_ref, nxt_ref, n, ms):
        new = []
        for hh in range(nh):
            scores_to(nxt_ref, n + 1, hh)
            new.append(update(cur_ref[hh], ms[hh], n, hh))
        return tuple(new)

    acc_ref[...] = jnp.zeros_like(acc_ref)
    for hh in range(nh):
        scores_to(sa_ref, 0, hh)

    def pair(i, ms):
        ms = step(sa_ref, sb_ref, 2 * i, ms)
        return step(sb_ref, sa_ref, 2 * i + 1, ms)

    m0 = jnp.full((1, tq), 2 * NEG, F32)
    ms = lax.fori_loop(0, t // 2, pair, (m0,) * nh)

    def odd_block(ms):
        new = []
        for hh in range(nh):
            new.append(update(sa_ref[hh], ms[hh], t - 1, hh))
            scores_to(sa_ref, t, hh)
        return tuple(new)

    ms = lax.cond(t % 2 == 1, odd_block, lambda ms: ms, ms)

    key = lax.broadcasted_iota(jnp.int32, (blk, tq), 0)
    qry = lax.broadcasted_iota(jnp.int32, (blk, tq), 1)
    outs = []
    for hh in range(nh):
        s = jnp.where(key <= qry, sa_ref[hh], NEG)
        update(s, ms[hh], t, hh)
        a = acc_ref[hh]
        outs.append(a[0:hd] / a[hd:hd + 1])
    o_ref[...] = jnp.concatenate(outs, axis=0).T.astype(o_ref.dtype)


def _moba_attn(qa, ka, vt, batch, seq, heads, hd):
    nt = seq // MOBA_BLOCK
    vt_rows = vt.shape[3]
    return pl.pallas_call(
        functools.partial(_moba_attn_kernel, hd),
        grid=(batch, nt),
        in_specs=[pl.BlockSpec((MOBA_BLOCK, heads * LANES), lambda b, t: (b * nt + t, 0)),
                  pl.BlockSpec((seq, heads * LANES), lambda b, t: (b, 0)),
                  pl.BlockSpec((1, heads, nt, vt_rows, MOBA_BLOCK), lambda b, t: (b, 0, 0, 0, 0))],
        out_specs=pl.BlockSpec((MOBA_BLOCK, heads * hd), lambda b, t: (b * nt + t, 0)),
        out_shape=jax.ShapeDtypeStruct((batch * seq, heads * hd), BF16),
        scratch_shapes=[pltpu.VMEM((heads, vt_rows, MOBA_BLOCK), F32),
                        pltpu.VMEM((heads, MOBA_BLOCK, MOBA_BLOCK), F32),
                        pltpu.VMEM((heads, MOBA_BLOCK, MOBA_BLOCK), F32)],
        compiler_params=_cparams("parallel", "arbitrary"),
        name="moba_attn",
    )(qa, ka, vt)


def _out_proj_kernel(widths, x_ref, oa_ref, ob_ref, oc_ref, w_ref, g_ref, wr_ref,
                     x1_ref, hx_ref, route_ref):
    rows = x_ref.shape[0]
    hw = x_ref.shape[1]
    chunks = [slice(r, r + OUT_PROJ_CHUNK) for r in range(0, rows, OUT_PROJ_CHUNK)]
    accs = []
    for c in chunks:
        acc = x_ref[c, :]
        r0 = 0
        for o_ref, wd in zip((oa_ref, ob_ref, oc_ref), widths):
            acc = acc + _dot(o_ref[c, :], w_ref[r0:r0 + wd, :])
            r0 += wd
        x1_ref[c, :] = acc
        accs.append(acc)
    wr_hi, wr_lo = _hi_lo(wr_ref[...])
    wr_cat = jnp.concatenate([wr_hi, wr_lo, wr_hi], axis=0)
    all_logits = []
    for c, acc in zip(chunks, accs):
        ms = jnp.mean(acc * acc, axis=-1, keepdims=True)
        h2 = acc * lax.rsqrt(ms + NORM_EPS) * g_ref[...]
        hx_ref[c, 0:hw] = h2
        h_hi, h_lo = _hi_lo(h2)
        all_logits.append(_dot(jnp.concatenate([h_hi, h_hi, h_lo], axis=1), wr_cat))
    for c, logits in zip(chunks, all_logits):
        route = _route(logits)
        hx_ref[c, hw:hw + LANES] = route
        route_ref[c, :] = route


def _route(logits):
    G, E = MOE_GROUPS, MOE_EXPERTS_PER_GROUP
    lane = lax.broadcasted_iota(jnp.int32, logits.shape, 1).astype(F32)
    big = float(LANES)
    is_g = lane < G
    lg = jnp.where(is_g, logits, -jnp.inf)
    mg = jnp.max(lg, axis=-1, keepdims=True)
    gate_group = 1.0 / jnp.sum(jnp.exp(lg - mg), axis=-1, keepdims=True)
    g_sel = jnp.min(jnp.where(lg == mg, lane, big), axis=-1, keepdims=True)
    lo = G + E * g_sel
    in_grp = (lane >= lo) & (lane < lo + E)
    v = jnp.where(in_grp, logits, -jnp.inf)
    v1 = jnp.max(v, axis=-1, keepdims=True)
    i1 = jnp.min(jnp.where(v == v1, lane, big), axis=-1, keepdims=True)
    v = jnp.where(lane == i1, -jnp.inf, v)
    v2 = jnp.max(v, axis=-1, keepdims=True)
    i2 = jnp.min(jnp.where(v == v2, lane, big), axis=-1, keepdims=True)
    e2 = jnp.exp(v2 - v1)
    w1 = gate_group / (1.0 + e2)
    w2 = gate_group * e2 / (1.0 + e2)
    first = i1 < i2
    a = jnp.minimum(i1, i2) - lo
    b = jnp.maximum(i1, i2) - lo
    w_lo = jnp.where(first, w1, w2)
    w_hi = jnp.where(first, w2, w1)
    pos = jnp.zeros_like(a)
    swap = a < 0.0
    for p, (ea, eb) in enumerate(_moe_pair_order()):
        hit = (a == float(min(ea, eb))) & (b == float(max(ea, eb)))
        pos = jnp.where(hit, float(p), pos)
        if ea > eb:
            swap = swap | hit
    cls = g_sel * float(E * (E - 1) // 2) + pos
    route = jnp.where(lane == 0.0, cls, 0.0)
    route = jnp.where(lane == 1.0, jnp.where(swap, w_hi, w_lo), route)
    return jnp.where(lane == 2.0, jnp.where(swap, w_lo, w_hi), route)


def _out_proj(x, oa, ob, oc, w, g, wr):
    n, d = x.shape
    tm = ROW_TILE
    widths = (oa.shape[1], ob.shape[1], oc.shape[1])
    row = lambda width: pl.BlockSpec((tm, width), lambda i: (i, 0))
    full = lambda a: pl.BlockSpec(a.shape, lambda i: (0, 0))
    g2 = g.reshape(1, d)
    hxw = d + LANES
    return pl.pallas_call(
        functools.partial(_out_proj_kernel, widths),
        grid=(n // tm,),
        in_specs=[row(d), row(widths[0]), row(widths[1]), row(widths[2]), full(w), full(g2), full(wr)],
        out_specs=[row(d), row(hxw), row(LANES)],
        out_shape=[jax.ShapeDtypeStruct((n, d), F32), jax.ShapeDtypeStruct((n, hxw), F32),
                   jax.ShapeDtypeStruct((n, LANES), F32)],
        compiler_params=_cparams("parallel"),
        name="out_proj",
    )(x, oa, ob, oc, w, g2, wr)


@functools.lru_cache(maxsize=None)
def _moe_pair_order():
    E = MOE_EXPERTS_PER_GROUP
    todo = {(a, b) for a in range(E) for b in range(a + 1, E)}

    def extend(seq, todo):
        if not todo:
            return seq
        for pr in sorted(todo):
            for cand in (pr, pr[::-1]):
                if not seq or cand[0] == seq[-1][0] or cand[1] == seq[-1][1]:
                    out = extend(seq + [cand], todo - {pr})
                    if out:
                        return out
        return None

    return tuple(extend([], todo))


def _moe_classes():
    G, E = MOE_GROUPS, MOE_EXPERTS_PER_GROUP
    ea = [g * E + a for g in range(G) for a, _ in _moe_pair_order()]
    eb = [g * E + b for g in range(G) for _, b in _moe_pair_order()]
    return ea, eb


def _moe_plan_kernel(elo, ehi, cls_ref, dest_ref, tab_ref):
    R = MOE_TILE
    cls = cls_ref[...]
    rows = cls.shape[0]
    ri = lax.broadcasted_iota(jnp.int32, (LANES, LANES), 0)
    ci = lax.broadcasted_iota(jnp.int32, (LANES, LANES), 1)
    before_lane = (ri < ci).astype(BF16)
    all_lane = jnp.ones((LANES, LANES), BF16)
    rr = lax.broadcasted_iota(jnp.int32, (rows, rows), 0)
    rc = lax.broadcasted_iota(jnp.int32, (rows, rows), 1)
    before_row = (rc < rr).astype(BF16)
    all_row = jnp.ones((rows, rows), BF16)
    tile_row0 = lax.broadcasted_iota(jnp.int32, (8, LANES), 1).astype(F32) * float(R)
    off = jnp.zeros((rows, LANES), F32)
    dest = jnp.zeros((rows, LANES), F32)
    tile_cls = jnp.zeros((8, LANES), F32)
    for k in range(len(elo)):
        mask = (cls == k).astype(BF16)
        row_tot = _dot(mask, all_lane)
        rank = _dot(mask, before_lane) + _dot(before_row, row_tot.astype(BF16))
        count = _dot(all_row, row_tot.astype(BF16))
        dest = dest + mask.astype(F32) * (off + rank)
        off = off + jnp.floor((count + float(R - 1)) * (1.0 / R)) * float(R)
        tile_cls = tile_cls + jnp.where(tile_row0 >= off[0:8], 1.0, 0.0)
    dest_ref[...] = dest.astype(jnp.int32)
    tile_cls = jnp.minimum(tile_cls, float(len(elo) - 1))
    t_lo = jnp.zeros((8, LANES), F32)
    t_hi = jnp.zeros((8, LANES), F32)
    for k in range(len(elo)):
        t_lo = jnp.where(tile_cls == k, float(elo[k]), t_lo)
        t_hi = jnp.where(tile_cls == k, float(ehi[k]), t_hi)
    sub = lax.broadcasted_iota(jnp.int32, (8, LANES), 0)
    tab = jnp.where(sub == 0, t_lo, jnp.where(sub == 1, t_hi, off[0:8] * (1.0 / R)))
    tab_ref[...] = tab.astype(jnp.int32)


def _moe_plan(cls):
    elo, ehi = _moe_classes()
    rows = cls.shape[0]
    return pl.pallas_call(
        functools.partial(_moe_plan_kernel, elo, ehi),
        out_shape=[jax.ShapeDtypeStruct((rows, LANES), jnp.int32),
                   jax.ShapeDtypeStruct((8, LANES), jnp.int32)],
        name="moe_plan",
    )(cls)


def _row_copies(n, make):
    def start(g, c):
        base = pl.multiple_of(g * SUBLANES, SUBLANES)
        for s in range(SUBLANES):
            make(base + s).start()
        return c

    def wait(g, c):
        base = pl.multiple_of(g * SUBLANES, SUBLANES)
        for s in range(SUBLANES):
            make(base + s).wait()
        return c

    lax.fori_loop(0, n // SUBLANES, start, 0)
    lax.fori_loop(0, n // SUBLANES, wait, 0)


def _moe_dispatch_kernel(dest_ref, hx_ref, xs_in_ref, xs_ref, sem):
    del xs_in_ref
    _row_copies(hx_ref.shape[0], lambda r: pltpu.make_async_copy(
        hx_ref.at[pl.ds(r, 1)], xs_ref.at[pl.ds(dest_ref[r], 1)], sem))


def _moe_dispatch(dest, hx, n_rows):
    n, w = hx.shape
    tm = ROW_TILE
    return pl.pallas_call(
        _moe_dispatch_kernel,
        grid=(n // tm,),
        in_specs=[pl.BlockSpec((tm,), lambda i: (i,), memory_space=pltpu.SMEM),
                  pl.BlockSpec((tm, w), lambda i: (i, 0)),
                  pl.BlockSpec(memory_space=pl.ANY)],
        out_specs=pl.BlockSpec(memory_space=pl.ANY),
        out_shape=jax.ShapeDtypeStruct((n_rows, w), hx.dtype),
        scratch_shapes=[pltpu.SemaphoreType.DMA],
        input_output_aliases={2: 0},
        compiler_params=_cparams("arbitrary"),
        name="moe_dispatch",
    )(dest, hx, jnp.zeros((n_rows, w), hx.dtype))


def _moe_experts_kernel(elo_ref, ehi_ref, nt_ref, xs_ref, wg0, wu0, wd0, wg1, wu1, wd1, y_ref,
                        wgu_ref, wdn_ref):
    j = pl.program_id(0)
    hw = xs_ref.shape[1] - LANES
    jp = jnp.maximum(j - 1, 0)
    fresh = (j == 0) | (elo_ref[j] != elo_ref[jp]) | (ehi_ref[j] != ehi_ref[jp])

    @pl.when((j < nt_ref[0]) & fresh)
    def _():
        for slot, (wg, wu, wd) in enumerate(((wg0, wu0, wd0), (wg1, wu1, wd1))):
            wgu_ref[slot, 0] = wg[0].astype(BF16)
            wgu_ref[slot, 1] = wu[0].astype(BF16)
            wdn_ref[slot] = wd[0].astype(BF16)

    @pl.when(j < nt_ref[0])
    def _():
        x = xs_ref[:, 0:hw].astype(BF16)
        route = xs_ref[:, hw:hw + LANES]
        lane = lax.broadcasted_iota(jnp.int32, route.shape, 1)
        y = None
        for slot in range(2):
            wt = jnp.sum(jnp.where(lane == 1 + slot, route, 0.0), axis=-1, keepdims=True)
            a = _dot(x, wgu_ref[slot, 0])
            u = _dot(x, wgu_ref[slot, 1])
            he = (a * _sigmoid(a)) * u * wt
            part = _dot(he.astype(BF16), wdn_ref[slot])
            y = part if y is None else y + part
        y_ref[...] = y

    @pl.when(j >= nt_ref[0])
    def _():
        y_ref[...] = jnp.zeros_like(y_ref)


def _moe_experts(tab, xs, wg, wu, wd, layer):
    n_rows, w = xs.shape
    _, ne, d, ff = wg.shape
    R = MOE_TILE
    last = lambda j, nt: jnp.minimum(j, nt[0] - 1)
    wspec = lambda shape, which: pl.BlockSpec(
        (None,) + shape, lambda j, elo, ehi, nt: (layer, (elo, ehi)[which][last(j, nt)], 0, 0))
    grid_spec = pltpu.PrefetchScalarGridSpec(
        num_scalar_prefetch=3,
        grid=(n_rows // R,),
        in_specs=[pl.BlockSpec((R, w), lambda j, elo, ehi, nt: (last(j, nt), 0)),
                  wspec((1, d, ff), 0), wspec((1, d, ff), 0), wspec((1, ff, d), 0),
                  wspec((1, d, ff), 1), wspec((1, d, ff), 1), wspec((1, ff, d), 1)],
        out_specs=pl.BlockSpec((R, d), lambda j, elo, ehi, nt: (j, 0)),
        scratch_shapes=[pltpu.VMEM((2, 2, d, ff), BF16), pltpu.VMEM((2, ff, d), BF16)],
    )
    return pl.pallas_call(
        _moe_experts_kernel,
        grid_spec=grid_spec,
        out_shape=jax.ShapeDtypeStruct((n_rows, d), F32),
        compiler_params=_cparams("arbitrary"),
        name="moe_experts",
    )(tab[0], tab[1], tab[2, 0:1], xs, wg, wu, wd, wg, wu, wd)


def _moe_combine_kernel(dest_ref, x1_ref, y_ref, o_ref, buf_ref, sem):
    _row_copies(buf_ref.shape[0], lambda r: pltpu.make_async_copy(
        y_ref.at[pl.ds(dest_ref[r], 1)], buf_ref.at[pl.ds(r, 1)], sem))
    o_ref[...] = x1_ref[...] + buf_ref[...]


def _moe_combine(dest, x1, y):
    n, d = x1.shape
    tm = ROW_TILE
    return pl.pallas_call(
        _moe_combine_kernel,
        grid=(n // tm,),
        in_specs=[pl.BlockSpec((tm,), lambda i: (i,), memory_space=pltpu.SMEM),
                  pl.BlockSpec((tm, d), lambda i: (i, 0)),
                  pl.BlockSpec(memory_space=pl.ANY)],
        out_specs=pl.BlockSpec((tm, d), lambda i: (i, 0)),
        out_shape=jax.ShapeDtypeStruct((n, d), F32),
        scratch_shapes=[pltpu.VMEM((tm, d), F32), pltpu.SemaphoreType.DMA],
        compiler_params=_cparams("arbitrary"),
        name="moe_combine",
    )(dest, x1, y)


def _moe(x1, hx, route, wg, wu, wd, layer):
    n, d = x1.shape
    elo, _ = _moe_classes()
    n_rows = n + len(elo) * MOE_TILE
    assert n % LANES == 0 and n_rows // MOE_TILE <= LANES
    cls = route[:, 0].astype(jnp.int32).reshape(n // LANES, LANES)
    dest, tab = _moe_plan(cls)
    dest = dest.reshape(n)
    xs = _moe_dispatch(dest, hx, n_rows)
    y = _moe_experts(tab, xs, wg, wu, wd, layer)
    return _moe_combine(dest, x1, y)


def _pad_cols(w, width):
    return jnp.pad(w, ((0, 0), (0, width - w.shape[1])))


def kernel(x, attn_norm_g, w_in, gla_gk_w2, gla_gk_b, gla_norm_g, moba_qnorm_g, moba_knorm_g,
           hgrn_lb_param, hgrn_norm_g, w_out, ffn_norm_g, w_router_group, w_router_expert,
           w_exp_gate, w_exp_up, w_exp_down):
    batch, seq, d = x.shape
    depth = w_in.shape[0]
    lowrank, gla_kw = gla_gk_w2.shape[1:]
    gla_dv = gla_norm_g.shape[1]
    moba_hd = moba_qnorm_g.shape[1]
    hgrn_kw = hgrn_lb_param.shape[1]
    hgrn_dv = hgrn_norm_g.shape[1]
    mix_w = w_out.shape[1]
    hgrn_heads = 4
    gla_heads = 4
    gla_vw = gla_heads * gla_dv
    hgrn_vw = hgrn_heads * hgrn_dv
    moba_w = mix_w - gla_vw - hgrn_vw
    moba_heads = moba_w // moba_hd
    gla_dk = gla_kw // gla_heads
    hgrn_dk = hgrn_kw // hgrn_heads

    splits = (gla_kw, gla_kw, gla_vw, gla_vw, lowrank, moba_w, moba_w, moba_w,
              hgrn_kw, hgrn_kw, hgrn_vw, hgrn_vw)
    offs = np.concatenate([[0], np.cumsum(splits)]).tolist()
    order = (0, 1, 2, 3, 4, 5, 6, 7, 8, 10, 11, 9)
    pieces, dst = [], 0
    for i in order:
        padded = -(-splits[i] // LANES) * LANES
        pieces.append((offs[i], splits[i], dst, padded))
        dst += padded
    g_gla = 2 * gla_kw + 2 * gla_vw
    g_hg = hgrn_kw + 2 * hgrn_vw
    bounds = np.cumsum([0, g_gla, LANES, 3 * moba_w, g_hg, hgrn_kw]).tolist()
    spans = [(bounds[i], bounds[i + 1]) for i in range(5)]
    groups = (spans[0], spans[1], spans[3], spans[4])
    assert dst == bounds[-1]

    w_in_t = jnp.swapaxes(w_in, 1, 2)
    xf = x.reshape(batch * seq, d)
    for l in range(depth):
        y_gla, y_lr, y_hg, y_f, qa, ka, vt = _in_proj(
            xf, attn_norm_g[l], w_in_t, l, tuple(pieces), groups, (BF16, F32, BF16, F32), spans[2],
            moba_qnorm_g[l], moba_knorm_g[l], batch, seq, moba_heads, moba_hd)

        w2 = jnp.pad(gla_gk_w2[l], ((0, LANES - lowrank), (0, 0)))
        o_a, o_c = _lin_call(
            batch, seq, l, (gla_heads, gla_dk, gla_dv), (hgrn_heads, hgrn_dk, hgrn_dv),
            (y_gla, y_lr, y_hg, y_f),
            (w2, gla_gk_b[l].reshape(1, gla_kw), jnp.tile(gla_norm_g[l], gla_heads).reshape(1, gla_vw),
             hgrn_lb_param, jnp.tile(hgrn_norm_g[l], hgrn_heads).reshape(1, hgrn_vw)))

        o_b = _moba_attn(qa, ka, vt, batch, seq, moba_heads, moba_hd)

        wr = _pad_cols(jnp.concatenate([w_router_group[l], w_router_expert[l]], axis=1), LANES)
        x1, hx, route = _out_proj(xf, o_a, o_b, o_c, w_out[l].astype(BF16), ffn_norm_g[l], wr)
        xf = _moe(x1, hx, route, w_exp_gate, w_exp_up, w_exp_down, l)
    return xf.reshape(batch, seq, d)
```

```python
import functools

import numpy as np
import jax
import jax.numpy as jnp
from jax import lax
from jax.experimental import pallas as pl
from jax.experimental.pallas import tpu as pltpu

F32 = jnp.float32
BF16 = jnp.bfloat16

NORM_EPS = 1e-6
GLA_GATE_TAU = 16.0
ROPE_THETA = 500000.0
MOBA_BLOCK = 256
MOBA_TOPK = 3
MOE_GROUPS = 4
MOE_EXPERTS_PER_GROUP = 4
MOE_TOPK = 2

LANES = 128
SUBLANES = 8
VMEM_LIMIT = 56 * 1024 * 1024
LIN_CHUNK = 64
LIN_TILE = 512
ROW_TILE = 512
OUT_PROJ_CHUNK = 128
MOE_TILE = 256
NEG = -1e30
LOG2E = 1.4426950408889634


def _cparams(*sem):
    return pltpu.CompilerParams(dimension_semantics=sem, vmem_limit_bytes=VMEM_LIMIT)


def _sigmoid(x):
    return 1.0 / (1.0 + jnp.exp(-x))


def _dot(a, b):
    return jnp.dot(a, b, preferred_element_type=F32)


def _dot_nt(a, b):
    return lax.dot_general(a, b, (((1,), (1,)), ((), ())), preferred_element_type=F32)


def _dot_tn(a, b):
    return lax.dot_general(a, b, (((0,), (0,)), ((), ())), preferred_element_type=F32)


def _hi_lo(x):
    hi = x.astype(BF16)
    return hi, (x - hi.astype(F32)).astype(BF16)


def _dot3(a, b, nt=False):
    a_hi, a_lo = _hi_lo(a)
    b_hi, b_lo = _hi_lo(b)
    lhs = jnp.concatenate([a_hi, a_hi, a_lo], axis=1)
    if nt:
        return _dot_nt(lhs, jnp.concatenate([b_hi, b_lo, b_hi], axis=1))
    return _dot(lhs, jnp.concatenate([b_hi, b_lo, b_hi], axis=0))


def _in_proj_kernel(pieces, groups, moba_cols, heads, hd, tiles_per_seq,
                    x_ref, g_ref, w_ref, qg_ref, kg_ref, c_ref, sp_ref, sm_ref,
                    ones_ref, hm_ref, pexp_ref, *refs):
    out_refs = refs[:len(groups)]
    qa_ref, ka_ref, vt_ref, wcat_ref, kmt_ref = refs[len(groups):]
    i = pl.program_id(0)

    @pl.when(i == 0)
    def _():
        for src, width, dst, padded in pieces:
            blk = w_ref[src:src + width, :].astype(BF16)
            if padded > width:
                blk = jnp.concatenate([blk, jnp.zeros((padded - width, blk.shape[1]), BF16)], axis=0)
            wcat_ref[dst:dst + padded, :] = blk

    tile = lax.rem(i, tiles_per_seq)

    @pl.when(tile == 0)
    def _():
        kmt_ref[...] = jnp.zeros_like(kmt_ref)

    x = x_ref[...]
    ms = jnp.mean(x * x, axis=-1, keepdims=True)
    h = (x * lax.rsqrt(ms + NORM_EPS) * g_ref[...]).astype(BF16)
    ym = _dot_nt(h, wcat_ref[moba_cols[0]:moba_cols[1], :])
    for (a, b), o_ref in zip(groups, out_refs):
        o_ref[...] = _dot_nt(h, wcat_ref[a:b, :]).astype(o_ref.dtype)

    W = heads * hd
    blocks = x.shape[0] // MOBA_BLOCK
    for blk in range(blocks):
        rows = slice(blk * MOBA_BLOCK, (blk + 1) * MOBA_BLOCK)

        def write_vt(hh, vrows, value, blk=blk):
            vt_ref[0, hh, blk, vrows, :] = value.astype(vt_ref.dtype)

        _moba_prep_block(
            heads, hd, tile * blocks + blk, ym[rows, 0:W], ym[rows, W:2 * W], ym[rows, 2 * W:3 * W],
            (c_ref[rows, :], sp_ref[rows, :], sm_ref[rows, :]), qg_ref[...], kg_ref[...],
            ones_ref, hm_ref, pexp_ref, kmt_ref, qa_ref.at[rows], ka_ref.at[rows], write_vt)


def _in_proj(x, g, w_all_t, layer, pieces, groups, dtypes, moba_cols, qg, kg, batch, seq, heads, hd):
    n, d = x.shape
    tm = ROW_TILE
    cols = w_all_t.shape[1]
    W = heads * hd
    nt = seq // MOBA_BLOCK
    blocks = tm // MOBA_BLOCK
    tiles_per_seq = seq // tm
    tabs, consts = _moba_constants(heads, hd, seq)
    tile_g = lambda v: jnp.tile(v.astype(F32), heads).reshape(1, W)
    row = lambda width: pl.BlockSpec((tm, width), lambda i: (i, 0))
    full = lambda a: pl.BlockSpec(a.shape, lambda i: (0,) * a.ndim)
    wide_sd = jax.ShapeDtypeStruct((n, heads * LANES), BF16)
    out_shape = [jax.ShapeDtypeStruct((n, b - a), dt) for (a, b), dt in zip(groups, dtypes)]
    vt_rows = _moba_vt_rows(hd)
    out_shape += [wide_sd, wide_sd, jax.ShapeDtypeStruct((batch, heads, nt, vt_rows, MOBA_BLOCK), BF16)]
    out_specs = [row(b - a) for (a, b) in groups] + [row(heads * LANES), row(heads * LANES)]
    out_specs.append(pl.BlockSpec((1, heads, blocks, vt_rows, MOBA_BLOCK),
                                  lambda i: (i // tiles_per_seq, 0, i % tiles_per_seq, 0, 0)))
    return pl.pallas_call(
        functools.partial(_in_proj_kernel, pieces, groups, moba_cols, heads, hd, tiles_per_seq),
        grid=(n // tm,),
        in_specs=[row(d),
                  pl.BlockSpec((1, d), lambda i: (0, 0)),
                  pl.BlockSpec((None, cols, d), lambda i: (layer, 0, 0), pipeline_mode=pl.Buffered(1)),
                  full(tile_g(qg)), full(tile_g(kg))]
        + [pl.BlockSpec((tm, LANES), lambda i: (i % tiles_per_seq, 0)) for _ in tabs]
        + [full(a) for a in consts],
        out_specs=out_specs,
        out_shape=out_shape,
        scratch_shapes=[pltpu.VMEM((pieces[-1][2] + pieces[-1][3], d), BF16),
                        pltpu.VMEM((LANES, W), F32)],
        compiler_params=_cparams("arbitrary"),
        name="in_proj",
    )(x, g.reshape(1, d), w_all_t, tile_g(qg), tile_g(kg), *tabs, *consts)


def _lin_constants(heads, dk, dv):
    L = LIN_CHUNK
    nlev = int(np.log2(L))
    K, V = heads * dk, heads * dv
    i = np.arange(L)[:, None]
    t = np.arange(L)[None, :]
    w_rows = [(t <= i), (t > i)]
    masks = []
    for lev in range(nlev):
        b = L >> lev
        half = b // 2
        r = (i // b) * b + half - 1
        w_rows.append((t > np.minimum(i, r)) & (t <= np.maximum(i, r)))
        j = t
        masks.append((i // b == j // b) & (i % b >= half) & (j % b < half))
    masks.append(i == t)
    w_all = np.concatenate(w_rows, axis=0).astype(np.float32)
    m_all = np.stack([np.tile(m, (1, heads)) for m in masks]).astype(np.float32)
    rh = np.repeat(np.arange(heads), L)[:, None]
    bdk = (rh == np.repeat(np.arange(heads), dk)[None, :]).astype(np.float32)
    bdv = (rh == np.repeat(np.arange(heads), dv)[None, :]).astype(np.float32)
    bds = (np.repeat(np.arange(heads), dv)[:, None]
           == np.repeat(np.arange(heads), dk)[None, :]).astype(np.float32)
    ones_v = (np.repeat(np.arange(heads), dv)[:, None]
              == np.repeat(np.arange(heads), dv)[None, :]).astype(np.float32)
    return (jnp.asarray(w_all, BF16), jnp.asarray(m_all, F32), jnp.asarray(bdk, BF16),
            jnp.asarray(bdv, BF16), jnp.asarray(bds, F32), jnp.asarray(ones_v, BF16))


def _lin_groups(probs):
    L = LIN_CHUNK
    items = []
    for p in probs:
        p["lacat"] = jnp.concatenate(_hi_lo(p["la"]), axis=1)
        p["v_bf"] = p["v"].astype(BF16)
        p["chunks"] = [slice(c * L, (c + 1) * L) for c in range(p["q"].shape[0] // L)]
    for p in probs:
        K = p["q"].shape[1]
        for c in p["chunks"]:
            z2 = _dot(p["w_ref"][...], p["lacat"][c])
            items.append((p, c, jnp.exp(z2[:, :K] + z2[:, K:])))
    nlev = probs[0]["m_ref"].shape[0] - 1
    scores = [None] * len(items)
    for lev in range(nlev + 1):
        for i, (p, c, e) in enumerate(items):
            q, k = p["q"], p["k"]
            if lev < nlev:
                el = e[(2 + lev) * L:(3 + lev) * L]
                ql = (q[c] * el).astype(BF16)
                kl = (k[c] * el).astype(BF16)
            else:
                ql = q[c].astype(BF16)
                kl = k[c].astype(BF16)
            kbd = jnp.concatenate([kl] * p["heads"], axis=0) * p["bdk_ref"][...]
            s = _dot_nt(ql, kbd) * p["m_ref"][lev]
            scores[i] = s if scores[i] is None else scores[i] + s
    intra, kvs = [], []
    for i, (p, c, e) in enumerate(items):
        vbd = jnp.concatenate([p["v_bf"][c]] * p["heads"], axis=0) * p["bdv_ref"][...]
        intra.append(_dot(scores[i].astype(BF16), vbd))
        kb = (p["k"][c] * e[L:2 * L]).astype(BF16)
        kvs.append(_dot_tn(p["v_bf"][c], kb) * p["bds_ref"][...])
    results = []
    i = 0
    for p in probs:
        st = p["st_ref"][...]
        outs = []
        for c in p["chunks"]:
            e = items[i][2]
            outs.append(intra[i] + _dot_nt((p["q"][c] * e[0:L]).astype(BF16), st.astype(BF16)))
            st = st * e[L - 1:L] + kvs[i]
            i += 1
        p["st_ref"][...] = st
        o = jnp.concatenate(outs, axis=0)
        ms = _dot((o * o).astype(BF16), p["ones_ref"][...]) * (1.0 / p["dv"])
        gf = p["g"].astype(F32)
        results.append(o * lax.rsqrt(ms + NORM_EPS) * p["gain"] * (gf * _sigmoid(gf)))
    return results


def _lin_kernel(gla_dims, hgrn_dims, layer, xa_ref, lr_ref, xc_ref, f_ref,
                w2_ref, b_ref, gain_a_ref, lbp_ref, gain_c_ref, *refs):
    const_a, const_c = refs[0:6], refs[6:12]
    oa_ref, oc_ref, sta_ref, stc_ref = refs[12:]

    @pl.when(pl.program_id(1) == 0)
    def _():
        sta_ref[...] = jnp.zeros_like(sta_ref)
        stc_ref[...] = jnp.zeros_like(stc_ref)

    def problem(consts, st_ref, heads, dv, **values):
        names = ("w_ref", "m_ref", "bdk_ref", "bdv_ref", "bds_ref", "ones_ref")
        return dict(zip(names, consts), st_ref=st_ref, heads=heads, dv=dv, **values)

    heads, dk, dv = gla_dims
    K, V = heads * dk, heads * dv
    gk = _dot3(lr_ref[...], w2_ref[...]) + b_ref[...]
    gla = problem(
        const_a, sta_ref, heads, dv,
        q=xa_ref[:, 0:K].astype(F32) * (dk ** -0.5), k=xa_ref[:, K:2 * K].astype(F32),
        v=xa_ref[:, 2 * K:2 * K + V], g=xa_ref[:, 2 * K + V:2 * K + 2 * V], gain=gain_a_ref[...],
        la=(jnp.minimum(gk, 0.0) - jnp.log1p(jnp.exp(-jnp.abs(gk)))) * (1.0 / GLA_GATE_TAU))

    heads, dk, dv = hgrn_dims
    K, V = heads * dk, heads * dv
    lbp = lbp_ref[...]
    depth = lbp.shape[0]
    mx = lbp[0:1]
    for r in range(1, depth):
        mx = jnp.maximum(mx, lbp[r:r + 1])
    ex = [jnp.exp(lbp[r:r + 1] - mx) for r in range(depth)]
    den = ex[0]
    for r in range(1, depth):
        den = den + ex[r]
    lb = jnp.zeros_like(den)
    for r in range(1, layer + 1):
        lb = lb + ex[r] / den
    cq = xc_ref[:, 0:K].astype(F32)
    f = f_ref[...]
    hgrn = problem(
        const_c, stc_ref, heads, dv,
        q=cq * _sigmoid(cq), k=(1.0 - lb) * _sigmoid(-f), v=xc_ref[:, K:K + V],
        g=xc_ref[:, K + V:K + 2 * V], gain=gain_c_ref[...],
        la=jnp.log(lb + (1.0 - lb) * _sigmoid(f)))

    o_a, o_c = _lin_groups([gla, hgrn])
    oa_ref[...] = o_a.astype(oa_ref.dtype)
    oc_ref[...] = o_c.astype(oc_ref.dtype)


def _lin_call(batch, seq, layer, gla_dims, hgrn_dims, row_inputs, small_inputs):
    T = LIN_TILE
    nt = seq // T
    consts = _lin_constants(*gla_dims) + _lin_constants(*hgrn_dims)
    vw_a, vw_c = gla_dims[0] * gla_dims[2], hgrn_dims[0] * hgrn_dims[2]
    row = lambda width: pl.BlockSpec((T, width), lambda b, t: (b * nt + t, 0))
    full = lambda a: pl.BlockSpec(a.shape, lambda b, t: (0,) * a.ndim)
    return pl.pallas_call(
        functools.partial(_lin_kernel, gla_dims, hgrn_dims, layer),
        grid=(batch, nt),
        in_specs=[row(a.shape[1]) for a in row_inputs] + [full(a) for a in small_inputs]
        + [full(a) for a in consts],
        out_specs=[row(vw_a), row(vw_c)],
        out_shape=[jax.ShapeDtypeStruct((batch * seq, vw_a), BF16),
                   jax.ShapeDtypeStruct((batch * seq, vw_c), BF16)],
        scratch_shapes=[pltpu.VMEM((vw_a, gla_dims[0] * gla_dims[1]), F32),
                        pltpu.VMEM((vw_c, hgrn_dims[0] * hgrn_dims[1]), F32)],
        compiler_params=_cparams("parallel", "arbitrary"),
        name="lin_attn",
    )(*row_inputs, *small_inputs, *consts)


def _moba_constants(heads, hd, seq):
    W = heads * hd
    nb = LANES // heads
    rot = hd // 4
    half = rot // 2
    inv_freq = ROPE_THETA ** (-(np.arange(0, rot, 2, dtype=np.float64) / rot))
    ang = np.arange(seq, dtype=np.float64)[:, None] * inv_freq[None, :]
    cos, sin = np.cos(ang), np.sin(ang)
    c = np.ones((seq, hd), np.float32)
    sp = np.zeros((seq, hd), np.float32)
    sm = np.zeros((seq, hd), np.float32)
    c[:, :half] = cos
    c[:, half:rot] = cos
    sm[:, :half] = -sin
    sp[:, half:rot] = sin
    reps = LANES // hd
    tabs = [jnp.asarray(np.tile(a, (1, reps)), F32) for a in (c, sp, sm)]
    hl = np.repeat(np.arange(heads), hd)
    ones_h = (hl[:, None] == hl[None, :]).astype(np.float32)
    hm = (np.arange(heads)[:, None] == hl[None, :]).astype(np.float32)
    pb = np.zeros((LANES, heads * LANES), np.float32)
    gl = np.arange(LANES)
    pb[gl, (gl % heads) * LANES + hd + gl // heads] = 1.0
    return tabs, (jnp.asarray(ones_h, BF16), jnp.asarray(hm, F32), jnp.asarray(pb, BF16))


def _moba_prep_block(heads, hd, t, xq, xk, xv, tabs, qg, kg, ones_ref, hm_ref, pexp_ref, kmt_ref,
                     qa_ref, ka_ref, write_vt):
    W = heads * hd
    half = hd // 8
    reps = W // LANES
    c, sp, sm = (jnp.concatenate([a] * reps, axis=1) for a in tabs)

    def norm_rot(x, gain):
        ms = _dot((x * x).astype(BF16), ones_ref[...]) * (1.0 / hd)
        y = x * lax.rsqrt(ms + NORM_EPS) * gain
        return y * c + pltpu.roll(y, half, 1) * sp + pltpu.roll(y, W - half, 1) * sm

    q = norm_rot(xq, qg)
    k = norm_rot(xk, kg)

    gate = _dot3(q, kmt_ref[...], nt=True)
    lane = lax.broadcasted_iota(jnp.int32, gate.shape, 1)
    blk = lax.shift_right_logical(lane, int(np.log2(heads)))
    valid = blk < t
    gate = jnp.where(valid, gate, -jnp.inf)
    rank = jnp.zeros(gate.shape, F32)
    nslots = LANES // heads
    for s in range(1, nslots):
        other = pltpu.roll(gate, s * heads, 1)
        ahead = (other > gate) | ((other == gate) & (blk >= s))
        rank = rank + ahead.astype(F32)
    sel = valid & (rank < float(MOBA_TOPK))
    bias = jnp.where(sel | (blk == t), 0.0, NEG)

    bias_slab = _dot(bias.astype(BF16), pexp_ref[...])
    qs = q * (hd ** -0.5 * LOG2E)
    lane1 = lax.broadcasted_iota(jnp.int32, (q.shape[0], LANES), 1)
    onehot = jnp.where(lane1 == hd + t, 1.0, 0.0)
    per_vreg = LANES // hd
    for h in range(heads):
        src = slice((h // per_vreg) * LANES, (h // per_vreg + 1) * LANES)
        dst = slice(h * LANES, (h + 1) * LANES)
        shift = (LANES - (h % per_vreg) * hd) % LANES
        move = (lambda a: a) if shift == 0 else (lambda a: pltpu.roll(a, shift, 1))
        qa_ref[:, dst] = jnp.where(lane1 < hd, move(qs[:, src]), bias_slab[:, dst]).astype(qa_ref.dtype)
        ka_ref[:, dst] = jnp.where(lane1 < hd, move(k[:, src]), onehot).astype(ka_ref.dtype)

    kmean = jnp.mean(k, axis=0, keepdims=True)
    row0 = pl.multiple_of(t * heads, heads)
    kmt_ref[pl.ds(row0, heads), :] = kmean * hm_ref[...]

    vt = xv.T
    vt_rows = _moba_vt_rows(hd)
    ones = jnp.ones((vt_rows - hd, vt.shape[1]), F32)
    for h in range(heads):
        write_vt(h, slice(0, hd), vt[h * hd:(h + 1) * hd])
        write_vt(h, slice(hd, vt_rows), ones)


def _moba_vt_rows(hd):
    return hd + 2 * SUBLANES


def _moba_attn_kernel(hd, qa_ref, ka_ref, vt_ref, o_ref, acc_ref, sa_ref, sb_ref):
    t = pl.program_id(1)
    tq = qa_ref.shape[0]
    blk = MOBA_BLOCK
    nh = vt_ref.shape[1]

    def scores_to(dst_ref, n, hh):
        cols = slice(hh * LANES, (hh + 1) * LANES)
        rows = pl.ds(pl.multiple_of(n * blk, blk), blk)
        dst_ref[hh] = _dot_nt(ka_ref[rows, cols], qa_ref[:, cols])

    def update(s, m, n, hh):
        m_new = jnp.maximum(m, jnp.max(s, axis=0, keepdims=True))
        alpha = jnp.exp2(m - m_new)
        p = jnp.exp2(s - m_new).astype(BF16)
        acc_ref[hh] = alpha * acc_ref[hh] + _dot(vt_ref[0, hh, n], p)
        return m_new

    def step(cur_ref, nxt_ref, n, ms):
        new = []
        for hh in range(nh):
            scores_to(nxt_ref, n + 1, hh)
            new.append(update(cur_ref[hh], ms[hh], n, hh))
        return tuple(new)

    acc_ref[...] = jnp.zeros_like(acc_ref)
    for hh in range(nh):
        scores_to(sa_ref, 0, hh)

    def pair(i, ms):
        ms = step(sa_ref, sb_ref, 2 * i, ms)
        return step(sb_ref, sa_ref, 2 * i + 1, ms)

    m0 = jnp.full((1, tq), 2 * NEG, F32)
    ms = lax.fori_loop(0, t // 2, pair, (m0,) * nh)

    def odd_block(ms):
        new = []
        for hh in range(nh):
            new.append(update(sa_ref[hh], ms[hh], t - 1, hh))
            scores_to(sa_ref, t, hh)
        return tuple(new)

    ms = lax.cond(t % 2 == 1, odd_block, lambda ms: ms, ms)

    key = lax.broadcasted_iota(jnp.int32, (blk, tq), 0)
    qry = lax.broadcasted_iota(jnp.int32, (blk, tq), 1)
    outs = []
    for hh in range(nh):
        s = jnp.where(key <= qry, sa_ref[hh], NEG)
        update(s, ms[hh], t, hh)
        a = acc_ref[hh]
        outs.append(a[0:hd] / a[hd:hd + 1])
    o_ref[...] = jnp.concatenate(outs, axis=0).T.astype(o_ref.dtype)


def _moba_attn(qa, ka, vt, batch, seq, heads, hd):
    nt = seq // MOBA_BLOCK
    vt_rows = vt.shape[3]
    return pl.pallas_call(
        functools.partial(_moba_attn_kernel, hd),
        grid=(batch, nt),
        in_specs=[pl.BlockSpec((MOBA_BLOCK, heads * LANES), lambda b, t: (b * nt + t, 0)),
                  pl.BlockSpec((seq, heads * LANES), lambda b, t: (b, 0)),
                  pl.BlockSpec((1, heads, nt, vt_rows, MOBA_BLOCK), lambda b, t: (b, 0, 0, 0, 0))],
        out_specs=pl.BlockSpec((MOBA_BLOCK, heads * hd), lambda b, t: (b * nt + t, 0)),
        out_shape=jax.ShapeDtypeStruct((batch * seq, heads * hd), BF16),
        scratch_shapes=[pltpu.VMEM((heads, vt_rows, MOBA_BLOCK), F32),
                        pltpu.VMEM((heads, MOBA_BLOCK, MOBA_BLOCK), F32),
                        pltpu.VMEM((heads, MOBA_BLOCK, MOBA_BLOCK), F32)],
        compiler_params=_cparams("parallel", "arbitrary"),
        name="moba_attn",
    )(qa, ka, vt)


def _out_proj_kernel(widths, x_ref, oa_ref, ob_ref, oc_ref, w_ref, g_ref, wr_ref,
                     x1_ref, hx_ref, route_ref):
    rows = x_ref.shape[0]
    hw = x_ref.shape[1]
    chunks = [slice(r, r + OUT_PROJ_CHUNK) for r in range(0, rows, OUT_PROJ_CHUNK)]
    accs = []
    for c in chunks:
        acc = x_ref[c, :]
        r0 = 0
        for o_ref, wd in zip((oa_ref, ob_ref, oc_ref), widths):
            acc = acc + _dot(o_ref[c, :], w_ref[r0:r0 + wd, :])
            r0 += wd
        x1_ref[c, :] = acc
        accs.append(acc)
    wr_hi, wr_lo = _hi_lo(wr_ref[...])
    wr_cat = jnp.concatenate([wr_hi, wr_lo, wr_hi], axis=0)
    all_logits = []
    for c, acc in zip(chunks, accs):
        ms = jnp.mean(acc * acc, axis=-1, keepdims=True)
        h2 = acc * lax.rsqrt(ms + NORM_EPS) * g_ref[...]
        hx_ref[c, 0:hw] = h2
        h_hi, h_lo = _hi_lo(h2)
        all_logits.append(_dot(jnp.concatenate([h_hi, h_hi, h_lo], axis=1), wr_cat))
    for c, logits in zip(chunks, all_logits):
        route = _route(logits)
        hx_ref[c, hw:hw + LANES] = route
        route_ref[c, :] = route


def _route(logits):
    G, E = MOE_GROUPS, MOE_EXPERTS_PER_GROUP
    lane = lax.broadcasted_iota(jnp.int32, logits.shape, 1).astype(F32)
    big = float(LANES)
    is_g = lane < G
    lg = jnp.where(is_g, logits, -jnp.inf)
    mg = jnp.max(lg, axis=-1, keepdims=True)
    gate_group = 1.0 / jnp.sum(jnp.exp(lg - mg), axis=-1, keepdims=True)
    g_sel = jnp.min(jnp.where(lg == mg, lane, big), axis=-1, keepdims=True)
    lo = G + E * g_sel
    in_grp = (lane >= lo) & (lane < lo + E)
    v = jnp.where(in_grp, logits, -jnp.inf)
    v1 = jnp.max(v, axis=-1, keepdims=True)
    i1 = jnp.min(jnp.where(v == v1, lane, big), axis=-1, keepdims=True)
    v = jnp.where(lane == i1, -jnp.inf, v)
    v2 = jnp.max(v, axis=-1, keepdims=True)
    i2 = jnp.min(jnp.where(v == v2, lane, big), axis=-1, keepdims=True)
    e2 = jnp.exp(v2 - v1)
    w1 = gate_group / (1.0 + e2)
    w2 = gate_group * e2 / (1.0 + e2)
    first = i1 < i2
    a = jnp.minimum(i1, i2) - lo
    b = jnp.maximum(i1, i2) - lo
    w_lo = jnp.where(first, w1, w2)
    w_hi = jnp.where(first, w2, w1)
    pos = jnp.zeros_like(a)
    swap = a < 0.0
    for p, (ea, eb) in enumerate(_moe_pair_order()):
        hit = (a == float(min(ea, eb))) & (b == float(max(ea, eb)))
        pos = jnp.where(hit, float(p), pos)
        if ea > eb:
            swap = swap | hit
    cls = g_sel * float(E * (E - 1) // 2) + pos
    route = jnp.where(lane == 0.0, cls, 0.0)
    route = jnp.where(lane == 1.0, jnp.where(swap, w_hi, w_lo), route)
    return jnp.where(lane == 2.0, jnp.where(swap, w_lo, w_hi), route)


def _out_proj(x, oa, ob, oc, w, g, wr):
    n, d = x.shape
    tm = ROW_TILE
    widths = (oa.shape[1], ob.shape[1], oc.shape[1])
    row = lambda width: pl.BlockSpec((tm, width), lambda i: (i, 0))
    full = lambda a: pl.BlockSpec(a.shape, lambda i: (0, 0))
    g2 = g.reshape(1, d)
    hxw = d + LANES
    return pl.pallas_call(
        functools.partial(_out_proj_kernel, widths),
        grid=(n // tm,),
        in_specs=[row(d), row(widths[0]), row(widths[1]), row(widths[2]), full(w), full(g2), full(wr)],
        out_specs=[row(d), row(hxw), row(LANES)],
        out_shape=[jax.ShapeDtypeStruct((n, d), F32), jax.ShapeDtypeStruct((n, hxw), F32),
                   jax.ShapeDtypeStruct((n, LANES), F32)],
        compiler_params=_cparams("parallel"),
        name="out_proj",
    )(x, oa, ob, oc, w, g2, wr)


@functools.lru_cache(maxsize=None)
def _moe_pair_order():
    E = MOE_EXPERTS_PER_GROUP
    todo = {(a, b) for a in range(E) for b in range(a + 1, E)}

    def extend(seq, todo):
        if not todo:
            return seq
        for pr in sorted(todo):
            for cand in (pr, pr[::-1]):
                if not seq or cand[0] == seq[-1][0] or cand[1] == seq[-1][1]:
                    out = extend(seq + [cand], todo - {pr})
                    if out:
                        return out
        return None

    return tuple(extend([], todo))


def _moe_classes():
    G, E = MOE_GROUPS, MOE_EXPERTS_PER_GROUP
    ea = [g * E + a for g in range(G) for a, _ in _moe_pair_order()]
    eb = [g * E + b for g in range(G) for _, b in _moe_pair_order()]
    return ea, eb


def _moe_plan_kernel(elo, ehi, cls_ref, dest_ref, tab_ref):
    R = MOE_TILE
    cls = cls_ref[...]
    rows = cls.shape[0]
    ri = lax.broadcasted_iota(jnp.int32, (LANES, LANES), 0)
    ci = lax.broadcasted_iota(jnp.int32, (LANES, LANES), 1)
    before_lane = (ri < ci).astype(BF16)
    all_lane = jnp.ones((LANES, LANES), BF16)
    rr = lax.broadcasted_iota(jnp.int32, (rows, rows), 0)
    rc = lax.broadcasted_iota(jnp.int32, (rows, rows), 1)
    before_row = (rc < rr).astype(BF16)
    all_row = jnp.ones((rows, rows), BF16)
    tile_row0 = lax.broadcasted_iota(jnp.int32, (8, LANES), 1).astype(F32) * float(R)
    off = jnp.zeros((rows, LANES), F32)
    dest = jnp.zeros((rows, LANES), F32)
    tile_cls = jnp.zeros((8, LANES), F32)
    for k in range(len(elo)):
        mask = (cls == k).astype(BF16)
        row_tot = _dot(mask, all_lane)
        rank = _dot(mask, before_lane) + _dot(before_row, row_tot.astype(BF16))
        count = _dot(all_row, row_tot.astype(BF16))
        dest = dest + mask.astype(F32) * (off + rank)
        off = off + jnp.floor((count + float(R - 1)) * (1.0 / R)) * float(R)
        tile_cls = tile_cls + jnp.where(tile_row0 >= off[0:8], 1.0, 0.0)
    dest_ref[...] = dest.astype(jnp.int32)
    tile_cls = jnp.minimum(tile_cls, float(len(elo) - 1))
    t_lo = jnp.zeros((8, LANES), F32)
    t_hi = jnp.zeros((8, LANES), F32)
    for k in range(len(elo)):
        t_lo = jnp.where(tile_cls == k, float(elo[k]), t_lo)
        t_hi = jnp.where(tile_cls == k, float(ehi[k]), t_hi)
    sub = lax.broadcasted_iota(jnp.int32, (8, LANES), 0)
    tab = jnp.where(sub == 0, t_lo, jnp.where(sub == 1, t_hi, off[0:8] * (1.0 / R)))
    tab_ref[...] = tab.astype(jnp.int32)


def _moe_plan(cls):
    elo, ehi = _moe_classes()
    rows = cls.shape[0]
    return pl.pallas_call(
        functools.partial(_moe_plan_kernel, elo, ehi),
        out_shape=[jax.ShapeDtypeStruct((rows, LANES), jnp.int32),
                   jax.ShapeDtypeStruct((8, LANES), jnp.int32)],
        name="moe_plan",
    )(cls)


def _row_copies(n, make):
    def start(g, c):
        base = pl.multiple_of(g * SUBLANES, SUBLANES)
        for s in range(SUBLANES):
            make(base + s).start(priority=s % 2)
        return c

    def wait(g, c):
        base = pl.multiple_of(g * SUBLANES, SUBLANES)
        for s in range(SUBLANES):
            make(base + s).wait()
        return c

    lax.fori_loop(0, n // SUBLANES, start, 0)
    lax.fori_loop(0, n // SUBLANES, wait, 0)


def _moe_dispatch_kernel(dest_ref, hx_ref, xs_in_ref, xs_ref, sem):
    del xs_in_ref
    _row_copies(hx_ref.shape[0], lambda r: pltpu.make_async_copy(
        hx_ref.at[pl.ds(r, 1)], xs_ref.at[pl.ds(dest_ref[r], 1)], sem))


def _moe_dispatch(dest, hx, n_rows):
    n, w = hx.shape
    tm = ROW_TILE
    return pl.pallas_call(
        _moe_dispatch_kernel,
        grid=(n // tm,),
        in_specs=[pl.BlockSpec((tm,), lambda i: (i,), memory_space=pltpu.SMEM),
                  pl.BlockSpec((tm, w), lambda i: (i, 0)),
                  pl.BlockSpec(memory_space=pl.ANY)],
        out_specs=pl.BlockSpec(memory_space=pl.ANY),
        out_shape=jax.ShapeDtypeStruct((n_rows, w), hx.dtype),
        scratch_shapes=[pltpu.SemaphoreType.DMA],
        input_output_aliases={2: 0},
        compiler_params=_cparams("arbitrary"),
        name="moe_dispatch",
    )(dest, hx, jnp.zeros((n_rows, w), hx.dtype))


def _moe_experts_kernel(elo_ref, ehi_ref, nt_ref, xs_ref, wg0, wu0, wd0, wg1, wu1, wd1, y_ref,
                        wgu_ref, wdn_ref):
    j = pl.program_id(0)
    hw = xs_ref.shape[1] - LANES
    jp = jnp.maximum(j - 1, 0)
    fresh = (j == 0) | (elo_ref[j] != elo_ref[jp]) | (ehi_ref[j] != ehi_ref[jp])

    @pl.when((j < nt_ref[0]) & fresh)
    def _():
        for slot, (wg, wu, wd) in enumerate(((wg0, wu0, wd0), (wg1, wu1, wd1))):
            wgu_ref[slot, 0] = wg[0].astype(BF16)
            wgu_ref[slot, 1] = wu[0].astype(BF16)
            wdn_ref[slot] = wd[0].astype(BF16)

    @pl.when(j < nt_ref[0])
    def _():
        x = xs_ref[:, 0:hw].astype(BF16)
        route = xs_ref[:, hw:hw + LANES]
        lane = lax.broadcasted_iota(jnp.int32, route.shape, 1)
        y = None
        for slot in range(2):
            wt = jnp.sum(jnp.where(lane == 1 + slot, route, 0.0), axis=-1, keepdims=True)
            a = _dot(x, wgu_ref[slot, 0])
            u = _dot(x, wgu_ref[slot, 1])
            he = (a * _sigmoid(a)) * u * wt
            part = _dot(he.astype(BF16), wdn_ref[slot])
            y = part if y is None else y + part
        y_ref[...] = y

    @pl.when(j >= nt_ref[0])
    def _():
        y_ref[...] = jnp.zeros_like(y_ref)


def _moe_experts(tab, xs, wg, wu, wd, layer):
    n_rows, w = xs.shape
    _, ne, d, ff = wg.shape
    R = MOE_TILE
    last = lambda j, nt: jnp.minimum(j, nt[0] - 1)
    wspec = lambda shape, which: pl.BlockSpec(
        (None,) + shape, lambda j, elo, ehi, nt: (layer, (elo, ehi)[which][last(j, nt)], 0, 0))
    grid_spec = pltpu.PrefetchScalarGridSpec(
        num_scalar_prefetch=3,
        grid=(n_rows // R,),
        in_specs=[pl.BlockSpec((R, w), lambda j, elo, ehi, nt: (last(j, nt), 0)),
                  wspec((1, d, ff), 0), wspec((1, d, ff), 0), wspec((1, ff, d), 0),
                  wspec((1, d, ff), 1), wspec((1, d, ff), 1), wspec((1, ff, d), 1)],
        out_specs=pl.BlockSpec((R, d), lambda j, elo, ehi, nt: (j, 0)),
        scratch_shapes=[pltpu.VMEM((2, 2, d, ff), BF16), pltpu.VMEM((2, ff, d), BF16)],
    )
    return pl.pallas_call(
        _moe_experts_kernel,
        grid_spec=grid_spec,
        out_shape=jax.ShapeDtypeStruct((n_rows, d), F32),
        compiler_params=_cparams("arbitrary"),
        name="moe_experts",
    )(tab[0], tab[1], tab[2, 0:1], xs, wg, wu, wd, wg, wu, wd)


def _moe_combine_kernel(dest_ref, x1_ref, y_ref, o_ref, buf_ref, sem):
    _row_copies(buf_ref.shape[0], lambda r: pltpu.make_async_copy(
        y_ref.at[pl.ds(dest_ref[r], 1)], buf_ref.at[pl.ds(r, 1)], sem))
    o_ref[...] = x1_ref[...] + buf_ref[...]


def _moe_combine(dest, x1, y):
    n, d = x1.shape
    tm = ROW_TILE
    return pl.pallas_call(
        _moe_combine_kernel,
        grid=(n // tm,),
        in_specs=[pl.BlockSpec((tm,), lambda i: (i,), memory_space=pltpu.SMEM),
                  pl.BlockSpec((tm, d), lambda i: (i, 0)),
                  pl.BlockSpec(memory_space=pl.ANY)],
        out_specs=pl.BlockSpec((tm, d), lambda i: (i, 0)),
        out_shape=jax.ShapeDtypeStruct((n, d), F32),
        scratch_shapes=[pltpu.VMEM((tm, d), F32), pltpu.SemaphoreType.DMA],
        compiler_params=_cparams("arbitrary"),
        name="moe_combine",
    )(dest, x1, y)


def _moe(x1, hx, route, wg, wu, wd, layer):
    n, d = x1.shape
    elo, _ = _moe_classes()
    n_rows = n + len(elo) * MOE_TILE
    assert n % LANES == 0 and n_rows // MOE_TILE <= LANES
    cls = route[:, 0].astype(jnp.int32).reshape(n // LANES, LANES)
    dest, tab = _moe_plan(cls)
    dest = dest.reshape(n)
    xs = _moe_dispatch(dest, hx, n_rows)
    y = _moe_experts(tab, xs, wg, wu, wd, layer)
    return _moe_combine(dest, x1, y)


def _pad_cols(w, width):
    return jnp.pad(w, ((0, 0), (0, width - w.shape[1])))


def kernel(x, attn_norm_g, w_in, gla_gk_w2, gla_gk_b, gla_norm_g, moba_qnorm_g, moba_knorm_g,
           hgrn_lb_param, hgrn_norm_g, w_out, ffn_norm_g, w_router_group, w_router_expert,
           w_exp_gate, w_exp_up, w_exp_down):
    batch, seq, d = x.shape
    depth = w_in.shape[0]
    lowrank, gla_kw = gla_gk_w2.shape[1:]
    gla_dv = gla_norm_g.shape[1]
    moba_hd = moba_qnorm_g.shape[1]
    hgrn_kw = hgrn_lb_param.shape[1]
    hgrn_dv = hgrn_norm_g.shape[1]
    mix_w = w_out.shape[1]
    hgrn_heads = 4
    gla_heads = 4
    gla_vw = gla_heads * gla_dv
    hgrn_vw = hgrn_heads * hgrn_dv
    moba_w = mix_w - gla_vw - hgrn_vw
    moba_heads = moba_w // moba_hd
    gla_dk = gla_kw // gla_heads
    hgrn_dk = hgrn_kw // hgrn_heads

    splits = (gla_kw, gla_kw, gla_vw, gla_vw, lowrank, moba_w, moba_w, moba_w,
              hgrn_kw, hgrn_kw, hgrn_vw, hgrn_vw)
    offs = np.concatenate([[0], np.cumsum(splits)]).tolist()
    order = (0, 1, 2, 3, 4, 5, 6, 7, 8, 10, 11, 9)
    pieces, dst = [], 0
    for i in order:
        padded = -(-splits[i] // LANES) * LANES
        pieces.append((offs[i], splits[i], dst, padded))
        dst += padded
    g_gla = 2 * gla_kw + 2 * gla_vw
    g_hg = hgrn_kw + 2 * hgrn_vw
    bounds = np.cumsum([0, g_gla, LANES, 3 * moba_w, g_hg, hgrn_kw]).tolist()
    spans = [(bounds[i], bounds[i + 1]) for i in range(5)]
    groups = (spans[0], spans[1], spans[3], spans[4])
    assert dst == bounds[-1]

    w_in_t = jnp.swapaxes(w_in, 1, 2)
    xf = x.reshape(batch * seq, d)
    for l in range(depth):
        y_gla, y_lr, y_hg, y_f, qa, ka, vt = _in_proj(
            xf, attn_norm_g[l], w_in_t, l, tuple(pieces), groups, (BF16, F32, BF16, F32), spans[2],
            moba_qnorm_g[l], moba_knorm_g[l], batch, seq, moba_heads, moba_hd)

        w2 = jnp.pad(gla_gk_w2[l], ((0, LANES - lowrank), (0, 0)))
        o_a, o_c = _lin_call(
            batch, seq, l, (gla_heads, gla_dk, gla_dv), (hgrn_heads, hgrn_dk, hgrn_dv),
            (y_gla, y_lr, y_hg, y_f),
            (w2, gla_gk_b[l].reshape(1, gla_kw), jnp.tile(gla_norm_g[l], gla_heads).reshape(1, gla_vw),
             hgrn_lb_param, jnp.tile(hgrn_norm_g[l], hgrn_heads).reshape(1, hgrn_vw)))

        o_b = _moba_attn(qa, ka, vt, batch, seq, moba_heads, moba_hd)

        wr = _pad_cols(jnp.concatenate([w_router_group[l], w_router_expert[l]], axis=1), LANES)
        x1, hx, route = _out_proj(xf, o_a, o_b, o_c, w_out[l].astype(BF16), ffn_norm_g[l], wr)
        xf = _moe(x1, hx, route, w_exp_gate, w_exp_up, w_exp_down, l)
    return xf.reshape(batch, seq, d)
```

```python
import functools

import numpy as np
import jax
import jax.numpy as jnp
from jax import lax
from jax.experimental import pallas as pl
from jax.experimental.pallas import tpu as pltpu

F32 = jnp.float32
BF16 = jnp.bfloat16

NORM_EPS = 1e-6
GLA_GATE_TAU = 16.0
ROPE_THETA = 500000.0
MOBA_BLOCK = 256
MOBA_TOPK = 3
MOE_GROUPS = 4
MOE_EXPERTS_PER_GROUP = 4
MOE_TOPK = 2

LANES = 128
SUBLANES = 8
VMEM_LIMIT = 56 * 1024 * 1024
LIN_CHUNK = 64
LIN_TILE = 512
ROW_TILE = 512
OUT_PROJ_CHUNK = 128
MOE_TILE = 256
NEG = -1e30
LOG2E = 1.4426950408889634


def _cparams(*sem):
    return pltpu.CompilerParams(dimension_semantics=sem, vmem_limit_bytes=VMEM_LIMIT)


def _sigmoid(x):
    return 1.0 / (1.0 + jnp.exp(-x))


def _dot(a, b):
    return jnp.dot(a, b, preferred_element_type=F32)


def _dot_nt(a, b):
    return lax.dot_general(a, b, (((1,), (1,)), ((), ())), preferred_element_type=F32)


def _dot_tn(a, b):
    return lax.dot_general(a, b, (((0,), (0,)), ((), ())), preferred_element_type=F32)


def _hi_lo(x):
    hi = x.astype(BF16)
    return hi, (x - hi.astype(F32)).astype(BF16)


def _dot3(a, b, nt=False):
    a_hi, a_lo = _hi_lo(a)
    b_hi, b_lo = _hi_lo(b)
    lhs = jnp.concatenate([a_hi, a_hi, a_lo], axis=1)
    if nt:
        return _dot_nt(lhs, jnp.concatenate([b_hi, b_lo, b_hi], axis=1))
    return _dot(lhs, jnp.concatenate([b_hi, b_lo, b_hi], axis=0))


def _in_proj_kernel(pieces, groups, moba_cols, heads, hd, tiles_per_seq,
                    x_ref, g_ref, w_ref, qg_ref, kg_ref, c_ref, sp_ref, sm_ref,
                    ones_ref, hm_ref, pexp_ref, *refs):
    out_refs = refs[:len(groups)]
    qa_ref, ka_ref, vt_ref, wcat_ref, kmt_ref = refs[len(groups):]
    i = pl.program_id(0)

    @pl.when(i == 0)
    def _():
        for src, width, dst, padded in pieces:
            blk = w_ref[src:src + width, :].astype(BF16)
            if padded > width:
                blk = jnp.concatenate([blk, jnp.zeros((padded - width, blk.shape[1]), BF16)], axis=0)
            wcat_ref[dst:dst + padded, :] = blk

    tile = lax.rem(i, tiles_per_seq)

    @pl.when(tile == 0)
    def _():
        kmt_ref[...] = jnp.zeros_like(kmt_ref)

    x = x_ref[...]
    ms = jnp.mean(x * x, axis=-1, keepdims=True)
    h = (x * lax.rsqrt(ms + NORM_EPS) * g_ref[...]).astype(BF16)
    ym = _dot_nt(h, wcat_ref[moba_cols[0]:moba_cols[1], :])
    for (a, b), o_ref in zip(groups, out_refs):
        o_ref[...] = _dot_nt(h, wcat_ref[a:b, :]).astype(o_ref.dtype)

    W = heads * hd
    blocks = x.shape[0] // MOBA_BLOCK
    for blk in range(blocks):
        rows = slice(blk * MOBA_BLOCK, (blk + 1) * MOBA_BLOCK)

        def write_vt(hh, vrows, value, blk=blk):
            vt_ref[0, hh, blk, vrows, :] = value.astype(vt_ref.dtype)

        _moba_prep_block(
            heads, hd, tile * blocks + blk, ym[rows, 0:W], ym[rows, W:2 * W], ym[rows, 2 * W:3 * W],
            (c_ref[rows, :], sp_ref[rows, :], sm_ref[rows, :]), qg_ref[...], kg_ref[...],
            ones_ref, hm_ref, pexp_ref, kmt_ref, qa_ref.at[rows], ka_ref.at[rows], write_vt)


def _in_proj(x, g, w_all_t, layer, pieces, groups, dtypes, moba_cols, qg, kg, batch, seq, heads, hd):
    n, d = x.shape
    tm = ROW_TILE
    cols = w_all_t.shape[1]
    W = heads * hd
    nt = seq // MOBA_BLOCK
    blocks = tm // MOBA_BLOCK
    tiles_per_seq = seq // tm
    tabs, consts = _moba_constants(heads, hd, seq)
    tile_g = lambda v: jnp.tile(v.astype(F32), heads).reshape(1, W)
    row = lambda width: pl.BlockSpec((tm, width), lambda i: (i, 0))
    full = lambda a: pl.BlockSpec(a.shape, lambda i: (0,) * a.ndim)
    wide_sd = jax.ShapeDtypeStruct((n, heads * LANES), BF16)
    out_shape = [jax.ShapeDtypeStruct((n, b - a), dt) for (a, b), dt in zip(groups, dtypes)]
    vt_rows = _moba_vt_rows(hd)
    out_shape += [wide_sd, wide_sd, jax.ShapeDtypeStruct((batch, heads, nt, vt_rows, MOBA_BLOCK), BF16)]
    out_specs = [row(b - a) for (a, b) in groups] + [row(heads * LANES), row(heads * LANES)]
    out_specs.append(pl.BlockSpec((1, heads, blocks, vt_rows, MOBA_BLOCK),
                                  lambda i: (i // tiles_per_seq, 0, i % tiles_per_seq, 0, 0)))
    return pl.pallas_call(
        functools.partial(_in_proj_kernel, pieces, groups, moba_cols, heads, hd, tiles_per_seq),
        grid=(n // tm,),
        in_specs=[row(d),
                  pl.BlockSpec((1, d), lambda i: (0, 0)),
                  pl.BlockSpec((None, cols, d), lambda i: (layer, 0, 0), pipeline_mode=pl.Buffered(1)),
                  full(tile_g(qg)), full(tile_g(kg))]
        + [pl.BlockSpec((tm, LANES), lambda i: (i % tiles_per_seq, 0)) for _ in tabs]
        + [full(a) for a in consts],
        out_specs=out_specs,
        out_shape=out_shape,
        scratch_shapes=[pltpu.VMEM((pieces[-1][2] + pieces[-1][3], d), BF16),
                        pltpu.VMEM((LANES, W), F32)],
        compiler_params=_cparams("arbitrary"),
        name="in_proj",
    )(x, g.reshape(1, d), w_all_t, tile_g(qg), tile_g(kg), *tabs, *consts)


def _lin_constants(heads, dk, dv):
    L = LIN_CHUNK
    nlev = int(np.log2(L))
    K, V = heads * dk, heads * dv
    i = np.arange(L)[:, None]
    t = np.arange(L)[None, :]
    w_rows = [(t <= i), (t > i)]
    masks = []
    for lev in range(nlev):
        b = L >> lev
        half = b // 2
        r = (i // b) * b + half - 1
        w_rows.append((t > np.minimum(i, r)) & (t <= np.maximum(i, r)))
        j = t
        masks.append((i // b == j // b) & (i % b >= half) & (j % b < half))
    masks.append(i == t)
    w_all = np.concatenate(w_rows, axis=0).astype(np.float32)
    m_all = np.stack([np.tile(m, (1, heads)) for m in masks]).astype(np.float32)
    rh = np.repeat(np.arange(heads), L)[:, None]
    bdk = (rh == np.repeat(np.arange(heads), dk)[None, :]).astype(np.float32)
    bdv = (rh == np.repeat(np.arange(heads), dv)[None, :]).astype(np.float32)
    bds = (np.repeat(np.arange(heads), dv)[:, None]
           == np.repeat(np.arange(heads), dk)[None, :]).astype(np.float32)
    ones_v = (np.repeat(np.arange(heads), dv)[:, None]
              == np.repeat(np.arange(heads), dv)[None, :]).astype(np.float32)
    return (jnp.asarray(w_all, BF16), jnp.asarray(m_all, F32), jnp.asarray(bdk, BF16),
            jnp.asarray(bdv, BF16), jnp.asarray(bds, F32), jnp.asarray(ones_v, BF16))


def _lin_groups(probs):
    L = LIN_CHUNK
    items = []
    for p in probs:
        p["lacat"] = jnp.concatenate(_hi_lo(p["la"]), axis=1)
        p["v_bf"] = p["v"].astype(BF16)
        p["chunks"] = [slice(c * L, (c + 1) * L) for c in range(p["q"].shape[0] // L)]
    for p in probs:
        K = p["q"].shape[1]
        for c in p["chunks"]:
            z2 = _dot(p["w_ref"][...], p["lacat"][c])
            items.append((p, c, jnp.exp(z2[:, :K] + z2[:, K:])))
    nlev = probs[0]["m_ref"].shape[0] - 1
    scores = [None] * len(items)
    for lev in range(nlev + 1):
        for i, (p, c, e) in enumerate(items):
            q, k = p["q"], p["k"]
            if lev < nlev:
                el = e[(2 + lev) * L:(3 + lev) * L]
                ql = (q[c] * el).astype(BF16)
                kl = (k[c] * el).astype(BF16)
            else:
                ql = q[c].astype(BF16)
                kl = k[c].astype(BF16)
            kbd = jnp.concatenate([kl] * p["heads"], axis=0) * p["bdk_ref"][...]
            s = _dot_nt(ql, kbd) * p["m_ref"][lev]
            scores[i] = s if scores[i] is None else scores[i] + s
    intra, kvs = [], []
    for i, (p, c, e) in enumerate(items):
        vbd = jnp.concatenate([p["v_bf"][c]] * p["heads"], axis=0) * p["bdv_ref"][...]
        intra.append(_dot(scores[i].astype(BF16), vbd))
        kb = (p["k"][c] * e[L:2 * L]).astype(BF16)
        kvs.append(_dot_tn(p["v_bf"][c], kb) * p["bds_ref"][...])
    results = []
    i = 0
    for p in probs:
        st = p["st_ref"][...]
        outs = []
        for c in p["chunks"]:
            e = items[i][2]
            outs.append(intra[i] + _dot_nt((p["q"][c] * e[0:L]).astype(BF16), st.astype(BF16)))
            st = st * e[L - 1:L] + kvs[i]
            i += 1
        p["st_ref"][...] = st
        o = jnp.concatenate(outs, axis=0)
        ms = _dot((o * o).astype(BF16), p["ones_ref"][...]) * (1.0 / p["dv"])
        gf = p["g"].astype(F32)
        results.append(o * lax.rsqrt(ms + NORM_EPS) * p["gain"] * (gf * _sigmoid(gf)))
    return results


def _lin_kernel(gla_dims, hgrn_dims, layer, xa_ref, lr_ref, xc_ref, f_ref,
                w2_ref, b_ref, gain_a_ref, lbp_ref, gain_c_ref, *refs):
    const_a, const_c = refs[0:6], refs[6:12]
    oa_ref, oc_ref, sta_ref, stc_ref = refs[12:]

    @pl.when(pl.program_id(1) == 0)
    def _():
        sta_ref[...] = jnp.zeros_like(sta_ref)
        stc_ref[...] = jnp.zeros_like(stc_ref)

    def problem(consts, st_ref, heads, dv, **values):
        names = ("w_ref", "m_ref", "bdk_ref", "bdv_ref", "bds_ref", "ones_ref")
        return dict(zip(names, consts), st_ref=st_ref, heads=heads, dv=dv, **values)

    heads, dk, dv = gla_dims
    K, V = heads * dk, heads * dv
    gk = _dot3(lr_ref[...], w2_ref[...]) + b_ref[...]
    gla = problem(
        const_a, sta_ref, heads, dv,
        q=xa_ref[:, 0:K].astype(F32) * (dk ** -0.5), k=xa_ref[:, K:2 * K].astype(F32),
        v=xa_ref[:, 2 * K:2 * K + V], g=xa_ref[:, 2 * K + V:2 * K + 2 * V], gain=gain_a_ref[...],
        la=(jnp.minimum(gk, 0.0) - jnp.log1p(jnp.exp(-jnp.abs(gk)))) * (1.0 / GLA_GATE_TAU))

    heads, dk, dv = hgrn_dims
    K, V = heads * dk, heads * dv
    lbp = lbp_ref[...]
    depth = lbp.shape[0]
    mx = lbp[0:1]
    for r in range(1, depth):
        mx = jnp.maximum(mx, lbp[r:r + 1])
    ex = [jnp.exp(lbp[r:r + 1] - mx) for r in range(depth)]
    den = ex[0]
    for r in range(1, depth):
        den = den + ex[r]
    lb = jnp.zeros_like(den)
    for r in range(1, layer + 1):
        lb = lb + ex[r] / den
    cq = xc_ref[:, 0:K].astype(F32)
    f = f_ref[...]
    hgrn = problem(
        const_c, stc_ref, heads, dv,
        q=cq * _sigmoid(cq), k=(1.0 - lb) * _sigmoid(-f), v=xc_ref[:, K:K + V],
        g=xc_ref[:, K + V:K + 2 * V], gain=gain_c_ref[...],
        la=jnp.log(lb + (1.0 - lb) * _sigmoid(f)))

    o_a, o_c = _lin_groups([gla, hgrn])
    oa_ref[...] = o_a.astype(oa_ref.dtype)
    oc_ref[...] = o_c.astype(oc_ref.dtype)


def _lin_call(batch, seq, layer, gla_dims, hgrn_dims, row_inputs, small_inputs):
    T = LIN_TILE
    nt = seq // T
    consts = _lin_constants(*gla_dims) + _lin_constants(*hgrn_dims)
    vw_a, vw_c = gla_dims[0] * gla_dims[2], hgrn_dims[0] * hgrn_dims[2]
    row = lambda width: pl.BlockSpec((T, width), lambda b, t: (b * nt + t, 0))
    full = lambda a: pl.BlockSpec(a.shape, lambda b, t: (0,) * a.ndim)
    return pl.pallas_call(
        functools.partial(_lin_kernel, gla_dims, hgrn_dims, layer),
        grid=(batch, nt),
        in_specs=[row(a.shape[1]) for a in row_inputs] + [full(a) for a in small_inputs]
        + [full(a) for a in consts],
        out_specs=[row(vw_a), row(vw_c)],
        out_shape=[jax.ShapeDtypeStruct((batch * seq, vw_a), BF16),
                   jax.ShapeDtypeStruct((batch * seq, vw_c), BF16)],
        scratch_shapes=[pltpu.VMEM((vw_a, gla_dims[0] * gla_dims[1]), F32),
                        pltpu.VMEM((vw_c, hgrn_dims[0] * hgrn_dims[1]), F32)],
        compiler_params=_cparams("parallel", "arbitrary"),
        name="lin_attn",
    )(*row_inputs, *small_inputs, *consts)


def _moba_constants(heads, hd, seq):
    W = heads * hd
    nb = LANES // heads
    rot = hd // 4
    half = rot // 2
    inv_freq = ROPE_THETA ** (-(np.arange(0, rot, 2, dtype=np.float64) / rot))
    ang = np.arange(seq, dtype=np.float64)[:, None] * inv_freq[None, :]
    cos, sin = np.cos(ang), np.sin(ang)
    c = np.ones((seq, hd), np.float32)
    sp = np.zeros((seq, hd), np.float32)
    sm = np.zeros((seq, hd), np.float32)
    c[:, :half] = cos
    c[:, half:rot] = cos
    sm[:, :half] = -sin
    sp[:, half:rot] = sin
    reps = LANES // hd
    tabs = [jnp.asarray(np.tile(a, (1, reps)), F32) for a in (c, sp, sm)]
    hl = np.repeat(np.arange(heads), hd)
    ones_h = (hl[:, None] == hl[None, :]).astype(np.float32)
    hm = (np.arange(heads)[:, None] == hl[None, :]).astype(np.float32)
    pb = np.zeros((LANES, heads * LANES), np.float32)
    gl = np.arange(LANES)
    pb[gl, (gl % heads) * LANES + hd + gl // heads] = 1.0
    return tabs, (jnp.asarray(ones_h, BF16), jnp.asarray(hm, F32), jnp.asarray(pb, BF16))


def _moba_prep_block(heads, hd, t, xq, xk, xv, tabs, qg, kg, ones_ref, hm_ref, pexp_ref, kmt_ref,
                     qa_ref, ka_ref, write_vt):
    W = heads * hd
    half = hd // 8
    reps = W // LANES
    c, sp, sm = (jnp.concatenate([a] * reps, axis=1) for a in tabs)

    def norm_rot(x, gain):
        ms = _dot((x * x).astype(BF16), ones_ref[...]) * (1.0 / hd)
        y = x * lax.rsqrt(ms + NORM_EPS) * gain
        return y * c + pltpu.roll(y, half, 1) * sp + pltpu.roll(y, W - half, 1) * sm

    q = norm_rot(xq, qg)
    k = norm_rot(xk, kg)

    gate = _dot3(q, kmt_ref[...], nt=True)
    lane = lax.broadcasted_iota(jnp.int32, gate.shape, 1)
    blk = lax.shift_right_logical(lane, int(np.log2(heads)))
    valid = blk < t
    gate = jnp.where(valid, gate, -jnp.inf)
    rank = jnp.zeros(gate.shape, F32)
    nslots = LANES // heads
    for s in range(1, nslots):
        other = pltpu.roll(gate, s * heads, 1)
        ahead = (other > gate) | ((other == gate) & (blk >= s))
        rank = rank + ahead.astype(F32)
    sel = valid & (rank < float(MOBA_TOPK))
    bias = jnp.where(sel | (blk == t), 0.0, NEG)

    bias_slab = _dot(bias.astype(BF16), pexp_ref[...])
    qs = q * (hd ** -0.5 * LOG2E)
    lane1 = lax.broadcasted_iota(jnp.int32, (q.shape[0], LANES), 1)
    onehot = jnp.where(lane1 == hd + t, 1.0, 0.0)
    per_vreg = LANES // hd
    for h in range(heads):
        src = slice((h // per_vreg) * LANES, (h // per_vreg + 1) * LANES)
        dst = slice(h * LANES, (h + 1) * LANES)
        shift = (LANES - (h % per_vreg) * hd) % LANES
        move = (lambda a: a) if shift == 0 else (lambda a: pltpu.roll(a, shift, 1))
        qa_ref[:, dst] = jnp.where(lane1 < hd, move(qs[:, src]), bias_slab[:, dst]).astype(qa_ref.dtype)
        ka_ref[:, dst] = jnp.where(lane1 < hd, move(k[:, src]), onehot).astype(ka_ref.dtype)

    kmean = jnp.mean(k, axis=0, keepdims=True)
    row0 = pl.multiple_of(t * heads, heads)
    kmt_ref[pl.ds(row0, heads), :] = kmean * hm_ref[...]

    vt = xv.T
    vt_rows = _moba_vt_rows(hd)
    ones = jnp.ones((vt_rows - hd, vt.shape[1]), F32)
    for h in range(heads):
        write_vt(h, slice(0, hd), vt[h * hd:(h + 1) * hd])
        write_vt(h, slice(hd, vt_rows), ones)


def _moba_vt_rows(hd):
    return hd + 2 * SUBLANES


def _moba_attn_kernel(hd, qa_ref, ka_ref, vt_ref, o_ref, acc_ref, sa_ref, sb_ref):
    t = pl.program_id(1)
    tq = qa_ref.shape[0]
    blk = MOBA_BLOCK
    nh = vt_ref.shape[1]

    def scores_to(dst_ref, n, hh):
        cols = slice(hh * LANES, (hh + 1) * LANES)
        rows = pl.ds(pl.multiple_of(n * blk, blk), blk)
        dst_ref[hh] = _dot_nt(ka_ref[rows, cols], qa_ref[:, cols])

    def update(s, m, n, hh):
        m_new = jnp.maximum(m, jnp.max(s, axis=0, keepdims=True))
        alpha = jnp.exp2(m - m_new)
        p = jnp.exp2(s - m_new).astype(BF16)
        acc_ref[hh] = alpha * acc_ref[hh] + _dot(vt_ref[0, hh, n], p)
        return m_new

    def step(cur_ref, nxt_ref, n, ms):
        new = []
        for hh in range(nh):
            scores_to(nxt_ref, n + 1, hh)
            new.append(update(cur_ref[hh], ms[hh], n, hh))
        return tuple(new)

    acc_ref[...] = jnp.zeros_like(acc_ref)
    for hh in range(nh):
        scores_to(sa_ref, 0, hh)

    def pair(i, ms):
        ms = step(sa_ref, sb_ref, 2 * i, ms)
        return step(sb_ref, sa_ref, 2 * i + 1, ms)

    m0 = jnp.full((1, tq), 2 * NEG, F32)
    ms = lax.fori_loop(0, t // 2, pair, (m0,) * nh)

    def odd_block(ms):
        new = []
        for hh in range(nh):
            new.append(update(sa_ref[hh], ms[hh], t - 1, hh))
            scores_to(sa_ref, t, hh)
        return tuple(new)

    ms = lax.cond(t % 2 == 1, odd_block, lambda ms: ms, ms)

    key = lax.broadcasted_iota(jnp.int32, (blk, tq), 0)
    qry = lax.broadcasted_iota(jnp.int32, (blk, tq), 1)
    outs = []
    for hh in range(nh):
        s = jnp.where(key <= qry, sa_ref[hh], NEG)
        update(s, ms[hh], t, hh)
        a = acc_ref[hh]
        outs.append(a[0:hd] / a[hd:hd + 1])
    o_ref[...] = jnp.concatenate(outs, axis=0).T.astype(o_ref.dtype)


def _moba_attn(qa, ka, vt, batch, seq, heads, hd):
    nt = seq // MOBA_BLOCK
    vt_rows = vt.shape[3]
    return pl.pallas_call(
        functools.partial(_moba_attn_kernel, hd),
        grid=(batch, nt),
        in_specs=[pl.BlockSpec((MOBA_BLOCK, heads * LANES), lambda b, t: (b * nt + t, 0)),
                  pl.BlockSpec((seq, heads * LANES), lambda b, t: (b, 0)),
                  pl.BlockSpec((1, heads, nt, vt_rows, MOBA_BLOCK), lambda b, t: (b, 0, 0, 0, 0))],
        out_specs=pl.BlockSpec((MOBA_BLOCK, heads * hd), lambda b, t: (b * nt + t, 0)),
        out_shape=jax.ShapeDtypeStruct((batch * seq, heads * hd), BF16),
        scratch_shapes=[pltpu.VMEM((heads, vt_rows, MOBA_BLOCK), F32),
                        pltpu.VMEM((heads, MOBA_BLOCK, MOBA_BLOCK), F32),
                        pltpu.VMEM((heads, MOBA_BLOCK, MOBA_BLOCK), F32)],
        compiler_params=_cparams("parallel", "arbitrary"),
        name="moba_attn",
    )(qa, ka, vt)


def _out_proj_kernel(widths, x_ref, oa_ref, ob_ref, oc_ref, w_ref, g_ref, wr_ref,
                     x1_ref, hx_ref, route_ref):
    rows = x_ref.shape[0]
    hw = x_ref.shape[1]
    chunks = [slice(r, r + OUT_PROJ_CHUNK) for r in range(0, rows, OUT_PROJ_CHUNK)]
    accs = []
    for c in chunks:
        acc = x_ref[c, :]
        r0 = 0
        for o_ref, wd in zip((oa_ref, ob_ref, oc_ref), widths):
            acc = acc + _dot(o_ref[c, :], w_ref[r0:r0 + wd, :])
            r0 += wd
        x1_ref[c, :] = acc
        accs.append(acc)
    wr_hi, wr_lo = _hi_lo(wr_ref[...])
    wr_cat = jnp.concatenate([wr_hi, wr_lo, wr_hi], axis=0)
    all_logits = []
    for c, acc in zip(chunks, accs):
        ms = jnp.mean(acc * acc, axis=-1, keepdims=True)
        h2 = acc * lax.rsqrt(ms + NORM_EPS) * g_ref[...]
        hx_ref[c, 0:hw] = h2
        h_hi, h_lo = _hi_lo(h2)
        all_logits.append(_dot(jnp.concatenate([h_hi, h_hi, h_lo], axis=1), wr_cat))
    for c, logits in zip(chunks, all_logits):
        route = _route(logits)
        hx_ref[c, hw:hw + LANES] = route
        route_ref[c, :] = route


def _route(logits):
    G, E = MOE_GROUPS, MOE_EXPERTS_PER_GROUP
    lane = lax.broadcasted_iota(jnp.int32, logits.shape, 1).astype(F32)
    big = float(LANES)
    is_g = lane < G
    lg = jnp.where(is_g, logits, -jnp.inf)
    mg = jnp.max(lg, axis=-1, keepdims=True)
    gate_group = 1.0 / jnp.sum(jnp.exp(lg - mg), axis=-1, keepdims=True)
    g_sel = jnp.min(jnp.where(lg == mg, lane, big), axis=-1, keepdims=True)
    lo = G + E * g_sel
    in_grp = (lane >= lo) & (lane < lo + E)
    v = jnp.where(in_grp, logits, -jnp.inf)
    v1 = jnp.max(v, axis=-1, keepdims=True)
    i1 = jnp.min(jnp.where(v == v1, lane, big), axis=-1, keepdims=True)
    v = jnp.where(lane == i1, -jnp.inf, v)
    v2 = jnp.max(v, axis=-1, keepdims=True)
    i2 = jnp.min(jnp.where(v == v2, lane, big), axis=-1, keepdims=True)
    e2 = jnp.exp(v2 - v1)
    w1 = gate_group / (1.0 + e2)
    w2 = gate_group * e2 / (1.0 + e2)
    first = i1 < i2
    a = jnp.minimum(i1, i2) - lo
    b = jnp.maximum(i1, i2) - lo
    w_lo = jnp.where(first, w1, w2)
    w_hi = jnp.where(first, w2, w1)
    pos = jnp.zeros_like(a)
    swap = a < 0.0
    for p, (ea, eb) in enumerate(_moe_pair_order()):
        hit = (a == float(min(ea, eb))) & (b == float(max(ea, eb)))
        pos = jnp.where(hit, float(p), pos)
        if ea > eb:
            swap = swap | hit
    cls = g_sel * float(E * (E - 1) // 2) + pos
    route = jnp.where(lane == 0.0, cls, 0.0)
    route = jnp.where(lane == 1.0, jnp.where(swap, w_hi, w_lo), route)
    return jnp.where(lane == 2.0, jnp.where(swap, w_lo, w_hi), route)


def _out_proj(x, oa, ob, oc, w, g, wr):
    n, d = x.shape
    tm = ROW_TILE
    widths = (oa.shape[1], ob.shape[1], oc.shape[1])
    row = lambda width: pl.BlockSpec((tm, width), lambda i: (i, 0))
    full = lambda a: pl.BlockSpec(a.shape, lambda i: (0, 0))
    g2 = g.reshape(1, d)
    hxw = d + LANES
    return pl.pallas_call(
        functools.partial(_out_proj_kernel, widths),
        grid=(n // tm,),
        in_specs=[row(d), row(widths[0]), row(widths[1]), row(widths[2]), full(w), full(g2), full(wr)],
        out_specs=[row(d), row(hxw), row(LANES)],
        out_shape=[jax.ShapeDtypeStruct((n, d), F32), jax.ShapeDtypeStruct((n, hxw), F32),
                   jax.ShapeDtypeStruct((n, LANES), F32)],
        compiler_params=_cparams("parallel"),
        name="out_proj",
    )(x, oa, ob, oc, w, g2, wr)


@functools.lru_cache(maxsize=None)
def _moe_pair_order():
    E = MOE_EXPERTS_PER_GROUP
    todo = {(a, b) for a in range(E) for b in range(a + 1, E)}

    def extend(seq, todo):
        if not todo:
            return seq
        for pr in sorted(todo):
            for cand in (pr, pr[::-1]):
                if not seq or cand[0] == seq[-1][0] or cand[1] == seq[-1][1]:
                    out = extend(seq + [cand], todo - {pr})
                    if out:
                        return out
        return None

    return tuple(extend([], todo))


def _moe_classes():
    G, E = MOE_GROUPS, MOE_EXPERTS_PER_GROUP
    ea = [g * E + a for g in range(G) for a, _ in _moe_pair_order()]
    eb = [g * E + b for g in range(G) for _, b in _moe_pair_order()]
    return ea, eb


def _moe_plan_kernel(elo, ehi, cls_ref, dest_ref, tab_ref):
    R = MOE_TILE
    cls = cls_ref[...]
    rows = cls.shape[0]
    ri = lax.broadcasted_iota(jnp.int32, (LANES, LANES), 0)
    ci = lax.broadcasted_iota(jnp.int32, (LANES, LANES), 1)
    before_lane = (ri < ci).astype(BF16)
    all_lane = jnp.ones((LANES, LANES), BF16)
    rr = lax.broadcasted_iota(jnp.int32, (rows, rows), 0)
    rc = lax.broadcasted_iota(jnp.int32, (rows, rows), 1)
    before_row = (rc < rr).astype(BF16)
    all_row = jnp.ones((rows, rows), BF16)
    tile_row0 = lax.broadcasted_iota(jnp.int32, (8, LANES), 1).astype(F32) * float(R)
    off = jnp.zeros((rows, LANES), F32)
    dest = jnp.zeros((rows, LANES), F32)
    tile_cls = jnp.zeros((8, LANES), F32)
    partial = jnp.zeros((8, LANES), F32)
    for k in range(len(elo)):
        mask = (cls == k).astype(BF16)
        row_tot = _dot(mask, all_lane)
        rank = _dot(mask, before_lane) + _dot(before_row, row_tot.astype(BF16))
        count = _dot(all_row, row_tot.astype(BF16))
        dest = dest + mask.astype(F32) * (off + rank)
        start = off
        off = off + jnp.floor((count + float(R - 1)) * (1.0 / R)) * float(R)
        tile_cls = tile_cls + jnp.where(tile_row0 >= off[0:8], 1.0, 0.0)
        partial = jnp.where((tile_row0 == off[0:8] - float(R)) & (off[0:8] > start[0:8]), 1.0, partial)
    partial = jnp.where(tile_row0 >= off[0:8], 1.0, partial)
    dest_ref[...] = dest.astype(jnp.int32)
    tile_cls = jnp.minimum(tile_cls, float(len(elo) - 1))
    t_lo = jnp.zeros((8, LANES), F32)
    t_hi = jnp.zeros((8, LANES), F32)
    for k in range(len(elo)):
        t_lo = jnp.where(tile_cls == k, float(elo[k]), t_lo)
        t_hi = jnp.where(tile_cls == k, float(ehi[k]), t_hi)
    sub = lax.broadcasted_iota(jnp.int32, (8, LANES), 0)
    tab = jnp.where(sub == 0, t_lo, jnp.where(sub == 1, t_hi, jnp.where(sub == 2, off[0:8] * (1.0 / R), partial)))
    tab_ref[...] = tab.astype(jnp.int32)


def _moe_plan(cls):
    elo, ehi = _moe_classes()
    rows = cls.shape[0]
    return pl.pallas_call(
        functools.partial(_moe_plan_kernel, elo, ehi),
        out_shape=[jax.ShapeDtypeStruct((rows, LANES), jnp.int32),
                   jax.ShapeDtypeStruct((8, LANES), jnp.int32)],
        name="moe_plan",
    )(cls)


def _row_copies(n, make):
    def start(g, c):
        base = pl.multiple_of(g * SUBLANES, SUBLANES)
        for s in range(SUBLANES):
            make(base + s).start()
        return c

    def wait(g, c):
        base = pl.multiple_of(g * SUBLANES, SUBLANES)
        for s in range(SUBLANES):
            make(base + s).wait()
        return c

    lax.fori_loop(0, n // SUBLANES, start, 0)
    lax.fori_loop(0, n // SUBLANES, wait, 0)


def _moe_dispatch_kernel(partial_ref, dest_ref, hx_ref, xs_ref, zero_ref, zsem, sem):
    R = zero_ref.shape[0]

    @pl.when(pl.program_id(0) == 0)
    def _():
        zero_ref[...] = jnp.zeros_like(zero_ref)
        fill = lambda j: pltpu.make_async_copy(zero_ref, xs_ref.at[pl.ds(j * R, R)], zsem)
        for j in range(xs_ref.shape[0] // R):
            @pl.when(partial_ref[j] == 1)
            def _():
                fill(j).start()
        for j in range(xs_ref.shape[0] // R):
            @pl.when(partial_ref[j] == 1)
            def _():
                fill(j).wait()

    _row_copies(hx_ref.shape[0], lambda r: pltpu.make_async_copy(
        hx_ref.at[pl.ds(r, 1)], xs_ref.at[pl.ds(dest_ref[r], 1)], sem))


def _moe_dispatch(partial, dest, hx, n_rows):
    n, w = hx.shape
    tm = ROW_TILE
    return pl.pallas_call(
        _moe_dispatch_kernel,
        grid=(n // tm,),
        in_specs=[pl.BlockSpec(partial.shape, lambda i: (0,), memory_space=pltpu.SMEM),
                  pl.BlockSpec((tm,), lambda i: (i,), memory_space=pltpu.SMEM),
                  pl.BlockSpec((tm, w), lambda i: (i, 0))],
        out_specs=pl.BlockSpec(memory_space=pl.ANY),
        out_shape=jax.ShapeDtypeStruct((n_rows, w), hx.dtype),
        scratch_shapes=[pltpu.VMEM((MOE_TILE, w), hx.dtype), pltpu.SemaphoreType.DMA,
                        pltpu.SemaphoreType.DMA],
        compiler_params=_cparams("arbitrary"),
        name="moe_dispatch",
    )(partial, dest, hx)


def _moe_experts_kernel(elo_ref, ehi_ref, nt_ref, xs_ref, wg0, wu0, wd0, wg1, wu1, wd1, y_ref,
                        wgu_ref, wdn_ref):
    j = pl.program_id(0)
    hw = xs_ref.shape[1] - LANES
    jp = jnp.maximum(j - 1, 0)
    fresh = (j == 0) | (elo_ref[j] != elo_ref[jp]) | (ehi_ref[j] != ehi_ref[jp])

    @pl.when((j < nt_ref[0]) & fresh)
    def _():
        for slot, (wg, wu, wd) in enumerate(((wg0, wu0, wd0), (wg1, wu1, wd1))):
            wgu_ref[slot, 0] = wg[0].astype(BF16)
            wgu_ref[slot, 1] = wu[0].astype(BF16)
            wdn_ref[slot] = wd[0].astype(BF16)

    @pl.when(j < nt_ref[0])
    def _():
        x = xs_ref[:, 0:hw].astype(BF16)
        route = xs_ref[:, hw:hw + LANES]
        lane = lax.broadcasted_iota(jnp.int32, route.shape, 1)
        y = None
        for slot in range(2):
            wt = jnp.sum(jnp.where(lane == 1 + slot, route, 0.0), axis=-1, keepdims=True)
            a = _dot(x, wgu_ref[slot, 0])
            u = _dot(x, wgu_ref[slot, 1])
            he = (a * _sigmoid(a)) * u * wt
            part = _dot(he.astype(BF16), wdn_ref[slot])
            y = part if y is None else y + part
        y_ref[...] = y

    @pl.when(j >= nt_ref[0])
    def _():
        y_ref[...] = jnp.zeros_like(y_ref)


def _moe_experts(tab, xs, wg, wu, wd, layer):
    n_rows, w = xs.shape
    _, ne, d, ff = wg.shape
    R = MOE_TILE
    last = lambda j, nt: jnp.minimum(j, nt[0] - 1)
    wspec = lambda shape, which: pl.BlockSpec(
        (None,) + shape, lambda j, elo, ehi, nt: (layer, (elo, ehi)[which][last(j, nt)], 0, 0))
    grid_spec = pltpu.PrefetchScalarGridSpec(
        num_scalar_prefetch=3,
        grid=(n_rows // R,),
        in_specs=[pl.BlockSpec((R, w), lambda j, elo, ehi, nt: (last(j, nt), 0)),
                  wspec((1, d, ff), 0), wspec((1, d, ff), 0), wspec((1, ff, d), 0),
                  wspec((1, d, ff), 1), wspec((1, d, ff), 1), wspec((1, ff, d), 1)],
        out_specs=pl.BlockSpec((R, d), lambda j, elo, ehi, nt: (j, 0)),
        scratch_shapes=[pltpu.VMEM((2, 2, d, ff), BF16), pltpu.VMEM((2, ff, d), BF16)],
    )
    return pl.pallas_call(
        _moe_experts_kernel,
        grid_spec=grid_spec,
        out_shape=jax.ShapeDtypeStruct((n_rows, d), F32),
        compiler_params=_cparams("arbitrary"),
        name="moe_experts",
    )(tab[0], tab[1], tab[2, 0:1], xs, wg, wu, wd, wg, wu, wd)


def _moe_combine_kernel(dest_ref, x1_ref, y_ref, o_ref, buf_ref, sem):
    _row_copies(buf_ref.shape[0], lambda r: pltpu.make_async_copy(
        y_ref.at[pl.ds(dest_ref[r], 1)], buf_ref.at[pl.ds(r, 1)], sem))
    o_ref[...] = x1_ref[...] + buf_ref[...]


def _moe_combine(dest, x1, y):
    n, d = x1.shape
    tm = ROW_TILE
    return pl.pallas_call(
        _moe_combine_kernel,
        grid=(n // tm,),
        in_specs=[pl.BlockSpec((tm,), lambda i: (i,), memory_space=pltpu.SMEM),
                  pl.BlockSpec((tm, d), lambda i: (i, 0)),
                  pl.BlockSpec(memory_space=pl.ANY)],
        out_specs=pl.BlockSpec((tm, d), lambda i: (i, 0)),
        out_shape=jax.ShapeDtypeStruct((n, d), F32),
        scratch_shapes=[pltpu.VMEM((tm, d), F32), pltpu.SemaphoreType.DMA],
        compiler_params=_cparams("arbitrary"),
        name="moe_combine",
    )(dest, x1, y)


def _moe(x1, hx, route, wg, wu, wd, layer):
    n, d = x1.shape
    elo, _ = _moe_classes()
    n_rows = n + len(elo) * MOE_TILE
    assert n % LANES == 0 and n_rows // MOE_TILE <= LANES
    cls = route[:, 0].astype(jnp.int32).reshape(n // LANES, LANES)
    dest, tab = _moe_plan(cls)
    dest = dest.reshape(n)
    xs = _moe_dispatch(tab[3], dest, hx, n_rows)
    y = _moe_experts(tab, xs, wg, wu, wd, layer)
    return _moe_combine(dest, x1, y)


def _pad_cols(w, width):
    return jnp.pad(w, ((0, 0), (0, width - w.shape[1])))


def kernel(x, attn_norm_g, w_in, gla_gk_w2, gla_gk_b, gla_norm_g, moba_qnorm_g, moba_knorm_g,
           hgrn_lb_param, hgrn_norm_g, w_out, ffn_norm_g, w_router_group, w_router_expert,
           w_exp_gate, w_exp_up, w_exp_down):
    batch, seq, d = x.shape
    depth = w_in.shape[0]
    lowrank, gla_kw = gla_gk_w2.shape[1:]
    gla_dv = gla_norm_g.shape[1]
    moba_hd = moba_qnorm_g.shape[1]
    hgrn_kw = hgrn_lb_param.shape[1]
    hgrn_dv = hgrn_norm_g.shape[1]
    mix_w = w_out.shape[1]
    hgrn_heads = 4
    gla_heads = 4
    gla_vw = gla_heads * gla_dv
    hgrn_vw = hgrn_heads * hgrn_dv
    moba_w = mix_w - gla_vw - hgrn_vw
    moba_heads = moba_w // moba_hd
    gla_dk = gla_kw // gla_heads
    hgrn_dk = hgrn_kw // hgrn_heads

    splits = (gla_kw, gla_kw, gla_vw, gla_vw, lowrank, moba_w, moba_w, moba_w,
              hgrn_kw, hgrn_kw, hgrn_vw, hgrn_vw)
    offs = np.concatenate([[0], np.cumsum(splits)]).tolist()
    order = (0, 1, 2, 3, 4, 5, 6, 7, 8, 10, 11, 9)
    pieces, dst = [], 0
    for i in order:
        padded = -(-splits[i] // LANES) * LANES
        pieces.append((offs[i], splits[i], dst, padded))
        dst += padded
    g_gla = 2 * gla_kw + 2 * gla_vw
    g_hg = hgrn_kw + 2 * hgrn_vw
    bounds = np.cumsum([0, g_gla, LANES, 3 * moba_w, g_hg, hgrn_kw]).tolist()
    spans = [(bounds[i], bounds[i + 1]) for i in range(5)]
    groups = (spans[0], spans[1], spans[3], spans[4])
    assert dst == bounds[-1]

    w_in_t = jnp.swapaxes(w_in, 1, 2)
    xf = x.reshape(batch * seq, d)
    for l in range(depth):
        y_gla, y_lr, y_hg, y_f, qa, ka, vt = _in_proj(
            xf, attn_norm_g[l], w_in_t, l, tuple(pieces), groups, (BF16, F32, BF16, F32), spans[2],
            moba_qnorm_g[l], moba_knorm_g[l], batch, seq, moba_heads, moba_hd)

        w2 = jnp.pad(gla_gk_w2[l], ((0, LANES - lowrank), (0, 0)))
        o_a, o_c = _lin_call(
            batch, seq, l, (gla_heads, gla_dk, gla_dv), (hgrn_heads, hgrn_dk, hgrn_dv),
            (y_gla, y_lr, y_hg, y_f),
            (w2, gla_gk_b[l].reshape(1, gla_kw), jnp.tile(gla_norm_g[l], gla_heads).reshape(1, gla_vw),
             hgrn_lb_param, jnp.tile(hgrn_norm_g[l], hgrn_heads).reshape(1, hgrn_vw)))

        o_b = _moba_attn(qa, ka, vt, batch, seq, moba_heads, moba_hd)

        wr = _pad_cols(jnp.concatenate([w_router_group[l], w_router_expert[l]], axis=1), LANES)
        x1, hx, route = _out_proj(xf, o_a, o_b, o_c, w_out[l].astype(BF16), ffn_norm_g[l], wr)
        xf = _moe(x1, hx, route, w_exp_gate, w_exp_up, w_exp_down, l)
    return xf.reshape(batch, seq, d)
```

```python
import functools

import numpy as np
import jax
import jax.numpy as jnp
from jax import lax
from jax.experimental import pallas as pl
from jax.experimental.pallas import tpu as pltpu

F32 = jnp.float32
BF16 = jnp.bfloat16

NORM_EPS = 1e-6
GLA_GATE_TAU = 16.0
ROPE_THETA = 500000.0
MOBA_BLOCK = 256
MOBA_TOPK = 3
MOE_GROUPS = 4
MOE_EXPERTS_PER_GROUP = 4
MOE_TOPK = 2

LANES = 128
SUBLANES = 8
VMEM_LIMIT = 56 * 1024 * 1024
LIN_CHUNK = 64
LIN_TILE = 512
ROW_TILE = 512
OUT_PROJ_CHUNK = 128
MOE_TILE = 256
MOE_COPY_ROWS = 1024
NEG = -1e30
LOG2E = 1.4426950408889634


def _cparams(*sem):
    return pltpu.CompilerParams(dimension_semantics=sem, vmem_limit_bytes=VMEM_LIMIT)


def _sigmoid(x):
    return 1.0 / (1.0 + jnp.exp(-x))


def _dot(a, b):
    return jnp.dot(a, b, preferred_element_type=F32)


def _dot_nt(a, b):
    return lax.dot_general(a, b, (((1,), (1,)), ((), ())), preferred_element_type=F32)


def _dot_tn(a, b):
    return lax.dot_general(a, b, (((0,), (0,)), ((), ())), preferred_element_type=F32)


def _hi_lo(x):
    hi = x.astype(BF16)
    return hi, (x - hi.astype(F32)).astype(BF16)


def _dot3(a, b, nt=False):
    a_hi, a_lo = _hi_lo(a)
    b_hi, b_lo = _hi_lo(b)
    lhs = jnp.concatenate([a_hi, a_hi, a_lo], axis=1)
    if nt:
        return _dot_nt(lhs, jnp.concatenate([b_hi, b_lo, b_hi], axis=1))
    return _dot(lhs, jnp.concatenate([b_hi, b_lo, b_hi], axis=0))


def _in_proj_kernel(pieces, groups, moba_cols, heads, hd, tiles_per_seq,
                    x_ref, g_ref, w_ref, qg_ref, kg_ref, c_ref, sp_ref, sm_ref,
                    ones_ref, hm_ref, pexp_ref, *refs):
    out_refs = refs[:len(groups)]
    qa_ref, ka_ref, vt_ref, wcat_ref, kmt_ref = refs[len(groups):]
    i = pl.program_id(0)

    @pl.when(i == 0)
    def _():
        for src, width, dst, padded in pieces:
            blk = w_ref[src:src + width, :].astype(BF16)
            if padded > width:
                blk = jnp.concatenate([blk, jnp.zeros((padded - width, blk.shape[1]), BF16)], axis=0)
            wcat_ref[dst:dst + padded, :] = blk

    tile = lax.rem(i, tiles_per_seq)

    @pl.when(tile == 0)
    def _():
        kmt_ref[...] = jnp.zeros_like(kmt_ref)

    x = x_ref[...]
    ms = jnp.mean(x * x, axis=-1, keepdims=True)
    h = (x * lax.rsqrt(ms + NORM_EPS) * g_ref[...]).astype(BF16)
    ym = _dot_nt(h, wcat_ref[moba_cols[0]:moba_cols[1], :])
    for (a, b), o_ref in zip(groups, out_refs):
        o_ref[...] = _dot_nt(h, wcat_ref[a:b, :]).astype(o_ref.dtype)

    W = heads * hd
    blocks = x.shape[0] // MOBA_BLOCK
    for blk in range(blocks):
        rows = slice(blk * MOBA_BLOCK, (blk + 1) * MOBA_BLOCK)

        def write_vt(hh, vrows, value, blk=blk):
            vt_ref[0, hh, blk, vrows, :] = value.astype(vt_ref.dtype)

        _moba_prep_block(
            heads, hd, tile * blocks + blk, ym[rows, 0:W], ym[rows, W:2 * W], ym[rows, 2 * W:3 * W],
            (c_ref[rows, :], sp_ref[rows, :], sm_ref[rows, :]), qg_ref[...], kg_ref[...],
            ones_ref, hm_ref, pexp_ref, kmt_ref, qa_ref.at[rows], ka_ref.at[rows], write_vt)


def _in_proj(x, g, w_all_t, layer, pieces, groups, dtypes, moba_cols, qg, kg, batch, seq, heads, hd):
    n, d = x.shape
    tm = ROW_TILE
    cols = w_all_t.shape[1]
    W = heads * hd
    nt = seq // MOBA_BLOCK
    blocks = tm // MOBA_BLOCK
    tiles_per_seq = seq // tm
    tabs, consts = _moba_constants(heads, hd, seq)
    tile_g = lambda v: jnp.tile(v.astype(F32), heads).reshape(1, W)
    row = lambda width: pl.BlockSpec((tm, width), lambda i: (i, 0))
    full = lambda a: pl.BlockSpec(a.shape, lambda i: (0,) * a.ndim)
    wide_sd = jax.ShapeDtypeStruct((n, heads * LANES), BF16)
    out_shape = [jax.ShapeDtypeStruct((n, b - a), dt) for (a, b), dt in zip(groups, dtypes)]
    vt_rows = _moba_vt_rows(hd)
    out_shape += [wide_sd, wide_sd, jax.ShapeDtypeStruct((batch, heads, nt, vt_rows, MOBA_BLOCK), BF16)]
    out_specs = [row(b - a) for (a, b) in groups] + [row(heads * LANES), row(heads * LANES)]
    out_specs.append(pl.BlockSpec((1, heads, blocks, vt_rows, MOBA_BLOCK),
                                  lambda i: (i // tiles_per_seq, 0, i % tiles_per_seq, 0, 0)))
    return pl.pallas_call(
        functools.partial(_in_proj_kernel, pieces, groups, moba_cols, heads, hd, tiles_per_seq),
        grid=(n // tm,),
        in_specs=[row(d),
                  pl.BlockSpec((1, d), lambda i: (0, 0)),
                  pl.BlockSpec((None, cols, d), lambda i: (layer, 0, 0), pipeline_mode=pl.Buffered(1)),
                  full(tile_g(qg)), full(tile_g(kg))]
        + [pl.BlockSpec((tm, LANES), lambda i: (i % tiles_per_seq, 0)) for _ in tabs]
        + [full(a) for a in consts],
        out_specs=out_specs,
        out_shape=out_shape,
        scratch_shapes=[pltpu.VMEM((pieces[-1][2] + pieces[-1][3], d), BF16),
                        pltpu.VMEM((LANES, W), F32)],
        compiler_params=_cparams("arbitrary"),
        name="in_proj",
    )(x, g.reshape(1, d), w_all_t, tile_g(qg), tile_g(kg), *tabs, *consts)


def _lin_constants(heads, dk, dv):
    L = LIN_CHUNK
    nlev = int(np.log2(L))
    K, V = heads * dk, heads * dv
    i = np.arange(L)[:, None]
    t = np.arange(L)[None, :]
    w_rows = [(t <= i), (t > i)]
    masks = []
    for lev in range(nlev):
        b = L >> lev
        half = b // 2
        r = (i // b) * b + half - 1
        w_rows.append((t > np.minimum(i, r)) & (t <= np.maximum(i, r)))
        j = t
        masks.append((i // b == j // b) & (i % b >= half) & (j % b < half))
    masks.append(i == t)
    w_all = np.concatenate(w_rows, axis=0).astype(np.float32)
    m_all = np.stack([np.tile(m, (1, heads)) for m in masks]).astype(np.float32)
    rh = np.repeat(np.arange(heads), L)[:, None]
    bdk = (rh == np.repeat(np.arange(heads), dk)[None, :]).astype(np.float32)
    bdv = (rh == np.repeat(np.arange(heads), dv)[None, :]).astype(np.float32)
    bds = (np.repeat(np.arange(heads), dv)[:, None]
           == np.repeat(np.arange(heads), dk)[None, :]).astype(np.float32)
    ones_v = (np.repeat(np.arange(heads), dv)[:, None]
              == np.repeat(np.arange(heads), dv)[None, :]).astype(np.float32)
    return (jnp.asarray(w_all, BF16), jnp.asarray(m_all, F32), jnp.asarray(bdk, BF16),
            jnp.asarray(bdv, BF16), jnp.asarray(bds, F32), jnp.asarray(ones_v, BF16))


def _lin_groups(probs):
    L = LIN_CHUNK
    items = []
    for p in probs:
        p["lacat"] = jnp.concatenate(_hi_lo(p["la"]), axis=1)
        p["v_bf"] = p["v"].astype(BF16)
        p["chunks"] = [slice(c * L, (c + 1) * L) for c in range(p["q"].shape[0] // L)]
    for p in probs:
        K = p["q"].shape[1]
        for c in p["chunks"]:
            z2 = _dot(p["w_ref"][...], p["lacat"][c])
            items.append((p, c, jnp.exp(z2[:, :K] + z2[:, K:])))
    nlev = probs[0]["m_ref"].shape[0] - 1
    scores = [None] * len(items)
    for lev in range(nlev + 1):
        for i, (p, c, e) in enumerate(items):
            q, k = p["q"], p["k"]
            if lev < nlev:
                el = e[(2 + lev) * L:(3 + lev) * L]
                ql = (q[c] * el).astype(BF16)
                kl = (k[c] * el).astype(BF16)
            else:
                ql = q[c].astype(BF16)
                kl = k[c].astype(BF16)
            kbd = jnp.concatenate([kl] * p["heads"], axis=0) * p["bdk_ref"][...]
            s = _dot_nt(ql, kbd) * p["m_ref"][lev]
            scores[i] = s if scores[i] is None else scores[i] + s
    intra, kvs = [], []
    for i, (p, c, e) in enumerate(items):
        vbd = jnp.concatenate([p["v_bf"][c]] * p["heads"], axis=0) * p["bdv_ref"][...]
        intra.append(_dot(scores[i].astype(BF16), vbd))
        kb = (p["k"][c] * e[L:2 * L]).astype(BF16)
        kvs.append(_dot_tn(p["v_bf"][c], kb) * p["bds_ref"][...])
    results = []
    i = 0
    for p in probs:
        st = p["st_ref"][...]
        outs = []
        for c in p["chunks"]:
            e = items[i][2]
            outs.append(intra[i] + _dot_nt((p["q"][c] * e[0:L]).astype(BF16), st.astype(BF16)))
            st = st * e[L - 1:L] + kvs[i]
            i += 1
        p["st_ref"][...] = st
        o = jnp.concatenate(outs, axis=0)
        ms = _dot((o * o).astype(BF16), p["ones_ref"][...]) * (1.0 / p["dv"])
        gf = p["g"].astype(F32)
        results.append(o * lax.rsqrt(ms + NORM_EPS) * p["gain"] * (gf * _sigmoid(gf)))
    return results


def _lin_kernel(gla_dims, hgrn_dims, layer, xa_ref, lr_ref, xc_ref, f_ref,
                w2_ref, b_ref, gain_a_ref, lbp_ref, gain_c_ref, *refs):
    const_a, const_c = refs[0:6], refs[6:12]
    oa_ref, oc_ref, sta_ref, stc_ref = refs[12:]

    @pl.when(pl.program_id(1) == 0)
    def _():
        sta_ref[...] = jnp.zeros_like(sta_ref)
        stc_ref[...] = jnp.zeros_like(stc_ref)

    def problem(consts, st_ref, heads, dv, **values):
        names = ("w_ref", "m_ref", "bdk_ref", "bdv_ref", "bds_ref", "ones_ref")
        return dict(zip(names, consts), st_ref=st_ref, heads=heads, dv=dv, **values)

    heads, dk, dv = gla_dims
    K, V = heads * dk, heads * dv
    gk = _dot3(lr_ref[...], w2_ref[...]) + b_ref[...]
    gla = problem(
        const_a, sta_ref, heads, dv,
        q=xa_ref[:, 0:K].astype(F32) * (dk ** -0.5), k=xa_ref[:, K:2 * K].astype(F32),
        v=xa_ref[:, 2 * K:2 * K + V], g=xa_ref[:, 2 * K + V:2 * K + 2 * V], gain=gain_a_ref[...],
        la=(jnp.minimum(gk, 0.0) - jnp.log1p(jnp.exp(-jnp.abs(gk)))) * (1.0 / GLA_GATE_TAU))

    heads, dk, dv = hgrn_dims
    K, V = heads * dk, heads * dv
    lbp = lbp_ref[...]
    depth = lbp.shape[0]
    mx = lbp[0:1]
    for r in range(1, depth):
        mx = jnp.maximum(mx, lbp[r:r + 1])
    ex = [jnp.exp(lbp[r:r + 1] - mx) for r in range(depth)]
    den = ex[0]
    for r in range(1, depth):
        den = den + ex[r]
    lb = jnp.zeros_like(den)
    for r in range(1, layer + 1):
        lb = lb + ex[r] / den
    cq = xc_ref[:, 0:K].astype(F32)
    f = f_ref[...]
    hgrn = problem(
        const_c, stc_ref, heads, dv,
        q=cq * _sigmoid(cq), k=(1.0 - lb) * _sigmoid(-f), v=xc_ref[:, K:K + V],
        g=xc_ref[:, K + V:K + 2 * V], gain=gain_c_ref[...],
        la=jnp.log(lb + (1.0 - lb) * _sigmoid(f)))

    o_a, o_c = _lin_groups([gla, hgrn])
    oa_ref[...] = o_a.astype(oa_ref.dtype)
    oc_ref[...] = o_c.astype(oc_ref.dtype)


def _lin_call(batch, seq, layer, gla_dims, hgrn_dims, row_inputs, small_inputs):
    T = LIN_TILE
    nt = seq // T
    consts = _lin_constants(*gla_dims) + _lin_constants(*hgrn_dims)
    vw_a, vw_c = gla_dims[0] * gla_dims[2], hgrn_dims[0] * hgrn_dims[2]
    row = lambda width: pl.BlockSpec((T, width), lambda b, t: (b * nt + t, 0))
    full = lambda a: pl.BlockSpec(a.shape, lambda b, t: (0,) * a.ndim)
    return pl.pallas_call(
        functools.partial(_lin_kernel, gla_dims, hgrn_dims, layer),
        grid=(batch, nt),
        in_specs=[row(a.shape[1]) for a in row_inputs] + [full(a) for a in small_inputs]
        + [full(a) for a in consts],
        out_specs=[row(vw_a), row(vw_c)],
        out_shape=[jax.ShapeDtypeStruct((batch * seq, vw_a), BF16),
                   jax.ShapeDtypeStruct((batch * seq, vw_c), BF16)],
        scratch_shapes=[pltpu.VMEM((vw_a, gla_dims[0] * gla_dims[1]), F32),
                        pltpu.VMEM((vw_c, hgrn_dims[0] * hgrn_dims[1]), F32)],
        compiler_params=_cparams("parallel", "arbitrary"),
        name="lin_attn",
    )(*row_inputs, *small_inputs, *consts)


def _moba_constants(heads, hd, seq):
    W = heads * hd
    nb = LANES // heads
    rot = hd // 4
    half = rot // 2
    inv_freq = ROPE_THETA ** (-(np.arange(0, rot, 2, dtype=np.float64) / rot))
    ang = np.arange(seq, dtype=np.float64)[:, None] * inv_freq[None, :]
    cos, sin = np.cos(ang), np.sin(ang)
    c = np.ones((seq, hd), np.float32)
    sp = np.zeros((seq, hd), np.float32)
    sm = np.zeros((seq, hd), np.float32)
    c[:, :half] = cos
    c[:, half:rot] = cos
    sm[:, :half] = -sin
    sp[:, half:rot] = sin
    reps = LANES // hd
    tabs = [jnp.asarray(np.tile(a, (1, reps)), F32) for a in (c, sp, sm)]
    hl = np.repeat(np.arange(heads), hd)
    ones_h = (hl[:, None] == hl[None, :]).astype(np.float32)
    hm = (np.arange(heads)[:, None] == hl[None, :]).astype(np.float32)
    pb = np.zeros((LANES, heads * LANES), np.float32)
    gl = np.arange(LANES)
    pb[gl, (gl % heads) * LANES + hd + gl // heads] = 1.0
    return tabs, (jnp.asarray(ones_h, BF16), jnp.asarray(hm, F32), jnp.asarray(pb, BF16))


def _moba_prep_block(heads, hd, t, xq, xk, xv, tabs, qg, kg, ones_ref, hm_ref, pexp_ref, kmt_ref,
                     qa_ref, ka_ref, write_vt):
    W = heads * hd
    half = hd // 8
    reps = W // LANES
    c, sp, sm = (jnp.concatenate([a] * reps, axis=1) for a in tabs)

    def norm_rot(x, gain):
        ms = _dot((x * x).astype(BF16), ones_ref[...]) * (1.0 / hd)
        y = x * lax.rsqrt(ms + NORM_EPS) * gain
        return y * c + pltpu.roll(y, half, 1) * sp + pltpu.roll(y, W - half, 1) * sm

    q = norm_rot(xq, qg)
    k = norm_rot(xk, kg)

    gate = _dot3(q, kmt_ref[...], nt=True)
    lane = lax.broadcasted_iota(jnp.int32, gate.shape, 1)
    blk = lax.shift_right_logical(lane, int(np.log2(heads)))
    valid = blk < t
    gate = jnp.where(valid, gate, -jnp.inf)
    rank = jnp.zeros(gate.shape, F32)
    nslots = LANES // heads
    for s in range(1, nslots):
        other = pltpu.roll(gate, s * heads, 1)
        ahead = (other > gate) | ((other == gate) & (blk >= s))
        rank = rank + ahead.astype(F32)
    sel = valid & (rank < float(MOBA_TOPK))
    bias = jnp.where(sel | (blk == t), 0.0, NEG)

    bias_slab = _dot(bias.astype(BF16), pexp_ref[...])
    qs = q * (hd ** -0.5 * LOG2E)
    lane1 = lax.broadcasted_iota(jnp.int32, (q.shape[0], LANES), 1)
    onehot = jnp.where(lane1 == hd + t, 1.0, 0.0)
    per_vreg = LANES // hd
    for h in range(heads):
        src = slice((h // per_vreg) * LANES, (h // per_vreg + 1) * LANES)
        dst = slice(h * LANES, (h + 1) * LANES)
        shift = (LANES - (h % per_vreg) * hd) % LANES
        move = (lambda a: a) if shift == 0 else (lambda a: pltpu.roll(a, shift, 1))
        qa_ref[:, dst] = jnp.where(lane1 < hd, move(qs[:, src]), bias_slab[:, dst]).astype(qa_ref.dtype)
        ka_ref[:, dst] = jnp.where(lane1 < hd, move(k[:, src]), onehot).astype(ka_ref.dtype)

    kmean = jnp.mean(k, axis=0, keepdims=True)
    row0 = pl.multiple_of(t * heads, heads)
    kmt_ref[pl.ds(row0, heads), :] = kmean * hm_ref[...]

    vt = xv.T
    vt_rows = _moba_vt_rows(hd)
    ones = jnp.ones((vt_rows - hd, vt.shape[1]), F32)
    for h in range(heads):
        write_vt(h, slice(0, hd), vt[h * hd:(h + 1) * hd])
        write_vt(h, slice(hd, vt_rows), ones)


def _moba_vt_rows(hd):
    return hd + 2 * SUBLANES


def _moba_attn_kernel(hd, qa_ref, ka_ref, vt_ref, o_ref, acc_ref, sa_ref, sb_ref):
    t = pl.program_id(1)
    tq = qa_ref.shape[0]
    blk = MOBA_BLOCK
    nh = vt_ref.shape[1]

    def scores_to(dst_ref, n, hh):
        cols = slice(hh * LANES, (hh + 1) * LANES)
        rows = pl.ds(pl.multiple_of(n * blk, blk), blk)
        dst_ref[hh] = _dot_nt(ka_ref[rows, cols], qa_ref[:, cols])

    def update(s, m, n, hh):
        m_new = jnp.maximum(m, jnp.max(s, axis=0, keepdims=True))
        alpha = jnp.exp2(m - m_new)
        p = jnp.exp2(s - m_new).astype(BF16)
        acc_ref[hh] = alpha * acc_ref[hh] + _dot(vt_ref[0, hh, n], p)
        return m_new

    def step(cur_ref, nxt_ref, n, ms):
        new = []
        for hh in range(nh):
            scores_to(nxt_ref, n + 1, hh)
            new.append(update(cur_ref[hh], ms[hh], n, hh))
        return tuple(new)

    acc_ref[...] = jnp.zeros_like(acc_ref)
    for hh in range(nh):
        scores_to(sa_ref, 0, hh)

    def pair(i, ms):
        ms = step(sa_ref, sb_ref, 2 * i, ms)
        return step(sb_ref, sa_ref, 2 * i + 1, ms)

    m0 = jnp.full((1, tq), 2 * NEG, F32)
    ms = lax.fori_loop(0, t // 2, pair, (m0,) * nh)

    def odd_block(ms):
        new = []
        for hh in range(nh):
            new.append(update(sa_ref[hh], ms[hh], t - 1, hh))
            scores_to(sa_ref, t, hh)
        return tuple(new)

    ms = lax.cond(t % 2 == 1, odd_block, lambda ms: ms, ms)

    key = lax.broadcasted_iota(jnp.int32, (blk, tq), 0)
    qry = lax.broadcasted_iota(jnp.int32, (blk, tq), 1)
    outs = []
    for hh in range(nh):
        s = jnp.where(key <= qry, sa_ref[hh], NEG)
        update(s, ms[hh], t, hh)
        a = acc_ref[hh]
        outs.append(a[0:hd] / a[hd:hd + 1])
    o_ref[...] = jnp.concatenate(outs, axis=0).T.astype(o_ref.dtype)


def _moba_attn(qa, ka, vt, batch, seq, heads, hd):
    nt = seq // MOBA_BLOCK
    vt_rows = vt.shape[3]
    return pl.pallas_call(
        functools.partial(_moba_attn_kernel, hd),
        grid=(batch, nt),
        in_specs=[pl.BlockSpec((MOBA_BLOCK, heads * LANES), lambda b, t: (b * nt + t, 0)),
                  pl.BlockSpec((seq, heads * LANES), lambda b, t: (b, 0)),
                  pl.BlockSpec((1, heads, nt, vt_rows, MOBA_BLOCK), lambda b, t: (b, 0, 0, 0, 0))],
        out_specs=pl.BlockSpec((MOBA_BLOCK, heads * hd), lambda b, t: (b * nt + t, 0)),
        out_shape=jax.ShapeDtypeStruct((batch * seq, heads * hd), BF16),
        scratch_shapes=[pltpu.VMEM((heads, vt_rows, MOBA_BLOCK), F32),
                        pltpu.VMEM((heads, MOBA_BLOCK, MOBA_BLOCK), F32),
                        pltpu.VMEM((heads, MOBA_BLOCK, MOBA_BLOCK), F32)],
        compiler_params=_cparams("parallel", "arbitrary"),
        name="moba_attn",
    )(qa, ka, vt)


def _out_proj_kernel(widths, x_ref, oa_ref, ob_ref, oc_ref, w_ref, g_ref, wr_ref,
                     x1_ref, hx_ref, route_ref):
    rows = x_ref.shape[0]
    hw = x_ref.shape[1]
    chunks = [slice(r, r + OUT_PROJ_CHUNK) for r in range(0, rows, OUT_PROJ_CHUNK)]
    accs = []
    for c in chunks:
        acc = x_ref[c, :]
        r0 = 0
        for o_ref, wd in zip((oa_ref, ob_ref, oc_ref), widths):
            acc = acc + _dot(o_ref[c, :], w_ref[r0:r0 + wd, :])
            r0 += wd
        x1_ref[c, :] = acc
        accs.append(acc)
    wr_hi, wr_lo = _hi_lo(wr_ref[...])
    wr_cat = jnp.concatenate([wr_hi, wr_lo, wr_hi], axis=0)
    all_logits = []
    for c, acc in zip(chunks, accs):
        ms = jnp.mean(acc * acc, axis=-1, keepdims=True)
        h2 = acc * lax.rsqrt(ms + NORM_EPS) * g_ref[...]
        hx_ref[c, 0:hw] = h2
        h_hi, h_lo = _hi_lo(h2)
        all_logits.append(_dot(jnp.concatenate([h_hi, h_hi, h_lo], axis=1), wr_cat))
    for c, logits in zip(chunks, all_logits):
        route = _route(logits)
        hx_ref[c, hw:hw + LANES] = route
        route_ref[c, :] = route


def _route(logits):
    G, E = MOE_GROUPS, MOE_EXPERTS_PER_GROUP
    lane = lax.broadcasted_iota(jnp.int32, logits.shape, 1).astype(F32)
    big = float(LANES)
    is_g = lane < G
    lg = jnp.where(is_g, logits, -jnp.inf)
    mg = jnp.max(lg, axis=-1, keepdims=True)
    gate_group = 1.0 / jnp.sum(jnp.exp(lg - mg), axis=-1, keepdims=True)
    g_sel = jnp.min(jnp.where(lg == mg, lane, big), axis=-1, keepdims=True)
    lo = G + E * g_sel
    in_grp = (lane >= lo) & (lane < lo + E)
    v = jnp.where(in_grp, logits, -jnp.inf)
    v1 = jnp.max(v, axis=-1, keepdims=True)
    i1 = jnp.min(jnp.where(v == v1, lane, big), axis=-1, keepdims=True)
    v = jnp.where(lane == i1, -jnp.inf, v)
    v2 = jnp.max(v, axis=-1, keepdims=True)
    i2 = jnp.min(jnp.where(v == v2, lane, big), axis=-1, keepdims=True)
    e2 = jnp.exp(v2 - v1)
    w1 = gate_group / (1.0 + e2)
    w2 = gate_group * e2 / (1.0 + e2)
    first = i1 < i2
    a = jnp.minimum(i1, i2) - lo
    b = jnp.maximum(i1, i2) - lo
    w_lo = jnp.where(first, w1, w2)
    w_hi = jnp.where(first, w2, w1)
    pos = jnp.zeros_like(a)
    swap = a < 0.0
    for p, (ea, eb) in enumerate(_moe_pair_order()):
        hit = (a == float(min(ea, eb))) & (b == float(max(ea, eb)))
        pos = jnp.where(hit, float(p), pos)
        if ea > eb:
            swap = swap | hit
    cls = g_sel * float(E * (E - 1) // 2) + pos
    route = jnp.where(lane == 0.0, cls, 0.0)
    route = jnp.where(lane == 1.0, jnp.where(swap, w_hi, w_lo), route)
    return jnp.where(lane == 2.0, jnp.where(swap, w_lo, w_hi), route)


def _out_proj(x, oa, ob, oc, w, g, wr):
    n, d = x.shape
    tm = ROW_TILE
    widths = (oa.shape[1], ob.shape[1], oc.shape[1])
    row = lambda width: pl.BlockSpec((tm, width), lambda i: (i, 0))
    full = lambda a: pl.BlockSpec(a.shape, lambda i: (0, 0))
    g2 = g.reshape(1, d)
    hxw = d + LANES
    return pl.pallas_call(
        functools.partial(_out_proj_kernel, widths),
        grid=(n // tm,),
        in_specs=[row(d), row(widths[0]), row(widths[1]), row(widths[2]), full(w), full(g2), full(wr)],
        out_specs=[row(d), row(hxw), row(LANES)],
        out_shape=[jax.ShapeDtypeStruct((n, d), F32), jax.ShapeDtypeStruct((n, hxw), F32),
                   jax.ShapeDtypeStruct((n, LANES), F32)],
        compiler_params=_cparams("parallel"),
        name="out_proj",
    )(x, oa, ob, oc, w, g2, wr)


@functools.lru_cache(maxsize=None)
def _moe_pair_order():
    E = MOE_EXPERTS_PER_GROUP
    todo = {(a, b) for a in range(E) for b in range(a + 1, E)}

    def extend(seq, todo):
        if not todo:
            return seq
        for pr in sorted(todo):
            for cand in (pr, pr[::-1]):
                if not seq or cand[0] == seq[-1][0] or cand[1] == seq[-1][1]:
                    out = extend(seq + [cand], todo - {pr})
                    if out:
                        return out
        return None

    return tuple(extend([], todo))


def _moe_classes():
    G, E = MOE_GROUPS, MOE_EXPERTS_PER_GROUP
    ea = [g * E + a for g in range(G) for a, _ in _moe_pair_order()]
    eb = [g * E + b for g in range(G) for _, b in _moe_pair_order()]
    return ea, eb


def _moe_plan_kernel(elo, ehi, cls_ref, dest_ref, tab_ref):
    R = MOE_TILE
    cls = cls_ref[...]
    rows = cls.shape[0]
    ri = lax.broadcasted_iota(jnp.int32, (LANES, LANES), 0)
    ci = lax.broadcasted_iota(jnp.int32, (LANES, LANES), 1)
    before_lane = (ri < ci).astype(BF16)
    all_lane = jnp.ones((LANES, LANES), BF16)
    rr = lax.broadcasted_iota(jnp.int32, (rows, rows), 0)
    rc = lax.broadcasted_iota(jnp.int32, (rows, rows), 1)
    before_row = (rc < rr).astype(BF16)
    all_row = jnp.ones((rows, rows), BF16)
    tile_row0 = lax.broadcasted_iota(jnp.int32, (8, LANES), 1).astype(F32) * float(R)
    off = jnp.zeros((rows, LANES), F32)
    dest = jnp.zeros((rows, LANES), F32)
    tile_cls = jnp.zeros((8, LANES), F32)
    partial = jnp.zeros((8, LANES), F32)
    for k in range(len(elo)):
        mask = (cls == k).astype(BF16)
        row_tot = _dot(mask, all_lane)
        rank = _dot(mask, before_lane) + _dot(before_row, row_tot.astype(BF16))
        count = _dot(all_row, row_tot.astype(BF16))
        dest = dest + mask.astype(F32) * (off + rank)
        start = off
        off = off + jnp.floor((count + float(R - 1)) * (1.0 / R)) * float(R)
        tile_cls = tile_cls + jnp.where(tile_row0 >= off[0:8], 1.0, 0.0)
        partial = jnp.where((tile_row0 == off[0:8] - float(R)) & (off[0:8] > start[0:8]), 1.0, partial)
    partial = jnp.where(tile_row0 >= off[0:8], 1.0, partial)
    dest_ref[...] = dest.astype(jnp.int32)
    tile_cls = jnp.minimum(tile_cls, float(len(elo) - 1))
    t_lo = jnp.zeros((8, LANES), F32)
    t_hi = jnp.zeros((8, LANES), F32)
    for k in range(len(elo)):
        t_lo = jnp.where(tile_cls == k, float(elo[k]), t_lo)
        t_hi = jnp.where(tile_cls == k, float(ehi[k]), t_hi)
    sub = lax.broadcasted_iota(jnp.int32, (8, LANES), 0)
    tab = jnp.where(sub == 0, t_lo, jnp.where(sub == 1, t_hi, jnp.where(sub == 2, off[0:8] * (1.0 / R), partial)))
    tab_ref[...] = tab.astype(jnp.int32)


def _moe_plan(cls):
    elo, ehi = _moe_classes()
    rows = cls.shape[0]
    return pl.pallas_call(
        functools.partial(_moe_plan_kernel, elo, ehi),
        out_shape=[jax.ShapeDtypeStruct((rows, LANES), jnp.int32),
                   jax.ShapeDtypeStruct((8, LANES), jnp.int32)],
        name="moe_plan",
    )(cls)


def _row_copies(n, make):
    def start(g, c):
        base = pl.multiple_of(g * SUBLANES, SUBLANES)
        for s in range(SUBLANES):
            make(base + s).start()
        return c

    def wait(g, c):
        base = pl.multiple_of(g * SUBLANES, SUBLANES)
        for s in range(SUBLANES):
            make(base + s).wait()
        return c

    lax.fori_loop(0, n // SUBLANES, start, 0)
    lax.fori_loop(0, n // SUBLANES, wait, 0)


def _moe_dispatch_kernel(partial_ref, dest_ref, hx_ref, xs_ref, zero_ref, zsem, sem):
    R = zero_ref.shape[0]

    @pl.when(pl.program_id(0) == 0)
    def _():
        zero_ref[...] = jnp.zeros_like(zero_ref)
        fill = lambda j: pltpu.make_async_copy(zero_ref, xs_ref.at[pl.ds(j * R, R)], zsem)
        for j in range(xs_ref.shape[0] // R):
            @pl.when(partial_ref[j] == 1)
            def _():
                fill(j).start()
        for j in range(xs_ref.shape[0] // R):
            @pl.when(partial_ref[j] == 1)
            def _():
                fill(j).wait()

    _row_copies(hx_ref.shape[0], lambda r: pltpu.make_async_copy(
        hx_ref.at[pl.ds(r, 1)], xs_ref.at[pl.ds(dest_ref[r], 1)], sem))


def _moe_dispatch(partial, dest, hx, n_rows):
    n, w = hx.shape
    tm = MOE_COPY_ROWS
    return pl.pallas_call(
        _moe_dispatch_kernel,
        grid=(n // tm,),
        in_specs=[pl.BlockSpec(partial.shape, lambda i: (0,), memory_space=pltpu.SMEM),
                  pl.BlockSpec((tm,), lambda i: (i,), memory_space=pltpu.SMEM),
                  pl.BlockSpec((tm, w), lambda i: (i, 0))],
        out_specs=pl.BlockSpec(memory_space=pl.ANY),
        out_shape=jax.ShapeDtypeStruct((n_rows, w), hx.dtype),
        scratch_shapes=[pltpu.VMEM((MOE_TILE, w), hx.dtype), pltpu.SemaphoreType.DMA,
                        pltpu.SemaphoreType.DMA],
        compiler_params=_cparams("arbitrary"),
        name="moe_dispatch",
    )(partial, dest, hx)


def _moe_experts_kernel(elo_ref, ehi_ref, nt_ref, xs_ref, wg0, wu0, wd0, wg1, wu1, wd1, y_ref,
                        wgu_ref, wdn_ref):
    j = pl.program_id(0)
    hw = xs_ref.shape[1] - LANES
    jp = jnp.maximum(j - 1, 0)
    fresh = (j == 0) | (elo_ref[j] != elo_ref[jp]) | (ehi_ref[j] != ehi_ref[jp])

    @pl.when((j < nt_ref[0]) & fresh)
    def _():
        for slot, (wg, wu, wd) in enumerate(((wg0, wu0, wd0), (wg1, wu1, wd1))):
            wgu_ref[slot, 0] = wg[0].astype(BF16)
            wgu_ref[slot, 1] = wu[0].astype(BF16)
            wdn_ref[slot] = wd[0].astype(BF16)

    @pl.when(j < nt_ref[0])
    def _():
        x = xs_ref[:, 0:hw].astype(BF16)
        route = xs_ref[:, hw:hw + LANES]
        lane = lax.broadcasted_iota(jnp.int32, route.shape, 1)
        y = None
        for slot in range(2):
            wt = jnp.sum(jnp.where(lane == 1 + slot, route, 0.0), axis=-1, keepdims=True)
            a = _dot(x, wgu_ref[slot, 0])
            u = _dot(x, wgu_ref[slot, 1])
            he = (a * _sigmoid(a)) * u * wt
            part = _dot(he.astype(BF16), wdn_ref[slot])
            y = part if y is None else y + part
        y_ref[...] = y

    @pl.when(j >= nt_ref[0])
    def _():
        y_ref[...] = jnp.zeros_like(y_ref)


def _moe_experts(tab, xs, wg, wu, wd, layer):
    n_rows, w = xs.shape
    _, ne, d, ff = wg.shape
    R = MOE_TILE
    last = lambda j, nt: jnp.minimum(j, nt[0] - 1)
    wspec = lambda shape, which: pl.BlockSpec(
        (None,) + shape, lambda j, elo, ehi, nt: (layer, (elo, ehi)[which][last(j, nt)], 0, 0))
    grid_spec = pltpu.PrefetchScalarGridSpec(
        num_scalar_prefetch=3,
        grid=(n_rows // R,),
        in_specs=[pl.BlockSpec((R, w), lambda j, elo, ehi, nt: (last(j, nt), 0)),
                  wspec((1, d, ff), 0), wspec((1, d, ff), 0), wspec((1, ff, d), 0),
                  wspec((1, d, ff), 1), wspec((1, d, ff), 1), wspec((1, ff, d), 1)],
        out_specs=pl.BlockSpec((R, d), lambda j, elo, ehi, nt: (j, 0)),
        scratch_shapes=[pltpu.VMEM((2, 2, d, ff), BF16), pltpu.VMEM((2, ff, d), BF16)],
    )
    return pl.pallas_call(
        _moe_experts_kernel,
        grid_spec=grid_spec,
        out_shape=jax.ShapeDtypeStruct((n_rows, d), F32),
        compiler_params=_cparams("arbitrary"),
        name="moe_experts",
    )(tab[0], tab[1], tab[2, 0:1], xs, wg, wu, wd, wg, wu, wd)


def _moe_combine_kernel(dest_ref, x1_ref, y_ref, o_ref, buf_ref, sem):
    _row_copies(buf_ref.shape[0], lambda r: pltpu.make_async_copy(
        y_ref.at[pl.ds(dest_ref[r], 1)], buf_ref.at[pl.ds(r, 1)], sem))
    o_ref[...] = x1_ref[...] + buf_ref[...]


def _moe_combine(dest, x1, y):
    n, d = x1.shape
    tm = MOE_COPY_ROWS
    return pl.pallas_call(
        _moe_combine_kernel,
        grid=(n // tm,),
        in_specs=[pl.BlockSpec((tm,), lambda i: (i,), memory_space=pltpu.SMEM),
                  pl.BlockSpec((tm, d), lambda i: (i, 0)),
                  pl.BlockSpec(memory_space=pl.ANY)],
        out_specs=pl.BlockSpec((tm, d), lambda i: (i, 0)),
        out_shape=jax.ShapeDtypeStruct((n, d), F32),
        scratch_shapes=[pltpu.VMEM((tm, d), F32), pltpu.SemaphoreType.DMA],
        compiler_params=_cparams("arbitrary"),
        name="moe_combine",
    )(dest, x1, y)


def _moe(x1, hx, route, wg, wu, wd, layer):
    n, d = x1.shape
    elo, _ = _moe_classes()
    n_rows = n + len(elo) * MOE_TILE
    assert n % LANES == 0 and n_rows // MOE_TILE <= LANES
    cls = route[:, 0].astype(jnp.int32).reshape(n // LANES, LANES)
    dest, tab = _moe_plan(cls)
    dest = dest.reshape(n)
    xs = _moe_dispatch(tab[3], dest, hx, n_rows)
    y = _moe_experts(tab, xs, wg, wu, wd, layer)
    return _moe_combine(dest, x1, y)


def _pad_cols(w, width):
    return jnp.pad(w, ((0, 0), (0, width - w.shape[1])))


def kernel(x, attn_norm_g, w_in, gla_gk_w2, gla_gk_b, gla_norm_g, moba_qnorm_g, moba_knorm_g,
           hgrn_lb_param, hgrn_norm_g, w_out, ffn_norm_g, w_router_group, w_router_expert,
           w_exp_gate, w_exp_up, w_exp_down):
    batch, seq, d = x.shape
    depth = w_in.shape[0]
    lowrank, gla_kw = gla_gk_w2.shape[1:]
    gla_dv = gla_norm_g.shape[1]
    moba_hd = moba_qnorm_g.shape[1]
    hgrn_kw = hgrn_lb_param.shape[1]
    hgrn_dv = hgrn_norm_g.shape[1]
    mix_w = w_out.shape[1]
    hgrn_heads = 4
    gla_heads = 4
    gla_vw = gla_heads * gla_dv
    hgrn_vw = hgrn_heads * hgrn_dv
    moba_w = mix_w - gla_vw - hgrn_vw
    moba_heads = moba_w // moba_hd
    gla_dk = gla_kw // gla_heads
    hgrn_dk = hgrn_kw // hgrn_heads

    splits = (gla_kw, gla_kw, gla_vw, gla_vw, lowrank, moba_w, moba_w, moba_w,
              hgrn_kw, hgrn_kw, hgrn_vw, hgrn_vw)
    offs = np.concatenate([[0], np.cumsum(splits)]).tolist()
    order = (0, 1, 2, 3, 4, 5, 6, 7, 8, 10, 11, 9)
    pieces, dst = [], 0
    for i in order:
        padded = -(-splits[i] // LANES) * LANES
        pieces.append((offs[i], splits[i], dst, padded))
        dst += padded
    g_gla = 2 * gla_kw + 2 * gla_vw
    g_hg = hgrn_kw + 2 * hgrn_vw
    bounds = np.cumsum([0, g_gla, LANES, 3 * moba_w, g_hg, hgrn_kw]).tolist()
    spans = [(bounds[i], bounds[i + 1]) for i in range(5)]
    groups = (spans[0], spans[1], spans[3], spans[4])
    assert dst == bounds[-1]

    w_in_t = jnp.swapaxes(w_in, 1, 2)
    xf = x.reshape(batch * seq, d)
    for l in range(depth):
        y_gla, y_lr, y_hg, y_f, qa, ka, vt = _in_proj(
            xf, attn_norm_g[l], w_in_t, l, tuple(pieces), groups, (BF16, F32, BF16, F32), spans[2],
            moba_qnorm_g[l], moba_knorm_g[l], batch, seq, moba_heads, moba_hd)

        w2 = jnp.pad(gla_gk_w2[l], ((0, LANES - lowrank), (0, 0)))
        o_a, o_c = _lin_call(
            batch, seq, l, (gla_heads, gla_dk, gla_dv), (hgrn_heads, hgrn_dk, hgrn_dv),
            (y_gla, y_lr, y_hg, y_f),
            (w2, gla_gk_b[l].reshape(1, gla_kw), jnp.tile(gla_norm_g[l], gla_heads).reshape(1, gla_vw),
             hgrn_lb_param, jnp.tile(hgrn_norm_g[l], hgrn_heads).reshape(1, hgrn_vw)))

        o_b = _moba_attn(qa, ka, vt, batch, seq, moba_heads, moba_hd)

        wr = _pad_cols(jnp.concatenate([w_router_group[l], w_router_expert[l]], axis=1), LANES)
        x1, hx, route = _out_proj(xf, o_a, o_b, o_c, w_out[l].astype(BF16), ffn_norm_g[l], wr)
        xf = _moe(x1, hx, route, w_exp_gate, w_exp_up, w_exp_down, l)
    return xf.reshape(batch, seq, d)
```
